```python
import jax, jax.numpy as jnp
from jax import lax
import numpy as np

D_MODEL = 1024
BATCH = 2
SEQ = 8192
DEPTH = 2

MEM_LEN = 256
HEAD_DIM = D_MODEL // 16
H_NA = 6
H_DIL = 6
H_MEM = 4
W_NA = H_NA * HEAD_DIM
W_DIL = H_DIL * HEAD_DIM
W_MEM = H_MEM * HEAD_DIM
MIX_WIDTH = W_NA + W_DIL + W_MEM
IN_WIDTH = 3 * W_NA + 3 * W_DIL + W_MEM
GRID_W = 64
NA_KH = 8
NA_KW = 16
NA_QROWS = 2
RPB_H = 2 * NA_KH - 1
RPB_W = 2 * NA_KW - 1
DIL_CFG = ((128, 1), (512, 4), (2048, 16))
DIL_QBLOCK = 128
ROPE_THETA = 10000.0
D_FF = 2816
N_EXPERTS = 8
TOP_K = 2
D_FF_EXPERT = 3584
MOE_BLOCK = 256
N_DENSE = (DEPTH + 1) // 2
N_MOE = DEPTH // 2
RMS_EPS = 1e-6
NEG_INF = -1e30

kernel_name = "hybrid_na_dilated_mem_moe_encoder"


def rms_norm(x, g):
    xf = x.astype(jnp.float32)
    y = xf * lax.rsqrt(jnp.mean(xf * xf, axis=-1, keepdims=True) + RMS_EPS)
    return (y * g.astype(jnp.float32)).astype(x.dtype)


def apply_rope(x):
    T = x.shape[1]
    half = HEAD_DIM // 2
    inv_freq = jnp.power(ROPE_THETA, -(2.0 / HEAD_DIM) * jnp.arange(half, dtype=jnp.float32))
    ang = jnp.arange(T, dtype=jnp.float32)[:, None] * inv_freq[None, :]
    cos = jnp.cos(ang)[None, :, None, :]
    sin = jnp.sin(ang)[None, :, None, :]
    xf = x.astype(jnp.float32)
    x1, x2 = xf[..., :half], xf[..., half:]
    return jnp.concatenate([x1 * cos - x2 * sin, x1 * sin + x2 * cos], axis=-1).astype(x.dtype)


def neighborhood_attention(q, k, v, rpb):
    B, T, H, Dh = q.shape
    rows = T // GRID_W
    kh = min(NA_KH, rows)
    kw = NA_KW
    r = jnp.arange(rows, dtype=jnp.int32)
    row_idx = jnp.clip(r - kh // 2, 0, rows - kh)[:, None] + jnp.arange(kh, dtype=jnp.int32)[None, :]
    c = jnp.arange(GRID_W, dtype=jnp.int32)
    col_idx = jnp.clip(c - kw // 2, 0, GRID_W - kw)[:, None] + jnp.arange(kw, dtype=jnp.int32)[None, :]
    row_rel = row_idx - r[:, None] + (NA_KH - 1)
    col_rel = col_idx - c[:, None] + (NA_KW - 1)
    tok_idx = row_idx[:, None, :, None] * GRID_W + col_idx[None, :, None, :]
    bias_idx = row_rel[:, None, :, None] * RPB_W + col_rel[None, :, None, :]
    rpb_flat = rpb.reshape(H, RPB_H * RPB_W)
    qg = q.reshape(B, rows, GRID_W, H, Dh)
    scale = Dh ** -0.5
    n_blk = rows // NA_QROWS
    tok_blocks = tok_idx.reshape(n_blk, NA_QROWS, GRID_W, kh, kw)
    bias_blocks = bias_idx.reshape(n_blk, NA_QROWS, GRID_W, kh, kw)
    q_blocks = jnp.moveaxis(qg.reshape(B, n_blk, NA_QROWS, GRID_W, H, Dh), 1, 0)

    def row_block(args):
        q_blk, ti, bi = args
        k_nb = jnp.take(k, ti, axis=1)
        v_nb = jnp.take(v, ti, axis=1)
        logits = jnp.einsum('brchd,brcijhd->bhrcij', q_blk, k_nb).astype(jnp.float32) * scale
        bias = jnp.take(rpb_flat, bi, axis=1)
        logits = logits + bias[None].astype(jnp.float32)
        shp = logits.shape
        p = jax.nn.softmax(logits.reshape(shp[:4] + (kh * kw,)), axis=-1).reshape(shp)
        return jnp.einsum('bhrcij,brcijhd->brchd', p.astype(v.dtype), v_nb)

    outs = lax.map(row_block, (q_blocks, tok_blocks, bias_blocks))
    return jnp.moveaxis(outs, 0, 1).reshape(B, T, H, Dh)


def dilated_mixture_attention(q, k, v):
    B, T, H, Dh = q.shape
    scale = Dh ** -0.5
    offsets = []
    for w, d in DIL_CFG:
        n_side = (w // 2) // d
        offsets.append(d * jnp.arange(-n_side, n_side + 1, dtype=jnp.int32))
    n_blk = T // DIL_QBLOCK
    q_blocks = jnp.moveaxis(q.reshape(B, n_blk, DIL_QBLOCK, H, Dh), 1, 0)
    starts = jnp.arange(n_blk, dtype=jnp.int32) * DIL_QBLOCK

    def q_block(args):
        q_blk, t0 = args
        t = t0 + jnp.arange(DIL_QBLOCK, dtype=jnp.int32)
        outs, lses = [], []
        for off in offsets:
            pos = t[:, None] + off[None, :]
            valid = (pos >= 0) & (pos < T)
            pc = jnp.clip(pos, 0, T - 1)
            kk = jnp.take(k, pc, axis=1)
            vv = jnp.take(v, pc, axis=1)
            s = jnp.einsum('bqhd,bqkhd->bhqk', q_blk, kk).astype(jnp.float32) * scale
            s = jnp.where(valid[None, None], s, NEG_INF)
            m = jnp.max(s, axis=-1, keepdims=True)
            p = jnp.exp(s - m)
            l = jnp.sum(p, axis=-1, keepdims=True)
            o = jnp.einsum('bhqk,bqkhd->bhqd', (p / l).astype(v.dtype), vv)
            outs.append(o.astype(jnp.float32))
            lses.append((m + jnp.log(l))[..., 0])
        wts = jax.nn.softmax(jnp.stack(lses), axis=0)
        o = jnp.sum(wts[..., None] * jnp.stack(outs), axis=0)
        return jnp.transpose(o, (0, 2, 1, 3)).astype(q.dtype)

    outs = lax.map(q_block, (q_blocks, starts))
    return jnp.moveaxis(outs, 0, 1).reshape(B, T, H, Dh)


def memory_attention(q, km, vm):
    s = jnp.einsum('bthd,bmhd->bhtm', q, km).astype(jnp.float32) * (HEAD_DIM ** -0.5)
    p = jax.nn.softmax(s, axis=-1)
    return jnp.einsum('bhtm,bmhd->bthd', p.astype(vm.dtype), vm)


def swiglu(x, w_gate, w_up, w_down):
    return (jax.nn.silu(x @ w_gate) * (x @ w_up)) @ w_down


def moe_swiglu(x, w_router, w_gate, w_up, w_down):
    B, T, D = x.shape
    N = B * T
    NK = N * TOP_K
    xf = x.reshape(N, D)
    logits = (xf @ w_router).astype(jnp.float32)
    top_logit, top_idx = lax.top_k(logits, TOP_K)
    gates = jax.nn.softmax(top_logit, axis=-1)
    e_flat = top_idx.reshape(NK).astype(jnp.int32)
    tok_flat = jnp.broadcast_to(jnp.arange(N, dtype=jnp.int32)[:, None], (N, TOP_K)).reshape(NK)
    g_flat = gates.reshape(NK)
    order = jnp.argsort(e_flat)
    e_s = e_flat[order]
    tok_s = tok_flat[order]
    g_s = g_flat[order]
    counts = jnp.sum((e_flat[:, None] == jnp.arange(N_EXPERTS, dtype=jnp.int32)[None, :]).astype(jnp.int32), axis=0)
    padded = (counts + MOE_BLOCK - 1) // MOE_BLOCK * MOE_BLOCK
    start = jnp.cumsum(counts) - counts
    pend = jnp.cumsum(padded)
    pstart = pend - padded
    dest = pstart[e_s] + (jnp.arange(NK, dtype=jnp.int32) - start[e_s])
    P = NK + N_EXPERTS * MOE_BLOCK
    row_tok = jnp.full((P,), N, jnp.int32).at[dest].set(tok_s)
    row_gate = jnp.zeros((P,), jnp.float32).at[dest].set(g_s)
    n_blk = P // MOE_BLOCK
    blk_start = jnp.arange(n_blk, dtype=jnp.int32) * MOE_BLOCK
    blk_expert = jnp.minimum(
        jnp.sum((blk_start[:, None] >= pend[None, :]).astype(jnp.int32), axis=1),
        N_EXPERTS - 1)
    x_pad = jnp.concatenate([xf, jnp.zeros((1, D), xf.dtype)], axis=0)
    x_rows = jnp.take(x_pad, row_tok, axis=0).reshape(n_blk, MOE_BLOCK, D)

    def expert_block(args):
        xb, e = args
        return swiglu(xb, w_gate[e], w_up[e], w_down[e])

    y = lax.map(expert_block, (x_rows, blk_expert)).reshape(P, D)
    y = y * row_gate[:, None].astype(y.dtype)
    out = jax.ops.segment_sum(y, row_tok, num_segments=N + 1)[:N]
    return out.reshape(B, T, D)


def setup_inputs(seed: int = 0) -> dict:
    key = jax.random.key(seed)
    ks = jax.random.split(key, 24)
    f32 = jnp.float32

    def nrm(k, shape, scale):
        return jax.random.normal(k, shape, f32) * scale

    def gain(k, shape):
        return 1.0 + 0.05 * jax.random.normal(k, shape, f32)

    return {
        "x": nrm(ks[0], (BATCH, SEQ, D_MODEL), 1.0),
        "mem": nrm(ks[1], (BATCH, MEM_LEN, D_MODEL), 1.0),
        "g_attn": gain(ks[2], (DEPTH, D_MODEL)),
        "w_in": nrm(ks[3], (DEPTH, D_MODEL, IN_WIDTH), D_MODEL ** -0.5),
        "g_qk_na": gain(ks[4], (DEPTH, 2, HEAD_DIM)),
        "rpb_na": nrm(ks[5], (DEPTH, H_NA, RPB_H, RPB_W), 0.1),
        "g_qk_dil": gain(ks[6], (DEPTH, 2, HEAD_DIM)),
        "g_mem": gain(ks[7], (DEPTH, D_MODEL)),
        "w_mem_kv": nrm(ks[8], (DEPTH, D_MODEL, 2 * W_MEM), D_MODEL ** -0.5),
        "g_qk_mem": gain(ks[9], (DEPTH, 2, HEAD_DIM)),
        "g_out": gain(ks[10], (DEPTH, MIX_WIDTH)),
        "w_out": nrm(ks[11], (DEPTH, MIX_WIDTH, D_MODEL), MIX_WIDTH ** -0.5),
        "g_ffn": gain(ks[12], (DEPTH, D_MODEL)),
        "w_gate_dense": nrm(ks[13], (N_DENSE, D_MODEL, D_FF), D_MODEL ** -0.5),
        "w_up_dense": nrm(ks[14], (N_DENSE, D_MODEL, D_FF), D_MODEL ** -0.5),
        "w_down_dense": nrm(ks[15], (N_DENSE, D_FF, D_MODEL), D_FF ** -0.5),
        "w_router": nrm(ks[16], (N_MOE, D_MODEL, N_EXPERTS), D_MODEL ** -0.5),
        "w_gate_moe": nrm(ks[17], (N_MOE, N_EXPERTS, D_MODEL, D_FF_EXPERT), D_MODEL ** -0.5),
        "w_up_moe": nrm(ks[18], (N_MOE, N_EXPERTS, D_MODEL, D_FF_EXPERT), D_MODEL ** -0.5),
        "w_down_moe": nrm(ks[19], (N_MOE, N_EXPERTS, D_FF_EXPERT, D_MODEL), D_FF_EXPERT ** -0.5),
    }


def reference(x, mem, g_attn, w_in, g_qk_na, rpb_na, g_qk_dil, g_mem, w_mem_kv, g_qk_mem,
              g_out, w_out, g_ffn, w_gate_dense, w_up_dense, w_down_dense, w_router,
              w_gate_moe, w_up_moe, w_down_moe):
    B, T, _ = x.shape
    M = mem.shape[1]
    o1 = W_NA
    o2 = 2 * W_NA
    o3 = 3 * W_NA
    o4 = o3 + W_DIL
    o5 = o3 + 2 * W_DIL
    o6 = o3 + 3 * W_DIL
    h = x
    for layer in range(DEPTH):
        u = rms_norm(h, g_attn[layer])
        proj = u @ w_in[layer]
        qa = proj[..., :o1]
        ka = proj[..., o1:o2]
        va = proj[..., o2:o3]
        qb = proj[..., o3:o4]
        kb = proj[..., o4:o5]
        vb = proj[..., o5:o6]
        qm = proj[..., o6:]

        qa = rms_norm(qa.reshape(B, T, H_NA, HEAD_DIM), g_qk_na[layer, 0])
        ka = rms_norm(ka.reshape(B, T, H_NA, HEAD_DIM), g_qk_na[layer, 1])
        va = va.reshape(B, T, H_NA, HEAD_DIM)
        o_na = neighborhood_attention(qa, ka, va, rpb_na[layer])

        qb = apply_rope(rms_norm(qb.reshape(B, T, H_DIL, HEAD_DIM), g_qk_dil[layer, 0]))
        kb = apply_rope(rms_norm(kb.reshape(B, T, H_DIL, HEAD_DIM), g_qk_dil[layer, 1]))
        vb = vb.reshape(B, T, H_DIL, HEAD_DIM)
        o_dil = dilated_mixture_attention(qb, kb, vb)

        mem_n = rms_norm(mem, g_mem[layer])
        kv_m = mem_n @ w_mem_kv[layer]
        km = rms_norm(kv_m[..., :W_MEM].reshape(B, M, H_MEM, HEAD_DIM), g_qk_mem[layer, 1])
        vm = kv_m[..., W_MEM:].reshape(B, M, H_MEM, HEAD_DIM)
        qm = rms_norm(qm.reshape(B, T, H_MEM, HEAD_DIM), g_qk_mem[layer, 0])
        o_mem = memory_attention(qm, km, vm)

        go = g_out[layer]
        mixed = jnp.concatenate([
            rms_norm(o_na.reshape(B, T, W_NA), go[:W_NA]),
            rms_norm(o_dil.reshape(B, T, W_DIL), go[W_NA:W_NA + W_DIL]),
            rms_norm(o_mem.reshape(B, T, W_MEM), go[W_NA + W_DIL:]),
        ], axis=-1)
        h = h + mixed @ w_out[layer]

        u = rms_norm(h, g_ffn[layer])
        i = layer // 2
        if layer % 2 == 0:
            f = swiglu(u, w_gate_dense[i], w_up_dense[i], w_down_dense[i])
        else:
            f = moe_swiglu(u, w_router[i], w_gate_moe[i], w_up_moe[i], w_down_moe[i])
        h = h + f
    return h
```

```python
import functools

import numpy as np
import jax
import jax.numpy as jnp
from jax import lax
from jax.experimental import pallas as pl
from jax.experimental.pallas import tpu as pltpu

F32 = jnp.float32
BF16 = jnp.bfloat16

D_MODEL = 1024
HEAD_DIM = 64
H_NA, H_DIL, H_MEM = 6, 6, 4
W_NA, W_DIL, W_MEM = H_NA * HEAD_DIM, H_DIL * HEAD_DIM, H_MEM * HEAD_DIM
IN_WIDTH = 3 * W_NA + 3 * W_DIL + W_MEM
GRID_W = 64
NA_KH, NA_KW = 8, 16
DIL_CFG = ((128, 1), (512, 4), (2048, 16))
ROPE_THETA = 10000.0
N_EXPERTS = 8
TOP_K = 2
RMS_EPS = 1e-6
NEG_INF = -1e30

LANES = 128
TM_PROJ = 512
NA_QROWS = 4
NA_QB = NA_QROWS * GRID_W
NA_KROWS = 12
NA_KB = NA_KROWS * GRID_W
DIL_QB = 256
DIL_HALF = 64
DIL_KB = DIL_QB + 2 * DIL_HALF
TQ_MEM = 512
TM_FFN = 1024
TF_DENSE = 256
TM_MOE = 1024
TF_MOE = 512
TC_COMBINE = 512
VMEM_LIMIT = 56 * 1024 * 1024


def _cparams(sem):
    return pltpu.CompilerParams(dimension_semantics=sem, vmem_limit_bytes=VMEM_LIMIT)


def _rms(x, g):
    return x * lax.rsqrt(jnp.mean(x * x, axis=-1, keepdims=True) + RMS_EPS) * g


_CHUNKS = (
    [(0, c, True, False) for c in range(3)] + [(1, c, True, False) for c in range(3)]
    + [(2, c, False, False) for c in range(3)]
    + [(3, c, True, True) for c in range(3)] + [(4, c, True, True) for c in range(3)]
    + [(5, c, False, False) for c in range(3)]
    + [(6, c, True, False) for c in range(2)]
)


def _inproj_kernel(h_ref, ga_ref, w_ref, gq_ref, cos_ref, sin_ref, seg_ref, *out_refs):
    u = _rms(h_ref[...], ga_ref[...]).astype(BF16)
    p = jnp.dot(u, w_ref[...], preferred_element_type=F32)
    seg = seg_ref[...]
    lane = lax.broadcasted_iota(jnp.int32, (1, LANES), 1)
    first_half = (lane % HEAD_DIM) < (HEAD_DIM // 2)
    for c, (oi, oc, norm, rope) in enumerate(_CHUNKS):
        x = p[:, c * LANES:(c + 1) * LANES]
        if norm:
            ss = jnp.dot((x * x).astype(BF16), seg, preferred_element_type=F32)
            x = x * lax.rsqrt(ss * (1.0 / HEAD_DIM) + RMS_EPS) * gq_ref[:, c * LANES:(c + 1) * LANES]
        if rope:
            swapped = jnp.where(first_half, pltpu.roll(x, LANES - HEAD_DIM // 2, 1),
                                pltpu.roll(x, HEAD_DIM // 2, 1))
            x = x * cos_ref[...] + swapped * sin_ref[...]
        out_refs[oi][:, oc * LANES:(oc + 1) * LANES] = x.astype(BF16)


def _inproj(h, g_attn, w_in_bf16, gq_row, cos_t, sin_t, seg, seq_len):
    n = h.shape[0]
    tm = TM_PROJ
    tblocks = seq_len // tm
    widths = (W_NA, W_NA, W_NA, W_DIL, W_DIL, W_DIL, W_MEM)
    return pl.pallas_call(
        _inproj_kernel,
        grid=(n // tm,),
        in_specs=[
            pl.BlockSpec((tm, D_MODEL), lambda i: (i, 0)),
            pl.BlockSpec((1, D_MODEL), lambda i: (0, 0)),
            pl.BlockSpec((D_MODEL, IN_WIDTH), lambda i: (0, 0)),
            pl.BlockSpec((1, IN_WIDTH), lambda i: (0, 0)),
            pl.BlockSpec((tm, LANES), lambda i: (i % tblocks, 0)),
            pl.BlockSpec((tm, LANES), lambda i: (i % tblocks, 0)),
            pl.BlockSpec((LANES, LANES), lambda i: (0, 0)),
        ],
        out_specs=[pl.BlockSpec((tm, w), lambda i: (i, 0)) for w in widths],
        out_shape=[jax.ShapeDtypeStruct((n, w), BF16) for w in widths],
        compiler_params=_cparams(("parallel",)),
        name="inproj",
    )(h, g_attn, w_in_bf16, gq_row, cos_t, sin_t, seg)


def _pair_attention(q, k, v, bias_fn, want_lse):
    lane = lax.broadcasted_iota(jnp.int32, (1, LANES), 1)
    out = None
    lse_out = None
    for hh in range(2):
        mine = (lane < HEAD_DIM) if hh == 0 else (lane >= HEAD_DIM)
        qh = jnp.where(mine, q, jnp.zeros_like(q))
        s = lax.dot_general(qh, k, (((1,), (1,)), ((), ())), preferred_element_type=F32)
        b = bias_fn(hh)
        if b is not None:
            s = s + b
        m = jnp.max(s, axis=-1, keepdims=True)
        p = jnp.exp(s - m)
        l = jnp.sum(p, axis=-1, keepdims=True)
        o = jnp.dot(p.astype(BF16), v, preferred_element_type=F32) / l
        out = o if out is None else jnp.where(mine, o, out)
        if want_lse:
            lse = jnp.broadcast_to(m + jnp.log(l), o.shape)
            lse_out = lse if lse_out is None else jnp.where(mine, lse, lse_out)
    return out, lse_out


def _na_kernel(q_ref, k_ref, v_ref, bias_ref, o_ref, *, rows):
    i = pl.program_id(2)
    srow = jnp.clip(i * NA_QROWS - NA_KH // 2, 0, rows - NA_KROWS)
    start = pl.multiple_of(srow * GRID_W, GRID_W)
    k = k_ref[pl.ds(start, NA_KB), :]
    v = v_ref[pl.ds(start, NA_KB), :]
    o, _ = _pair_attention(q_ref[...], k, v, lambda hh: bias_ref[0, hh], False)
    o_ref[...] = o


def _na_attention(q, k, v, bias, batch, seq_len):
    rows = seq_len // GRID_W
    nblk = seq_len // NA_QB
    q3, k3, v3 = (a.reshape(batch, seq_len, W_NA) for a in (q, k, v))

    def variant(i):
        return (i > 0).astype(jnp.int32) + (i == nblk - 1).astype(jnp.int32)

    out = pl.pallas_call(
        functools.partial(_na_kernel, rows=rows),
        grid=(batch, W_NA // LANES, nblk),
        in_specs=[
            pl.BlockSpec((None, NA_QB, LANES), lambda b, p, i: (b, i, p)),
            pl.BlockSpec((None, seq_len, LANES), lambda b, p, i: (b, 0, p)),
            pl.BlockSpec((None, seq_len, LANES), lambda b, p, i: (b, 0, p)),
            pl.BlockSpec((1, 2, NA_QB, NA_KB), lambda b, p, i: (variant(i), p, 0, 0)),
        ],
        out_specs=pl.BlockSpec((None, NA_QB, LANES), lambda b, p, i: (b, i, p)),
        out_shape=jax.ShapeDtypeStruct((batch, seq_len, W_NA), F32),
        compiler_params=_cparams(("parallel", "parallel", "arbitrary")),
        name="na_attention",
    )(q3, k3, v3, bias)
    return out.reshape(batch * seq_len, W_NA)


def _na_bias_tables(rpb, rows):
    nblk = rows // NA_QROWS
    tables = []
    for blk in (0, 1, nblk - 1):
        r0 = blk * NA_QROWS
        srow = int(np.clip(r0 - NA_KH // 2, 0, rows - NA_KROWS))
        qr = r0 + np.arange(NA_QROWS)[:, None].repeat(GRID_W, 1).reshape(-1)
        qc = np.tile(np.arange(GRID_W), NA_QROWS)
        kr = srow + np.arange(NA_KROWS)[:, None].repeat(GRID_W, 1).reshape(-1)
        kc = np.tile(np.arange(GRID_W), NA_KROWS)
        rs = np.clip(qr - NA_KH // 2, 0, rows - NA_KH)[:, None]
        cs = np.clip(qc - NA_KW // 2, 0, GRID_W - NA_KW)[:, None]
        valid = ((kr[None, :] >= rs) & (kr[None, :] < rs + NA_KH)
                 & (kc[None, :] >= cs) & (kc[None, :] < cs + NA_KW))
        rr = np.clip(kr[None, :] - qr[:, None] + (NA_KH - 1), 0, 2 * NA_KH - 2)
        cr = np.clip(kc[None, :] - qc[:, None] + (NA_KW - 1), 0, 2 * NA_KW - 2)
        vals = rpb[:, rr, cr]
        tables.append(jnp.where(jnp.asarray(valid)[None], vals, NEG_INF))
    return jnp.stack(tables).astype(F32)


def _band_kernel(q_ref, k_ref, v_ref, mask_ref, o_ref, lse_ref, *, length):
    j = pl.program_id(2)
    start = jnp.clip(j * DIL_QB - DIL_HALF, 0, length - DIL_KB)
    start = pl.multiple_of(start, DIL_HALF)
    k = k_ref[pl.ds(start, DIL_KB), :]
    v = v_ref[pl.ds(start, DIL_KB), :]
    o, lse = _pair_attention(q_ref[...], k, v, lambda hh: mask_ref[0], True)
    o_ref[...] = o
    lse_ref[...] = lse


def _band_attention(q, k, v, masks):
    nseq, length, _ = q.shape
    nblk = length // DIL_QB

    def variant(j):
        return (j > 0).astype(jnp.int32) + (j == nblk - 1).astype(jnp.int32)

    return pl.pallas_call(
        functools.partial(_band_kernel, length=length),
        grid=(nseq, W_DIL // LANES, nblk),
        in_specs=[
            pl.BlockSpec((None, DIL_QB, LANES), lambda s, p, j: (s, j, p)),
            pl.BlockSpec((None, length, LANES), lambda s, p, j: (s, 0, p)),
            pl.BlockSpec((None, length, LANES), lambda s, p, j: (s, 0, p)),
            pl.BlockSpec((1, DIL_QB, DIL_KB), lambda s, p, j: (variant(j), 0, 0)),
        ],
        out_specs=[pl.BlockSpec((None, DIL_QB, LANES), lambda s, p, j: (s, j, p))] * 2,
        out_shape=[jax.ShapeDtypeStruct((nseq, length, W_DIL), F32)] * 2,
        compiler_params=_cparams(("parallel", "parallel", "arbitrary")),
        name="band_attention",
    )(q, k, v, masks)


def _band_masks():
    qq = np.arange(DIL_QB)[:, None]
    kk = np.arange(DIL_KB)[None, :]
    out = []
    for off in (0, -DIL_HALF, DIL_QB - DIL_KB):
        rel = kk + off - qq
        out.append(np.where(np.abs(rel) <= DIL_HALF, 0.0, NEG_INF))
    return jnp.asarray(np.stack(out), F32)


def _memkv_kernel(mem_ref, g_ref, w_ref, gk_ref, seg_ref, km_ref, vm_ref):
    mn = _rms(mem_ref[...], g_ref[...]).astype(BF16)
    kv = jnp.dot(mn, w_ref[...], preferred_element_type=F32)
    for c in range(W_MEM // LANES):
        x = kv[:, c * LANES:(c + 1) * LANES]
        ss = jnp.dot((x * x).astype(BF16), seg_ref[...], preferred_element_type=F32)
        x = x * lax.rsqrt(ss * (1.0 / HEAD_DIM) + RMS_EPS) * gk_ref[...]
        km_ref[:, c * LANES:(c + 1) * LANES] = x.astype(BF16)
    vm_ref[...] = kv[:, W_MEM:].astype(BF16)


def _memkv(mem2, g_mem, w_kv_bf16, gk_row, seg):
    rows = mem2.shape[0]
    full = lambda shape: pl.BlockSpec(shape, lambda i: (0,) * len(shape))
    return pl.pallas_call(
        _memkv_kernel,
        grid=(1,),
        in_specs=[full((rows, D_MODEL)), full((1, D_MODEL)), full((D_MODEL, 2 * W_MEM)),
                  full((1, LANES)), full((LANES, LANES))],
        out_specs=[full((rows, W_MEM)), full((rows, W_MEM))],
        out_shape=[jax.ShapeDtypeStruct((rows, W_MEM), BF16)] * 2,
        compiler_params=_cparams(("arbitrary",)),
        name="mem_kv",
    )(mem2, g_mem, w_kv_bf16, gk_row, seg)


def _memattn_kernel(q_ref, k_ref, v_ref, o_ref):
    for c in range(W_MEM // LANES):
        sl = slice(c * LANES, (c + 1) * LANES)
        o, _ = _pair_attention(q_ref[:, sl], k_ref[:, sl], v_ref[:, sl], lambda hh: None, False)
        o_ref[:, sl] = o


def _mem_attention(qm, km, vm, batch, seq_len, mem_len):
    tq = TQ_MEM
    nq = seq_len // tq
    return pl.pallas_call(
        _memattn_kernel,
        grid=(batch, nq),
        in_specs=[
            pl.BlockSpec((tq, W_MEM), lambda b, i: (b * nq + i, 0)),
            pl.BlockSpec((mem_len, W_MEM), lambda b, i: (b, 0)),
            pl.BlockSpec((mem_len, W_MEM), lambda b, i: (b, 0)),
        ],
        out_specs=pl.BlockSpec((tq, W_MEM), lambda b, i: (b * nq + i, 0)),
        out_shape=jax.ShapeDtypeStruct((batch * seq_len, W_MEM), F32),
        compiler_params=_cparams(("parallel", "arbitrary")),
        name="mem_attention",
    )(qm, km, vm)


def _outproj_kernel(ona_ref, o1_ref, l1_ref, o2_ref, l2_ref, o3_ref, l3_ref, om_ref, h_ref,
                    go_ref, w_ref, gf_ref, *rest, with_router):
    if with_router:
        wr_ref, hn_ref, u_ref, lg_ref = rest
    else:
        hn_ref, u_ref = rest
    l1, l2, l3 = l1_ref[...], l2_ref[...], l3_ref[...]
    mx = jnp.maximum(jnp.maximum(l1, l2), l3)
    e1, e2, e3 = jnp.exp(l1 - mx), jnp.exp(l2 - mx), jnp.exp(l3 - mx)
    odil = (e1 * o1_ref[...] + e2 * o2_ref[...] + e3 * o3_ref[...]) / (e1 + e2 + e3)
    go = go_ref[...]
    mixed = jnp.concatenate([
        _rms(ona_ref[...], go[:, :W_NA]),
        _rms(odil, go[:, W_NA:W_NA + W_DIL]),
        _rms(om_ref[...], go[:, W_NA + W_DIL:]),
    ], axis=-1).astype(BF16)
    hn = h_ref[...] + jnp.dot(mixed, w_ref[...], preferred_element_type=F32)
    hn_ref[...] = hn
    u = _rms(hn, gf_ref[...])
    if with_router:
        u_ref[...] = u
        lg_ref[...] = jnp.dot(u, wr_ref[...], preferred_element_type=F32,
                              precision=lax.Precision.HIGHEST)
    else:
        u_ref[...] = u.astype(BF16)


def _outproj(ona, dil, om, h, g_out, w_out_bf16, g_ffn, w_router_pad):
    n = h.shape[0]
    tm = TM_PROJ
    with_router = w_router_pad is not None
    row = lambda w: pl.BlockSpec((tm, w), lambda i: (i, 0))
    full = lambda shape: pl.BlockSpec(shape, lambda i: (0,) * len(shape))
    in_specs = ([row(W_NA)] + [row(W_DIL)] * 6 + [row(W_MEM), row(D_MODEL),
                full((1, D_MODEL)), full((D_MODEL, D_MODEL)), full((1, D_MODEL))])
    args = [ona, *dil, om, h, g_out, w_out_bf16, g_ffn]
    out_specs = [row(D_MODEL), row(D_MODEL)]
    out_shape = [jax.ShapeDtypeStruct((n, D_MODEL), F32),
                 jax.ShapeDtypeStruct((n, D_MODEL), F32 if with_router else BF16)]
    if with_router:
        in_specs.append(full((D_MODEL, LANES)))
        args.append(w_router_pad)
        out_specs.append(row(LANES))
        out_shape.append(jax.ShapeDtypeStruct((n, LANES), F32))
    return pl.pallas_call(
        functools.partial(_outproj_kernel, with_router=with_router),
        grid=(n // tm,),
        in_specs=in_specs,
        out_specs=out_specs,
        out_shape=out_shape,
        compiler_params=_cparams(("parallel",)),
        name="outproj",
    )(*args)


def _swiglu_step(x, wg_ref, wu_ref, wd_ref):
    g = jnp.dot(x, wg_ref[...].astype(BF16), preferred_element_type=F32)
    u = jnp.dot(x, wu_ref[...].astype(BF16), preferred_element_type=F32)
    hm = (g * jax.nn.sigmoid(g) * u).astype(BF16)
    return jnp.dot(hm, wd_ref[...].astype(BF16), preferred_element_type=F32)


def _dense_ffn_kernel(u_ref, h_ref, wg_ref, wu_ref, wd_ref, o_ref, acc_ref):
    j = pl.program_id(1)

    @pl.when(j == 0)
    def _():
        acc_ref[...] = h_ref[...]

    acc_ref[...] += _swiglu_step(u_ref[...], wg_ref, wu_ref, wd_ref)

    @pl.when(j == pl.num_programs(1) - 1)
    def _():
        o_ref[...] = acc_ref[...]


def _dense_ffn(u, h, wg, wu, wd):
    n = h.shape[0]
    d_ff = wg.shape[1]
    tm, tf = TM_FFN, TF_DENSE
    return pl.pallas_call(
        _dense_ffn_kernel,
        grid=(n // tm, d_ff // tf),
        in_specs=[
            pl.BlockSpec((tm, D_MODEL), lambda i, j: (i, 0)),
            pl.BlockSpec((tm, D_MODEL), lambda i, j: (i, 0)),
            pl.BlockSpec((D_MODEL, tf), lambda i, j: (0, j)),
            pl.BlockSpec((D_MODEL, tf), lambda i, j: (0, j)),
            pl.BlockSpec((tf, D_MODEL), lambda i, j: (j, 0)),
        ],
        out_specs=pl.BlockSpec((tm, D_MODEL), lambda i, j: (i, 0)),
        out_shape=jax.ShapeDtypeStruct((n, D_MODEL), F32),
        scratch_shapes=[pltpu.VMEM((tm, D_MODEL), F32)],
        compiler_params=_cparams(("parallel", "arbitrary")),
        name="dense_ffn",
    )(u, h, wg, wu, wd)


def _moe_ffn_kernel(be_ref, nu_ref, tok_ref, u_hbm, gate_ref, wg_ref, wu_ref, wd_ref, y_ref,
                    xbuf, xb, acc_ref, sem):
    i = pl.program_id(0)
    j = pl.program_id(1)
    active = i < nu_ref[0]
    tm = xbuf.shape[0]

    @pl.when(jnp.logical_and(active, j == 0))
    def _():
        def issue(r, carry):
            tok = tok_ref[0, 0, r]
            pltpu.make_async_copy(u_hbm.at[pl.ds(tok, 1), :], xbuf.at[pl.ds(r, 1), :], sem).start()
            return carry

        lax.fori_loop(0, tm, issue, 0)
        pltpu.make_async_copy(u_hbm.at[pl.ds(0, tm), :], xbuf, sem).wait()
        xb[...] = xbuf[...].astype(BF16)
        acc_ref[...] = jnp.zeros_like(acc_ref)

    @pl.when(active)
    def _():
        acc_ref[...] += _swiglu_step(xb[...], wg_ref, wu_ref, wd_ref)

    last = j == pl.num_programs(1) - 1

    @pl.when(jnp.logical_and(active, last))
    def _():
        y_ref[...] = acc_ref[...] * gate_ref[...]

    @pl.when(jnp.logical_and(jnp.logical_not(active), last))
    def _():
        y_ref[...] = jnp.zeros_like(y_ref)


def _moe_ffn(u, row_tok, row_gate, blk_expert, n_used, wg, wu, wd):
    n_rows = row_tok.shape[0]
    tm, tf = TM_MOE, TF_MOE
    nblk = n_rows // tm
    d_ff = wg.shape[2]
    nj = d_ff // tf

    def jeff(i, j, nu):
        return jnp.where(i < nu[0], j, nj - 1)

    grid_spec = pltpu.PrefetchScalarGridSpec(
        num_scalar_prefetch=2,
        grid=(nblk, nj),
        in_specs=[
            pl.BlockSpec((1, 1, tm), lambda i, j, be, nu: (i, 0, 0), memory_space=pltpu.SMEM),
            pl.BlockSpec(memory_space=pl.ANY),
            pl.BlockSpec((tm, 1), lambda i, j, be, nu: (i, 0)),
            pl.BlockSpec((None, D_MODEL, tf), lambda i, j, be, nu: (be[i], 0, jeff(i, j, nu))),
            pl.BlockSpec((None, D_MODEL, tf), lambda i, j, be, nu: (be[i], 0, jeff(i, j, nu))),
            pl.BlockSpec((None, tf, D_MODEL), lambda i, j, be, nu: (be[i], jeff(i, j, nu), 0)),
        ],
        out_specs=pl.BlockSpec((tm, D_MODEL), lambda i, j, be, nu: (i, 0)),
        scratch_shapes=[
            pltpu.VMEM((tm, D_MODEL), F32),
            pltpu.VMEM((tm, D_MODEL), BF16),
            pltpu.VMEM((tm, D_MODEL), F32),
            pltpu.SemaphoreType.DMA(()),
        ],
    )
    return pl.pallas_call(
        _moe_ffn_kernel,
        grid_spec=grid_spec,
        out_shape=jax.ShapeDtypeStruct((n_rows, D_MODEL), F32),
        compiler_params=_cparams(("arbitrary", "arbitrary")),
        name="moe_ffn",
    )(blk_expert, n_used, row_tok.reshape(nblk, 1, tm), u, row_gate.reshape(n_rows, 1), wg, wu, wd)


def _combine_kernel(p0_ref, p1_ref, h_ref, y_hbm, o_ref, buf0, buf1, sem):
    tc = buf0.shape[0]

    def issue(r, carry):
        pltpu.make_async_copy(y_hbm.at[pl.ds(p0_ref[0, 0, r], 1), :], buf0.at[pl.ds(r, 1), :],
                              sem.at[0]).start()
        pltpu.make_async_copy(y_hbm.at[pl.ds(p1_ref[0, 0, r], 1), :], buf1.at[pl.ds(r, 1), :],
                              sem.at[1]).start()
        return carry

    lax.fori_loop(0, tc, issue, 0)
    pltpu.make_async_copy(y_hbm.at[pl.ds(0, tc), :], buf0, sem.at[0]).wait()
    pltpu.make_async_copy(y_hbm.at[pl.ds(0, tc), :], buf1, sem.at[1]).wait()
    o_ref[...] = h_ref[...] + buf0[...] + buf1[...]


def _combine(h, y, pos):
    n = h.shape[0]
    tc = TC_COMBINE
    nblk = n // tc
    p0 = pos[:, 0].reshape(nblk, 1, tc)
    p1 = pos[:, 1].reshape(nblk, 1, tc)
    smem = pl.BlockSpec((1, 1, tc), lambda i: (i, 0, 0), memory_space=pltpu.SMEM)
    return pl.pallas_call(
        _combine_kernel,
        grid=(nblk,),
        in_specs=[smem, smem, pl.BlockSpec((tc, D_MODEL), lambda i: (i, 0)),
                  pl.BlockSpec(memory_space=pl.ANY)],
        out_specs=pl.BlockSpec((tc, D_MODEL), lambda i: (i, 0)),
        out_shape=jax.ShapeDtypeStruct((n, D_MODEL), F32),
        scratch_shapes=[pltpu.VMEM((tc, D_MODEL), F32), pltpu.VMEM((tc, D_MODEL), F32),
                        pltpu.SemaphoreType.DMA((2,))],
        compiler_params=_cparams(("arbitrary",)),
        name="moe_combine",
    )(p0, p1, h, y)


def _route(logits):
    n = logits.shape[0]
    top_logit, top_idx = lax.top_k(logits, TOP_K)
    gates = jax.nn.softmax(top_logit, axis=-1)
    e_flat = top_idx.reshape(n * TOP_K).astype(jnp.int32)
    onehot = (e_flat[:, None] == jnp.arange(N_EXPERTS, dtype=jnp.int32)[None, :]).astype(jnp.int32)
    csum = jnp.cumsum(onehot, axis=0)
    counts = csum[-1]
    rank = jnp.sum(onehot * csum, axis=1) - 1
    padded = (counts + TM_MOE - 1) // TM_MOE * TM_MOE
    pend = jnp.cumsum(padded)
    pstart = pend - padded
    dest = pstart[e_flat] + rank
    n_rows = n * TOP_K + N_EXPERTS * TM_MOE
    nblk = n_rows // TM_MOE
    tok_flat = jnp.arange(n * TOP_K, dtype=jnp.int32) // TOP_K
    row_tok = jnp.zeros((n_rows,), jnp.int32).at[dest].set(tok_flat)
    row_gate = jnp.zeros((n_rows,), F32).at[dest].set(gates.reshape(-1))
    n_used = (pend[-1] // TM_MOE).astype(jnp.int32)
    blk_start = jnp.arange(nblk, dtype=jnp.int32) * TM_MOE
    blk_start = jnp.minimum(blk_start, (n_used - 1) * TM_MOE)
    blk_expert = jnp.minimum(
        jnp.sum((blk_start[:, None] >= pend[None, :]).astype(jnp.int32), axis=1), N_EXPERTS - 1)
    return row_tok, row_gate, blk_expert.astype(jnp.int32), n_used.reshape(1), dest.reshape(n, TOP_K)


def _rope_tables(seq_len):
    half = HEAD_DIM // 2
    inv_freq = jnp.power(ROPE_THETA, -(2.0 / HEAD_DIM) * jnp.arange(half, dtype=F32))
    ang = jnp.arange(seq_len, dtype=F32)[:, None] * inv_freq[None, :]
    cos, sin = jnp.cos(ang), jnp.sin(ang)
    cos_t = jnp.tile(cos, (1, LANES // half))
    sin_t = jnp.tile(jnp.concatenate([-sin, sin], axis=1), (1, LANES // HEAD_DIM))
    return cos_t, sin_t


def _to_residue_major(a, batch, seq_len, d):
    w = a.shape[-1]
    return a.reshape(batch, seq_len // d, d, w).transpose(0, 2, 1, 3).reshape(batch * d, seq_len // d, w)


def _from_residue_major(a, batch, seq_len, d):
    w = a.shape[-1]
    return a.reshape(batch, d, seq_len // d, w).transpose(0, 2, 1, 3).reshape(batch * seq_len, w)


def kernel(x, mem, g_attn, w_in, g_qk_na, rpb_na, g_qk_dil, g_mem, w_mem_kv, g_qk_mem, g_out, w_out,
           g_ffn, w_gate_dense, w_up_dense, w_down_dense, w_router, w_gate_moe, w_up_moe, w_down_moe):
    batch, seq_len, _ = x.shape
    mem_len = mem.shape[1]
    depth = g_attn.shape[0]
    n = batch * seq_len
    scale = HEAD_DIM ** -0.5

    cos_t, sin_t = _rope_tables(seq_len)
    seg = jnp.asarray(np.kron(np.eye(LANES // HEAD_DIM), np.ones((HEAD_DIM, HEAD_DIM))), BF16)
    band_masks = _band_masks()
    ones = lambda w: jnp.ones((w,), F32)

    h = x.reshape(n, D_MODEL)
    mem2 = mem.reshape(batch * mem_len, D_MODEL)
    for layer in range(depth):
        gq_row = jnp.concatenate([
            jnp.tile(g_qk_na[layer, 0] * scale, H_NA), jnp.tile(g_qk_na[layer, 1], H_NA), ones(W_NA),
            jnp.tile(g_qk_dil[layer, 0] * scale, H_DIL), jnp.tile(g_qk_dil[layer, 1], H_DIL), ones(W_DIL),
            jnp.tile(g_qk_mem[layer, 0] * scale, H_MEM)]).reshape(1, IN_WIDTH)
        qa, ka, va, qb, kb, vb, qm = _inproj(
            h, g_attn[layer].reshape(1, D_MODEL), w_in[layer].astype(BF16), gq_row, cos_t, sin_t, seg,
            seq_len)

        o_na = _na_attention(qa, ka, va, _na_bias_tables(rpb_na[layer], seq_len // GRID_W),
                             batch, seq_len)

        dil = []
        for _, d in DIL_CFG:
            qd, kd, vd = (_to_residue_major(a, batch, seq_len, d) for a in (qb, kb, vb))
            o_d, lse_d = _band_attention(qd, kd, vd, band_masks)
            dil += [_from_residue_major(o_d, batch, seq_len, d),
                    _from_residue_major(lse_d, batch, seq_len, d)]

        km, vm = _memkv(mem2, g_mem[layer].reshape(1, D_MODEL), w_mem_kv[layer].astype(BF16),
                        jnp.tile(g_qk_mem[layer, 1], LANES // HEAD_DIM).reshape(1, LANES), seg)
        o_mem = _mem_attention(qm, km, vm, batch, seq_len, mem_len)

        i = layer // 2
        moe = layer % 2 == 1
        w_router_pad = None
        if moe:
            w_router_pad = jnp.pad(w_router[i], ((0, 0), (0, LANES - N_EXPERTS)))
        outs = _outproj(o_na, dil, o_mem, h, g_out[layer].reshape(1, D_MODEL), w_out[layer].astype(BF16),
                        g_ffn[layer].reshape(1, D_MODEL), w_router_pad)
        if not moe:
            h, u = outs
            h = _dense_ffn(u, h, w_gate_dense[i], w_up_dense[i], w_down_dense[i])
        else:
            h, u, logits = outs
            row_tok, row_gate, blk_expert, n_used, pos = _route(logits[:, :N_EXPERTS])
            y = _moe_ffn(u, row_tok, row_gate, blk_expert, n_used, w_gate_moe[i], w_up_moe[i], w_down_moe[i])
            h = _combine(h, y, pos)
    return h.reshape(batch, seq_len, D_MODEL)
```

```python
import functools

import numpy as np
import jax
import jax.numpy as jnp
from jax import lax
from jax.experimental import pallas as pl
from jax.experimental.pallas import tpu as pltpu

F32 = jnp.float32
BF16 = jnp.bfloat16

D_MODEL = 1024
HEAD_DIM = 64
H_NA, H_DIL, H_MEM = 6, 6, 4
W_NA, W_DIL, W_MEM = H_NA * HEAD_DIM, H_DIL * HEAD_DIM, H_MEM * HEAD_DIM
IN_WIDTH = 3 * W_NA + 3 * W_DIL + W_MEM
GRID_W = 64
NA_KH, NA_KW = 8, 16
DIL_CFG = ((128, 1), (512, 4), (2048, 16))
ROPE_THETA = 10000.0
N_EXPERTS = 8
TOP_K = 2
RMS_EPS = 1e-6
NEG_INF = -1e30

LANES = 128
TM_PROJ = 512
NA_QROWS = 4
NA_QB = NA_QROWS * GRID_W
NA_KROWS = 12
NA_KB = NA_KROWS * GRID_W
DIL_QB = 256
DIL_HALF = 64
DIL_KB = DIL_QB + 2 * DIL_HALF
TQ_MEM = 512
TM_FFN = 1024
TF_DENSE = 256
TM_MOE = 1024
TF_MOE = 512
TC_COMBINE = 512
VMEM_LIMIT = 56 * 1024 * 1024


def _cparams(sem):
    return pltpu.CompilerParams(dimension_semantics=sem, vmem_limit_bytes=VMEM_LIMIT)


def _rms(x, g):
    return x * lax.rsqrt(jnp.mean(x * x, axis=-1, keepdims=True) + RMS_EPS) * g


_CHUNKS = (
    [(0, c, True, False) for c in range(3)] + [(1, c, True, False) for c in range(3)]
    + [(2, c, False, False) for c in range(3)]
    + [(3, c, True, True) for c in range(3)] + [(4, c, True, True) for c in range(3)]
    + [(5, c, False, False) for c in range(3)]
    + [(6, c, True, False) for c in range(2)]
)


def _inproj_kernel(h_ref, ga_ref, w_ref, gq_ref, cos_ref, sin_ref, seg_ref, *out_refs):
    u = _rms(h_ref[...], ga_ref[...]).astype(BF16)
    p = jnp.dot(u, w_ref[...], preferred_element_type=F32)
    seg = seg_ref[...]
    lane = lax.broadcasted_iota(jnp.int32, (1, LANES), 1)
    first_half = (lane % HEAD_DIM) < (HEAD_DIM // 2)
    for c, (oi, oc, norm, rope) in enumerate(_CHUNKS):
        x = p[:, c * LANES:(c + 1) * LANES]
        if norm:
            ss = jnp.dot((x * x).astype(BF16), seg, preferred_element_type=F32)
            x = x * lax.rsqrt(ss * (1.0 / HEAD_DIM) + RMS_EPS) * gq_ref[:, c * LANES:(c + 1) * LANES]
        if rope:
            swapped = jnp.where(first_half, pltpu.roll(x, LANES - HEAD_DIM // 2, 1),
                                pltpu.roll(x, HEAD_DIM // 2, 1))
            x = x * cos_ref[...] + swapped * sin_ref[...]
        out_refs[oi][:, oc * LANES:(oc + 1) * LANES] = x.astype(BF16)


def _inproj(h, g_attn, w_in_bf16, gq_row, cos_t, sin_t, seg, seq_len):
    n = h.shape[0]
    tm = TM_PROJ
    tblocks = seq_len // tm
    widths = (W_NA, W_NA, W_NA, W_DIL, W_DIL, W_DIL, W_MEM)
    return pl.pallas_call(
        _inproj_kernel,
        grid=(n // tm,),
        in_specs=[
            pl.BlockSpec((tm, D_MODEL), lambda i: (i, 0)),
            pl.BlockSpec((1, D_MODEL), lambda i: (0, 0)),
            pl.BlockSpec((D_MODEL, IN_WIDTH), lambda i: (0, 0)),
            pl.BlockSpec((1, IN_WIDTH), lambda i: (0, 0)),
            pl.BlockSpec((tm, LANES), lambda i: (i % tblocks, 0)),
            pl.BlockSpec((tm, LANES), lambda i: (i % tblocks, 0)),
            pl.BlockSpec((LANES, LANES), lambda i: (0, 0)),
        ],
        out_specs=[pl.BlockSpec((tm, w), lambda i: (i, 0)) for w in widths],
        out_shape=[jax.ShapeDtypeStruct((n, w), BF16) for w in widths],
        compiler_params=_cparams(("parallel",)),
        name="inproj",
    )(h, g_attn, w_in_bf16, gq_row, cos_t, sin_t, seg)


def _pair_attention(q, k, v, bias_fn, want_lse):
    lane = lax.broadcasted_iota(jnp.int32, (1, LANES), 1)
    out = None
    lse_out = None
    for hh in range(2):
        mine = (lane < HEAD_DIM) if hh == 0 else (lane >= HEAD_DIM)
        qh = jnp.where(mine, q, jnp.zeros_like(q))
        s = lax.dot_general(qh, k, (((1,), (1,)), ((), ())), preferred_element_type=F32)
        b = bias_fn(hh)
        if b is not None:
            s = s + b
        m = jnp.max(s, axis=-1, keepdims=True)
        p = jnp.exp(s - m)
        l = jnp.sum(p, axis=-1, keepdims=True)
        o = jnp.dot(p.astype(BF16), v, preferred_element_type=F32) / l
        out = o if out is None else jnp.where(mine, o, out)
        if want_lse:
            lse = jnp.broadcast_to(m + jnp.log(l), o.shape)
            lse_out = lse if lse_out is None else jnp.where(mine, lse, lse_out)
    return out, lse_out


def _na_kernel(q_ref, k_ref, v_ref, bias_ref, o_ref, *, rows):
    i = pl.program_id(2)
    srow = jnp.clip(i * NA_QROWS - NA_KH // 2, 0, rows - NA_KROWS)
    start = pl.multiple_of(srow * GRID_W, GRID_W)
    k = k_ref[pl.ds(start, NA_KB), :]
    v = v_ref[pl.ds(start, NA_KB), :]
    o, _ = _pair_attention(q_ref[...], k, v, lambda hh: bias_ref[0, hh], False)
    o_ref[...] = o


def _na_attention(q, k, v, bias, batch, seq_len):
    rows = seq_len // GRID_W
    nblk = seq_len // NA_QB
    q3, k3, v3 = (a.reshape(batch, seq_len, W_NA) for a in (q, k, v))

    def variant(i):
        return (i > 0).astype(jnp.int32) + (i == nblk - 1).astype(jnp.int32)

    out = pl.pallas_call(
        functools.partial(_na_kernel, rows=rows),
        grid=(batch, W_NA // LANES, nblk),
        in_specs=[
            pl.BlockSpec((None, NA_QB, LANES), lambda b, p, i: (b, i, p)),
            pl.BlockSpec((None, seq_len, LANES), lambda b, p, i: (b, 0, p)),
            pl.BlockSpec((None, seq_len, LANES), lambda b, p, i: (b, 0, p)),
            pl.BlockSpec((1, 2, NA_QB, NA_KB), lambda b, p, i: (variant(i), p, 0, 0)),
        ],
        out_specs=pl.BlockSpec((None, NA_QB, LANES), lambda b, p, i: (b, i, p)),
        out_shape=jax.ShapeDtypeStruct((batch, seq_len, W_NA), F32),
        compiler_params=_cparams(("parallel", "parallel", "arbitrary")),
        name="na_attention",
    )(q3, k3, v3, bias)
    return out.reshape(batch * seq_len, W_NA)


def _na_bias_tables(rpb, rows):
    nblk = rows // NA_QROWS
    hi = lax.Precision.HIGHEST
    qc = np.arange(GRID_W)[:, None]
    kc = np.arange(GRID_W)[None, :]
    cs = np.clip(qc - NA_KW // 2, 0, GRID_W - NA_KW)
    col_valid = (kc >= cs) & (kc < cs + NA_KW)
    col_rel = kc - qc + (NA_KW - 1)
    col_sel = (col_rel[None] == np.arange(2 * NA_KW - 1)[:, None, None]).astype(np.float32)
    t1 = jnp.einsum('hab,bqk->haqk', rpb, jnp.asarray(col_sel), precision=hi)
    row_sel, row_valid = [], []
    for blk in (0, 1, nblk - 1):
        r0 = blk * NA_QROWS
        srow = int(np.clip(r0 - NA_KH // 2, 0, rows - NA_KROWS))
        qr = (r0 + np.arange(NA_QROWS))[:, None]
        kr = (srow + np.arange(NA_KROWS))[None, :]
        rs = np.clip(qr - NA_KH // 2, 0, rows - NA_KH)
        row_valid.append((kr >= rs) & (kr < rs + NA_KH))
        row_rel = kr - qr + (NA_KH - 1)
        row_sel.append((row_rel[..., None] == np.arange(2 * NA_KH - 1)).astype(np.float32))
    vals = jnp.einsum('vrsa,haqk->vhrqsk', jnp.asarray(np.stack(row_sel)), t1, precision=hi)
    valid = (np.stack(row_valid)[:, :, None, :, None] & col_valid[None, None, :, None, :])
    out = jnp.where(jnp.asarray(valid)[:, None], vals, NEG_INF)
    return out.reshape(3, H_NA, NA_QB, NA_KB).astype(F32)


def _band_kernel(q_ref, k_ref, v_ref, mask_ref, o_ref, lse_ref, *, length):
    j = pl.program_id(2)
    start = jnp.clip(j * DIL_QB - DIL_HALF, 0, length - DIL_KB)
    start = pl.multiple_of(start, DIL_HALF)
    k = k_ref[pl.ds(start, DIL_KB), :]
    v = v_ref[pl.ds(start, DIL_KB), :]
    o, lse = _pair_attention(q_ref[...], k, v, lambda hh: mask_ref[0], True)
    o_ref[...] = o
    lse_ref[...] = lse


def _band_attention(q, k, v, masks):
    nseq, length, _ = q.shape
    nblk = length // DIL_QB

    def variant(j):
        return (j > 0).astype(jnp.int32) + (j == nblk - 1).astype(jnp.int32)

    return pl.pallas_call(
        functools.partial(_band_kernel, length=length),
        grid=(nseq, W_DIL // LANES, nblk),
        in_specs=[
            pl.BlockSpec((None, DIL_QB, LANES), lambda s, p, j: (s, j, p)),
            pl.BlockSpec((None, length, LANES), lambda s, p, j: (s, 0, p)),
            pl.BlockSpec((None, length, LANES), lambda s, p, j: (s, 0, p)),
            pl.BlockSpec((1, DIL_QB, DIL_KB), lambda s, p, j: (variant(j), 0, 0)),
        ],
        out_specs=[pl.BlockSpec((None, DIL_QB, LANES), lambda s, p, j: (s, j, p))] * 2,
        out_shape=[jax.ShapeDtypeStruct((nseq, length, W_DIL), F32)] * 2,
        compiler_params=_cparams(("parallel", "parallel", "arbitrary")),
        name="band_attention",
    )(q, k, v, masks)


def _band_masks():
    qq = np.arange(DIL_QB)[:, None]
    kk = np.arange(DIL_KB)[None, :]
    out = []
    for off in (0, -DIL_HALF, DIL_QB - DIL_KB):
        rel = kk + off - qq
        out.append(np.where(np.abs(rel) <= DIL_HALF, 0.0, NEG_INF))
    return jnp.asarray(np.stack(out), F32)


def _memkv_kernel(mem_ref, g_ref, w_ref, gk_ref, seg_ref, km_ref, vm_ref):
    mn = _rms(mem_ref[...], g_ref[...]).astype(BF16)
    kv = jnp.dot(mn, w_ref[...], preferred_element_type=F32)
    for c in range(W_MEM // LANES):
        x = kv[:, c * LANES:(c + 1) * LANES]
        ss = jnp.dot((x * x).astype(BF16), seg_ref[...], preferred_element_type=F32)
        x = x * lax.rsqrt(ss * (1.0 / HEAD_DIM) + RMS_EPS) * gk_ref[...]
        km_ref[:, c * LANES:(c + 1) * LANES] = x.astype(BF16)
    vm_ref[...] = kv[:, W_MEM:].astype(BF16)


def _memkv(mem2, g_mem, w_kv_bf16, gk_row, seg):
    rows = mem2.shape[0]
    full = lambda shape: pl.BlockSpec(shape, lambda i: (0,) * len(shape))
    return pl.pallas_call(
        _memkv_kernel,
        grid=(1,),
        in_specs=[full((rows, D_MODEL)), full((1, D_MODEL)), full((D_MODEL, 2 * W_MEM)),
                  full((1, LANES)), full((LANES, LANES))],
        out_specs=[full((rows, W_MEM)), full((rows, W_MEM))],
        out_shape=[jax.ShapeDtypeStruct((rows, W_MEM), BF16)] * 2,
        compiler_params=_cparams(("arbitrary",)),
        name="mem_kv",
    )(mem2, g_mem, w_kv_bf16, gk_row, seg)


def _memattn_kernel(q_ref, k_ref, v_ref, o_ref):
    for c in range(W_MEM // LANES):
        sl = slice(c * LANES, (c + 1) * LANES)
        o, _ = _pair_attention(q_ref[:, sl], k_ref[:, sl], v_ref[:, sl], lambda hh: None, False)
        o_ref[:, sl] = o


def _mem_attention(qm, km, vm, batch, seq_len, mem_len):
    tq = TQ_MEM
    nq = seq_len // tq
    return pl.pallas_call(
        _memattn_kernel,
        grid=(batch, nq),
        in_specs=[
            pl.BlockSpec((tq, W_MEM), lambda b, i: (b * nq + i, 0)),
            pl.BlockSpec((mem_len, W_MEM), lambda b, i: (b, 0)),
            pl.BlockSpec((mem_len, W_MEM), lambda b, i: (b, 0)),
        ],
        out_specs=pl.BlockSpec((tq, W_MEM), lambda b, i: (b * nq + i, 0)),
        out_shape=jax.ShapeDtypeStruct((batch * seq_len, W_MEM), F32),
        compiler_params=_cparams(("parallel", "arbitrary")),
        name="mem_attention",
    )(qm, km, vm)


def _outproj_kernel(ona_ref, o1_ref, l1_ref, o2_ref, l2_ref, o3_ref, l3_ref, om_ref, h_ref,
                    go_ref, w_ref, gf_ref, *rest, with_router):
    if with_router:
        wr_ref, hn_ref, u_ref, lg_ref = rest
    else:
        hn_ref, u_ref = rest
    l1, l2, l3 = l1_ref[...], l2_ref[...], l3_ref[...]
    mx = jnp.maximum(jnp.maximum(l1, l2), l3)
    e1, e2, e3 = jnp.exp(l1 - mx), jnp.exp(l2 - mx), jnp.exp(l3 - mx)
    odil = (e1 * o1_ref[...] + e2 * o2_ref[...] + e3 * o3_ref[...]) / (e1 + e2 + e3)
    go = go_ref[...]
    mixed = jnp.concatenate([
        _rms(ona_ref[...], go[:, :W_NA]),
        _rms(odil, go[:, W_NA:W_NA + W_DIL]),
        _rms(om_ref[...], go[:, W_NA + W_DIL:]),
    ], axis=-1).astype(BF16)
    hn = h_ref[...] + jnp.dot(mixed, w_ref[...], preferred_element_type=F32)
    hn_ref[...] = hn
    u = _rms(hn, gf_ref[...])
    if with_router:
        u_ref[...] = u
        lg_ref[...] = jnp.dot(u, wr_ref[...], preferred_element_type=F32,
                              precision=lax.Precision.HIGHEST)
    else:
        u_ref[...] = u.astype(BF16)


def _outproj(ona, dil, om, h, g_out, w_out_bf16, g_ffn, w_router_pad):
    n = h.shape[0]
    tm = TM_PROJ
    with_router = w_router_pad is not None
    row = lambda w: pl.BlockSpec((tm, w), lambda i: (i, 0))
    full = lambda shape: pl.BlockSpec(shape, lambda i: (0,) * len(shape))
    in_specs = ([row(W_NA)] + [row(W_DIL)] * 6 + [row(W_MEM), row(D_MODEL),
                full((1, D_MODEL)), full((D_MODEL, D_MODEL)), full((1, D_MODEL))])
    args = [ona, *dil, om, h, g_out, w_out_bf16, g_ffn]
    out_specs = [row(D_MODEL), row(D_MODEL)]
    out_shape = [jax.ShapeDtypeStruct((n, D_MODEL), F32),
                 jax.ShapeDtypeStruct((n, D_MODEL), F32 if with_router else BF16)]
    if with_router:
        in_specs.append(full((D_MODEL, LANES)))
        args.append(w_router_pad)
        out_specs.append(row(LANES))
        out_shape.append(jax.ShapeDtypeStruct((n, LANES), F32))
    return pl.pallas_call(
        functools.partial(_outproj_kernel, with_router=with_router),
        grid=(n // tm,),
        in_specs=in_specs,
        out_specs=out_specs,
        out_shape=out_shape,
        compiler_params=_cparams(("parallel",)),
        name="outproj",
    )(*args)


def _swiglu_step(x, wg_ref, wu_ref, wd_ref):
    g = jnp.dot(x, wg_ref[...].astype(BF16), preferred_element_type=F32)
    u = jnp.dot(x, wu_ref[...].astype(BF16), preferred_element_type=F32)
    hm = (g * jax.nn.sigmoid(g) * u).astype(BF16)
    return jnp.dot(hm, wd_ref[...].astype(BF16), preferred_element_type=F32)


def _dense_ffn_kernel(u_ref, h_ref, wg_ref, wu_ref, wd_ref, o_ref, acc_ref):
    j = pl.program_id(1)

    @pl.when(j == 0)
    def _():
        acc_ref[...] = h_ref[...]

    acc_ref[...] += _swiglu_step(u_ref[...], wg_ref, wu_ref, wd_ref)

    @pl.when(j == pl.num_programs(1) - 1)
    def _():
        o_ref[...] = acc_ref[...]


def _dense_ffn(u, h, wg, wu, wd):
    n = h.shape[0]
    d_ff = wg.shape[1]
    tm, tf = TM_FFN, TF_DENSE
    return pl.pallas_call(
        _dense_ffn_kernel,
        grid=(n // tm, d_ff // tf),
        in_specs=[
            pl.BlockSpec((tm, D_MODEL), lambda i, j: (i, 0)),
            pl.BlockSpec((tm, D_MODEL), lambda i, j: (i, 0)),
            pl.BlockSpec((D_MODEL, tf), lambda i, j: (0, j)),
            pl.BlockSpec((D_MODEL, tf), lambda i, j: (0, j)),
            pl.BlockSpec((tf, D_MODEL), lambda i, j: (j, 0)),
        ],
        out_specs=pl.BlockSpec((tm, D_MODEL), lambda i, j: (i, 0)),
        out_shape=jax.ShapeDtypeStruct((n, D_MODEL), F32),
        scratch_shapes=[pltpu.VMEM((tm, D_MODEL), F32)],
        compiler_params=_cparams(("parallel", "arbitrary")),
        name="dense_ffn",
    )(u, h, wg, wu, wd)


def _moe_ffn_kernel(be_ref, nu_ref, tok_ref, u_hbm, gate_ref, wg_ref, wu_ref, wd_ref, y_ref,
                    xbuf, xb, acc_ref, sem):
    i = pl.program_id(0)
    j = pl.program_id(1)
    active = i < nu_ref[0]
    tm = xbuf.shape[0]

    @pl.when(jnp.logical_and(active, j == 0))
    def _():
        def issue(r, carry):
            tok = tok_ref[0, 0, r]
            pltpu.make_async_copy(u_hbm.at[pl.ds(tok, 1), :], xbuf.at[pl.ds(r, 1), :], sem).start()
            return carry

        lax.fori_loop(0, tm, issue, 0)
        pltpu.make_async_copy(u_hbm.at[pl.ds(0, tm), :], xbuf, sem).wait()
        xb[...] = xbuf[...].astype(BF16)
        acc_ref[...] = jnp.zeros_like(acc_ref)

    @pl.when(active)
    def _():
        acc_ref[...] += _swiglu_step(xb[...], wg_ref, wu_ref, wd_ref)

    last = j == pl.num_programs(1) - 1

    @pl.when(jnp.logical_and(active, last))
    def _():
        y_ref[...] = acc_ref[...] * gate_ref[...]

    @pl.when(jnp.logical_and(jnp.logical_not(active), last))
    def _():
        y_ref[...] = jnp.zeros_like(y_ref)


def _moe_ffn(u, row_tok, row_gate, blk_expert, n_used, wg, wu, wd):
    n_rows = row_tok.shape[0]
    tm, tf = TM_MOE, TF_MOE
    nblk = n_rows // tm
    d_ff = wg.shape[2]
    nj = d_ff // tf

    def jeff(i, j, nu):
        return jnp.where(i < nu[0], j, nj - 1)

    grid_spec = pltpu.PrefetchScalarGridSpec(
        num_scalar_prefetch=2,
        grid=(nblk, nj),
        in_specs=[
            pl.BlockSpec((1, 1, tm), lambda i, j, be, nu: (i, 0, 0), memory_space=pltpu.SMEM),
            pl.BlockSpec(memory_space=pl.ANY),
            pl.BlockSpec((tm, 1), lambda i, j, be, nu: (i, 0)),
            pl.BlockSpec((None, D_MODEL, tf), lambda i, j, be, nu: (be[i], 0, jeff(i, j, nu))),
            pl.BlockSpec((None, D_MODEL, tf), lambda i, j, be, nu: (be[i], 0, jeff(i, j, nu))),
            pl.BlockSpec((None, tf, D_MODEL), lambda i, j, be, nu: (be[i], jeff(i, j, nu), 0)),
        ],
        out_specs=pl.BlockSpec((tm, D_MODEL), lambda i, j, be, nu: (i, 0)),
        scratch_shapes=[
            pltpu.VMEM((tm, D_MODEL), F32),
            pltpu.VMEM((tm, D_MODEL), BF16),
            pltpu.VMEM((tm, D_MODEL), F32),
            pltpu.SemaphoreType.DMA(()),
        ],
    )
    return pl.pallas_call(
        _moe_ffn_kernel,
        grid_spec=grid_spec,
        out_shape=jax.ShapeDtypeStruct((n_rows, D_MODEL), F32),
        compiler_params=_cparams(("arbitrary", "arbitrary")),
        name="moe_ffn",
    )(blk_expert, n_used, row_tok.reshape(nblk, 1, tm), u, row_gate.reshape(n_rows, 1), wg, wu, wd)


def _combine_kernel(p0_ref, p1_ref, h_ref, y_hbm, o_ref, buf0, buf1, sem):
    tc = buf0.shape[0]

    def issue(r, carry):
        pltpu.make_async_copy(y_hbm.at[pl.ds(p0_ref[0, 0, r], 1), :], buf0.at[pl.ds(r, 1), :],
                              sem.at[0]).start()
        pltpu.make_async_copy(y_hbm.at[pl.ds(p1_ref[0, 0, r], 1), :], buf1.at[pl.ds(r, 1), :],
                              sem.at[1]).start()
        return carry

    lax.fori_loop(0, tc, issue, 0)
    pltpu.make_async_copy(y_hbm.at[pl.ds(0, tc), :], buf0, sem.at[0]).wait()
    pltpu.make_async_copy(y_hbm.at[pl.ds(0, tc), :], buf1, sem.at[1]).wait()
    o_ref[...] = h_ref[...] + buf0[...] + buf1[...]


def _combine(h, y, pos):
    n = h.shape[0]
    tc = TC_COMBINE
    nblk = n // tc
    p0 = pos[:, 0].reshape(nblk, 1, tc)
    p1 = pos[:, 1].reshape(nblk, 1, tc)
    smem = pl.BlockSpec((1, 1, tc), lambda i: (i, 0, 0), memory_space=pltpu.SMEM)
    return pl.pallas_call(
        _combine_kernel,
        grid=(nblk,),
        in_specs=[smem, smem, pl.BlockSpec((tc, D_MODEL), lambda i: (i, 0)),
                  pl.BlockSpec(memory_space=pl.ANY)],
        out_specs=pl.BlockSpec((tc, D_MODEL), lambda i: (i, 0)),
        out_shape=jax.ShapeDtypeStruct((n, D_MODEL), F32),
        scratch_shapes=[pltpu.VMEM((tc, D_MODEL), F32), pltpu.VMEM((tc, D_MODEL), F32),
                        pltpu.SemaphoreType.DMA((2,))],
        compiler_params=_cparams(("arbitrary",)),
        name="moe_combine",
    )(p0, p1, h, y)


def _route(logits):
    n = logits.shape[0]
    top_logit, top_idx = lax.top_k(logits, TOP_K)
    gates = jax.nn.softmax(top_logit, axis=-1)
    e_flat = top_idx.reshape(n * TOP_K).astype(jnp.int32)
    onehot = (e_flat[:, None] == jnp.arange(N_EXPERTS, dtype=jnp.int32)[None, :]).astype(jnp.int32)
    csum = jnp.cumsum(onehot, axis=0)
    counts = csum[-1]
    rank = jnp.sum(onehot * csum, axis=1) - 1
    padded = (counts + TM_MOE - 1) // TM_MOE * TM_MOE
    pend = jnp.cumsum(padded)
    pstart = pend - padded
    dest = pstart[e_flat] + rank
    n_rows = n * TOP_K + N_EXPERTS * TM_MOE
    nblk = n_rows // TM_MOE
    tok_flat = jnp.arange(n * TOP_K, dtype=jnp.int32) // TOP_K
    row_tok = jnp.zeros((n_rows,), jnp.int32).at[dest].set(tok_flat)
    row_gate = jnp.zeros((n_rows,), F32).at[dest].set(gates.reshape(-1))
    n_used = (pend[-1] // TM_MOE).astype(jnp.int32)
    blk_start = jnp.arange(nblk, dtype=jnp.int32) * TM_MOE
    blk_start = jnp.minimum(blk_start, (n_used - 1) * TM_MOE)
    blk_expert = jnp.minimum(
        jnp.sum((blk_start[:, None] >= pend[None, :]).astype(jnp.int32), axis=1), N_EXPERTS - 1)
    return row_tok, row_gate, blk_expert.astype(jnp.int32), n_used.reshape(1), dest.reshape(n, TOP_K)


def _rope_tables(seq_len):
    half = HEAD_DIM // 2
    inv_freq = jnp.power(ROPE_THETA, -(2.0 / HEAD_DIM) * jnp.arange(half, dtype=F32))
    ang = jnp.arange(seq_len, dtype=F32)[:, None] * inv_freq[None, :]
    cos, sin = jnp.cos(ang), jnp.sin(ang)
    cos_t = jnp.tile(cos, (1, LANES // half))
    sin_t = jnp.tile(jnp.concatenate([-sin, sin], axis=1), (1, LANES // HEAD_DIM))
    return cos_t, sin_t


def _to_residue_major(a, batch, seq_len, d):
    w = a.shape[-1]
    return a.reshape(batch, seq_len // d, d, w).transpose(0, 2, 1, 3).reshape(batch * d, seq_len // d, w)


def _from_residue_major(a, batch, seq_len, d):
    w = a.shape[-1]
    return a.reshape(batch, d, seq_len // d, w).transpose(0, 2, 1, 3).reshape(batch * seq_len, w)


def kernel(x, mem, g_attn, w_in, g_qk_na, rpb_na, g_qk_dil, g_mem, w_mem_kv, g_qk_mem, g_out, w_out,
           g_ffn, w_gate_dense, w_up_dense, w_down_dense, w_router, w_gate_moe, w_up_moe, w_down_moe):
    batch, seq_len, _ = x.shape
    mem_len = mem.shape[1]
    depth = g_attn.shape[0]
    n = batch * seq_len
    scale = HEAD_DIM ** -0.5

    cos_t, sin_t = _rope_tables(seq_len)
    seg = jnp.asarray(np.kron(np.eye(LANES // HEAD_DIM), np.ones((HEAD_DIM, HEAD_DIM))), BF16)
    band_masks = _band_masks()
    ones = lambda w: jnp.ones((w,), F32)

    h = x.reshape(n, D_MODEL)
    mem2 = mem.reshape(batch * mem_len, D_MODEL)
    for layer in range(depth):
        gq_row = jnp.concatenate([
            jnp.tile(g_qk_na[layer, 0] * scale, H_NA), jnp.tile(g_qk_na[layer, 1], H_NA), ones(W_NA),
            jnp.tile(g_qk_dil[layer, 0] * scale, H_DIL), jnp.tile(g_qk_dil[layer, 1], H_DIL), ones(W_DIL),
            jnp.tile(g_qk_mem[layer, 0] * scale, H_MEM)]).reshape(1, IN_WIDTH)
        qa, ka, va, qb, kb, vb, qm = _inproj(
            h, g_attn[layer].reshape(1, D_MODEL), w_in[layer].astype(BF16), gq_row, cos_t, sin_t, seg,
            seq_len)

        o_na = _na_attention(qa, ka, va, _na_bias_tables(rpb_na[layer], seq_len // GRID_W),
                             batch, seq_len)

        dil = []
        for _, d in DIL_CFG:
            qd, kd, vd = (_to_residue_major(a, batch, seq_len, d) for a in (qb, kb, vb))
            o_d, lse_d = _band_attention(qd, kd, vd, band_masks)
            dil += [_from_residue_major(o_d, batch, seq_len, d),
                    _from_residue_major(lse_d, batch, seq_len, d)]

        km, vm = _memkv(mem2, g_mem[layer].reshape(1, D_MODEL), w_mem_kv[layer].astype(BF16),
                        jnp.tile(g_qk_mem[layer, 1], LANES // HEAD_DIM).reshape(1, LANES), seg)
        o_mem = _mem_attention(qm, km, vm, batch, seq_len, mem_len)

        i = layer // 2
        moe = layer % 2 == 1
        w_router_pad = None
        if moe:
            w_router_pad = jnp.pad(w_router[i], ((0, 0), (0, LANES - N_EXPERTS)))
        outs = _outproj(o_na, dil, o_mem, h, g_out[layer].reshape(1, D_MODEL), w_out[layer].astype(BF16),
                        g_ffn[layer].reshape(1, D_MODEL), w_router_pad)
        if not moe:
            h, u = outs
            h = _dense_ffn(u, h, w_gate_dense[i], w_up_dense[i], w_down_dense[i])
        else:
            h, u, logits = outs
            row_tok, row_gate, blk_expert, n_used, pos = _route(logits[:, :N_EXPERTS])
            y = _moe_ffn(u, row_tok, row_gate, blk_expert, n_used, w_gate_moe[i], w_up_moe[i], w_down_moe[i])
            h = _combine(h, y, pos)
    return h.reshape(batch, seq_len, D_MODEL)
```

```python
import functools

import numpy as np
import jax
import jax.numpy as jnp
from jax import lax
from jax.experimental import pallas as pl
from jax.experimental.pallas import tpu as pltpu

F32 = jnp.float32
BF16 = jnp.bfloat16

D_MODEL = 1024
HEAD_DIM = 64
H_NA, H_DIL, H_MEM = 6, 6, 4
W_NA, W_DIL, W_MEM = H_NA * HEAD_DIM, H_DIL * HEAD_DIM, H_MEM * HEAD_DIM
IN_WIDTH = 3 * W_NA + 3 * W_DIL + W_MEM
GRID_W = 64
NA_KH, NA_KW = 8, 16
DIL_CFG = ((128, 1), (512, 4), (2048, 16))
ROPE_THETA = 10000.0
N_EXPERTS = 8
TOP_K = 2
RMS_EPS = 1e-6
NEG_INF = -1e30

LANES = 128
TM_PROJ = 512
NA_QROWS = 4
NA_QB = NA_QROWS * GRID_W
NA_KROWS = 12
NA_KB = NA_KROWS * GRID_W
DIL_QB = 256
DIL_HALF = 64
DIL_KB = DIL_QB + 2 * DIL_HALF
TQ_MEM = 512
TM_FFN = 1024
TF_DENSE = 256
TM_MOE = 1024
TF_MOE = 512
TC_COMBINE = 512
VMEM_LIMIT = 56 * 1024 * 1024


def _cparams(sem):
    return pltpu.CompilerParams(dimension_semantics=sem, vmem_limit_bytes=VMEM_LIMIT)


def _rms(x, g):
    return x * lax.rsqrt(jnp.mean(x * x, axis=-1, keepdims=True) + RMS_EPS) * g


_CHUNKS = (
    [(0, c, True, False) for c in range(3)] + [(1, c, True, False) for c in range(3)]
    + [(2, c, False, False) for c in range(3)]
    + [(3, c, True, True) for c in range(3)] + [(4, c, True, True) for c in range(3)]
    + [(5, c, False, False) for c in range(3)]
    + [(6, c, True, False) for c in range(2)]
)


def _inproj_kernel(h_ref, ga_ref, w_ref, gq_ref, cos_ref, sin_ref, seg_ref, *out_refs):
    u = _rms(h_ref[...], ga_ref[...]).astype(BF16)
    p = jnp.dot(u, w_ref[...], preferred_element_type=F32)
    seg = seg_ref[...]
    lane = lax.broadcasted_iota(jnp.int32, (1, LANES), 1)
    first_half = (lane % HEAD_DIM) < (HEAD_DIM // 2)
    for c, (oi, oc, norm, rope) in enumerate(_CHUNKS):
        x = p[:, c * LANES:(c + 1) * LANES]
        if norm:
            ss = jnp.dot((x * x).astype(BF16), seg, preferred_element_type=F32)
            x = x * lax.rsqrt(ss * (1.0 / HEAD_DIM) + RMS_EPS) * gq_ref[:, c * LANES:(c + 1) * LANES]
        if rope:
            swapped = jnp.where(first_half, pltpu.roll(x, LANES - HEAD_DIM // 2, 1),
                                pltpu.roll(x, HEAD_DIM // 2, 1))
            x = x * cos_ref[...] + swapped * sin_ref[...]
        out_refs[oi][:, oc * LANES:(oc + 1) * LANES] = x.astype(BF16)


def _inproj(h, g_attn, w_in_bf16, gq_row, cos_t, sin_t, seg, seq_len):
    n = h.shape[0]
    tm = TM_PROJ
    tblocks = seq_len // tm
    widths = (W_NA, W_NA, W_NA, W_DIL, W_DIL, W_DIL, W_MEM)
    return pl.pallas_call(
        _inproj_kernel,
        grid=(n // tm,),
        in_specs=[
            pl.BlockSpec((tm, D_MODEL), lambda i: (i, 0)),
            pl.BlockSpec((1, D_MODEL), lambda i: (0, 0)),
            pl.BlockSpec((D_MODEL, IN_WIDTH), lambda i: (0, 0)),
            pl.BlockSpec((1, IN_WIDTH), lambda i: (0, 0)),
            pl.BlockSpec((tm, LANES), lambda i: (i % tblocks, 0)),
            pl.BlockSpec((tm, LANES), lambda i: (i % tblocks, 0)),
            pl.BlockSpec((LANES, LANES), lambda i: (0, 0)),
        ],
        out_specs=[pl.BlockSpec((tm, w), lambda i: (i, 0)) for w in widths],
        out_shape=[jax.ShapeDtypeStruct((n, w), BF16) for w in widths],
        compiler_params=_cparams(("parallel",)),
        name="inproj",
    )(h, g_attn, w_in_bf16, gq_row, cos_t, sin_t, seg)


def _pair_attention(q, k, v, bias_fn, want_lse):
    lane = lax.broadcasted_iota(jnp.int32, (1, LANES), 1)
    v_ones = jnp.concatenate([v, jnp.ones_like(v)], axis=1)
    out = None
    lse_out = None
    for hh in range(2):
        mine = (lane < HEAD_DIM) if hh == 0 else (lane >= HEAD_DIM)
        qh = jnp.where(mine, q, jnp.zeros_like(q))
        s = lax.dot_general(qh, k, (((1,), (1,)), ((), ())), preferred_element_type=F32)
        b = bias_fn(hh)
        if b is not None:
            s = s + b
        m = jnp.max(s, axis=-1, keepdims=True)
        p = jnp.exp(s - m).astype(BF16)
        ol = jnp.dot(p, v_ones, preferred_element_type=F32)
        l = ol[:, LANES:]
        o = ol[:, :LANES] / l
        out = o if out is None else jnp.where(mine, o, out)
        if want_lse:
            lse = m + jnp.log(l)
            lse_out = lse if lse_out is None else jnp.where(mine, lse, lse_out)
    return out, lse_out


def _na_kernel(q_ref, k_ref, v_ref, bias_ref, o_ref, *, rows):
    i = pl.program_id(1)
    srow = jnp.clip(i * NA_QROWS - NA_KH // 2, 0, rows - NA_KROWS)
    start = pl.multiple_of(srow * GRID_W, GRID_W)
    for p in range(W_NA // LANES):
        sl = slice(p * LANES, (p + 1) * LANES)
        k = k_ref[pl.ds(start, NA_KB), sl]
        v = v_ref[pl.ds(start, NA_KB), sl]
        o, _ = _pair_attention(q_ref[:, sl], k, v, lambda hh: bias_ref[0, 2 * p + hh], False)
        o_ref[:, sl] = o


def _na_attention(q, k, v, bias, batch, seq_len):
    rows = seq_len // GRID_W
    nblk = seq_len // NA_QB
    q3, k3, v3 = (a.reshape(batch, seq_len, W_NA) for a in (q, k, v))

    def variant(i):
        return (i > 0).astype(jnp.int32) + (i == nblk - 1).astype(jnp.int32)

    out = pl.pallas_call(
        functools.partial(_na_kernel, rows=rows),
        grid=(batch, nblk),
        in_specs=[
            pl.BlockSpec((None, NA_QB, W_NA), lambda b, i: (b, i, 0)),
            pl.BlockSpec((None, seq_len, W_NA), lambda b, i: (b, 0, 0)),
            pl.BlockSpec((None, seq_len, W_NA), lambda b, i: (b, 0, 0)),
            pl.BlockSpec((1, H_NA, NA_QB, NA_KB), lambda b, i: (variant(i), 0, 0, 0)),
        ],
        out_specs=pl.BlockSpec((None, NA_QB, W_NA), lambda b, i: (b, i, 0)),
        out_shape=jax.ShapeDtypeStruct((batch, seq_len, W_NA), F32),
        compiler_params=_cparams(("parallel", "arbitrary")),
        name="na_attention",
    )(q3, k3, v3, bias)
    return out.reshape(batch * seq_len, W_NA)


def _na_bias_tables(rpb, rows):
    nblk = rows // NA_QROWS
    hi = lax.Precision.HIGHEST
    qc = np.arange(GRID_W)[:, None]
    kc = np.arange(GRID_W)[None, :]
    cs = np.clip(qc - NA_KW // 2, 0, GRID_W - NA_KW)
    col_valid = (kc >= cs) & (kc < cs + NA_KW)
    col_rel = kc - qc + (NA_KW - 1)
    col_sel = (col_rel[None] == np.arange(2 * NA_KW - 1)[:, None, None]).astype(np.float32)
    t1 = jnp.einsum('hab,bqk->haqk', rpb, jnp.asarray(col_sel), precision=hi)
    row_sel, row_valid = [], []
    for blk in (0, 1, nblk - 1):
        r0 = blk * NA_QROWS
        srow = int(np.clip(r0 - NA_KH // 2, 0, rows - NA_KROWS))
        qr = (r0 + np.arange(NA_QROWS))[:, None]
        kr = (srow + np.arange(NA_KROWS))[None, :]
        rs = np.clip(qr - NA_KH // 2, 0, rows - NA_KH)
        row_valid.append((kr >= rs) & (kr < rs + NA_KH))
        row_rel = kr - qr + (NA_KH - 1)
        row_sel.append((row_rel[..., None] == np.arange(2 * NA_KH - 1)).astype(np.float32))
    vals = jnp.einsum('vrsa,haqk->vhrqsk', jnp.asarray(np.stack(row_sel)), t1, precision=hi)
    valid = (np.stack(row_valid)[:, :, None, :, None] & col_valid[None, None, :, None, :])
    out = jnp.where(jnp.asarray(valid)[:, None], vals, NEG_INF)
    return out.reshape(3, H_NA, NA_QB, NA_KB).astype(F32)


def _band_kernel(q_ref, k_ref, v_ref, mask_ref, o_ref, lse_ref, *, length):
    j = pl.program_id(1)
    start = jnp.clip(j * DIL_QB - DIL_HALF, 0, length - DIL_KB)
    start = pl.multiple_of(start, DIL_HALF)
    for p in range(W_DIL // LANES):
        sl = slice(p * LANES, (p + 1) * LANES)
        k = k_ref[pl.ds(start, DIL_KB), sl]
        v = v_ref[pl.ds(start, DIL_KB), sl]
        o, lse = _pair_attention(q_ref[:, sl], k, v, lambda hh: mask_ref[0], True)
        o_ref[:, sl] = o
        lse_ref[:, sl] = lse


def _band_attention(q, k, v, masks, batch, seq_len, d):
    length = seq_len // d
    nblk = length // DIL_QB
    qv, kv, vv = (a.reshape(batch, length, d * W_DIL) for a in (q, k, v))

    def variant(j):
        return (j > 0).astype(jnp.int32) + (j == nblk - 1).astype(jnp.int32)

    blk = pl.BlockSpec((None, DIL_QB, W_DIL), lambda s, j: (s // d, j, s % d))
    seq = pl.BlockSpec((None, length, W_DIL), lambda s, j: (s // d, 0, s % d))
    o, lse = pl.pallas_call(
        functools.partial(_band_kernel, length=length),
        grid=(batch * d, nblk),
        in_specs=[blk, seq, seq, pl.BlockSpec((1, DIL_QB, DIL_KB), lambda s, j: (variant(j), 0, 0))],
        out_specs=[blk, blk],
        out_shape=[jax.ShapeDtypeStruct((batch, length, d * W_DIL), F32)] * 2,
        compiler_params=_cparams(("parallel", "arbitrary")),
        name="band_attention",
    )(qv, kv, vv, masks)
    return o.reshape(batch * seq_len, W_DIL), lse.reshape(batch * seq_len, W_DIL)


def _band_masks():
    qq = np.arange(DIL_QB)[:, None]
    kk = np.arange(DIL_KB)[None, :]
    out = []
    for off in (0, -DIL_HALF, DIL_QB - DIL_KB):
        rel = kk + off - qq
        out.append(np.where(np.abs(rel) <= DIL_HALF, 0.0, NEG_INF))
    return jnp.asarray(np.stack(out), F32)


def _memkv_kernel(mem_ref, g_ref, w_ref, gk_ref, seg_ref, km_ref, vm_ref):
    mn = _rms(mem_ref[...], g_ref[...]).astype(BF16)
    kv = jnp.dot(mn, w_ref[...], preferred_element_type=F32)
    for c in range(W_MEM // LANES):
        x = kv[:, c * LANES:(c + 1) * LANES]
        ss = jnp.dot((x * x).astype(BF16), seg_ref[...], preferred_element_type=F32)
        x = x * lax.rsqrt(ss * (1.0 / HEAD_DIM) + RMS_EPS) * gk_ref[...]
        km_ref[:, c * LANES:(c + 1) * LANES] = x.astype(BF16)
    vm_ref[...] = kv[:, W_MEM:].astype(BF16)


def _memkv(mem2, g_mem, w_kv_bf16, gk_row, seg):
    rows = mem2.shape[0]
    full = lambda shape: pl.BlockSpec(shape, lambda i: (0,) * len(shape))
    return pl.pallas_call(
        _memkv_kernel,
        grid=(1,),
        in_specs=[full((rows, D_MODEL)), full((1, D_MODEL)), full((D_MODEL, 2 * W_MEM)),
                  full((1, LANES)), full((LANES, LANES))],
        out_specs=[full((rows, W_MEM)), full((rows, W_MEM))],
        out_shape=[jax.ShapeDtypeStruct((rows, W_MEM), BF16)] * 2,
        compiler_params=_cparams(("arbitrary",)),
        name="mem_kv",
    )(mem2, g_mem, w_kv_bf16, gk_row, seg)


def _memattn_kernel(q_ref, k_ref, v_ref, o_ref):
    for c in range(W_MEM // LANES):
        sl = slice(c * LANES, (c + 1) * LANES)
        o, _ = _pair_attention(q_ref[:, sl], k_ref[:, sl], v_ref[:, sl], lambda hh: None, False)
        o_ref[:, sl] = o


def _mem_attention(qm, km, vm, batch, seq_len, mem_len):
    tq = TQ_MEM
    nq = seq_len // tq
    return pl.pallas_call(
        _memattn_kernel,
        grid=(batch, nq),
        in_specs=[
            pl.BlockSpec((tq, W_MEM), lambda b, i: (b * nq + i, 0)),
            pl.BlockSpec((mem_len, W_MEM), lambda b, i: (b, 0)),
            pl.BlockSpec((mem_len, W_MEM), lambda b, i: (b, 0)),
        ],
        out_specs=pl.BlockSpec((tq, W_MEM), lambda b, i: (b * nq + i, 0)),
        out_shape=jax.ShapeDtypeStruct((batch * seq_len, W_MEM), F32),
        compiler_params=_cparams(("parallel", "arbitrary")),
        name="mem_attention",
    )(qm, km, vm)


def _outproj_kernel(ona_ref, o1_ref, l1_ref, o2_ref, l2_ref, o3_ref, l3_ref, om_ref, h_ref,
                    go_ref, w_ref, gf_ref, *rest, with_router):
    if with_router:
        wr_ref, hn_ref, u_ref, lg_ref = rest
    else:
        hn_ref, u_ref = rest
    l1, l2, l3 = l1_ref[...], l2_ref[...], l3_ref[...]
    mx = jnp.maximum(jnp.maximum(l1, l2), l3)
    e1, e2, e3 = jnp.exp(l1 - mx), jnp.exp(l2 - mx), jnp.exp(l3 - mx)
    odil = (e1 * o1_ref[...] + e2 * o2_ref[...] + e3 * o3_ref[...]) / (e1 + e2 + e3)
    go = go_ref[...]
    mixed = jnp.concatenate([
        _rms(ona_ref[...], go[:, :W_NA]),
        _rms(odil, go[:, W_NA:W_NA + W_DIL]),
        _rms(om_ref[...], go[:, W_NA + W_DIL:]),
    ], axis=-1).astype(BF16)
    hn = h_ref[...] + jnp.dot(mixed, w_ref[...], preferred_element_type=F32)
    hn_ref[...] = hn
    u = _rms(hn, gf_ref[...])
    if with_router:
        u_ref[...] = u
        lg_ref[...] = jnp.dot(u, wr_ref[...], preferred_element_type=F32,
                              precision=lax.Precision.HIGHEST)
    else:
        u_ref[...] = u.astype(BF16)


def _outproj(ona, dil, om, h, g_out, w_out_bf16, g_ffn, w_router_pad):
    n = h.shape[0]
    tm = TM_PROJ
    with_router = w_router_pad is not None
    row = lambda w: pl.BlockSpec((tm, w), lambda i: (i, 0))
    full = lambda shape: pl.BlockSpec(shape, lambda i: (0,) * len(shape))
    in_specs = ([row(W_NA)] + [row(W_DIL)] * 6 + [row(W_MEM), row(D_MODEL),
                full((1, D_MODEL)), full((D_MODEL, D_MODEL)), full((1, D_MODEL))])
    args = [ona, *dil, om, h, g_out, w_out_bf16, g_ffn]
    out_specs = [row(D_MODEL), row(D_MODEL)]
    out_shape = [jax.ShapeDtypeStruct((n, D_MODEL), F32),
                 jax.ShapeDtypeStruct((n, D_MODEL), F32 if with_router else BF16)]
    if with_router:
        in_specs.append(full((D_MODEL, LANES)))
        args.append(w_router_pad)
        out_specs.append(row(LANES))
        out_shape.append(jax.ShapeDtypeStruct((n, LANES), F32))
    return pl.pallas_call(
        functools.partial(_outproj_kernel, with_router=with_router),
        grid=(n // tm,),
        in_specs=in_specs,
        out_specs=out_specs,
        out_shape=out_shape,
        compiler_params=_cparams(("parallel",)),
        name="outproj",
    )(*args)


def _swiglu_step(x, wg_ref, wu_ref, wd_ref):
    g = jnp.dot(x, wg_ref[...].astype(BF16), preferred_element_type=F32)
    u = jnp.dot(x, wu_ref[...].astype(BF16), preferred_element_type=F32)
    hm = (g * jax.nn.sigmoid(g) * u).astype(BF16)
    return jnp.dot(hm, wd_ref[...].astype(BF16), preferred_element_type=F32)


def _dense_ffn_kernel(u_ref, h_ref, wg_ref, wu_ref, wd_ref, o_ref, acc_ref):
    j = pl.program_id(1)

    @pl.when(j == 0)
    def _():
        acc_ref[...] = h_ref[...]

    acc_ref[...] += _swiglu_step(u_ref[...], wg_ref, wu_ref, wd_ref)

    @pl.when(j == pl.num_programs(1) - 1)
    def _():
        o_ref[...] = acc_ref[...]


def _dense_ffn(u, h, wg, wu, wd):
    n = h.shape[0]
    d_ff = wg.shape[1]
    tm, tf = TM_FFN, TF_DENSE
    return pl.pallas_call(
        _dense_ffn_kernel,
        grid=(n // tm, d_ff // tf),
        in_specs=[
            pl.BlockSpec((tm, D_MODEL), lambda i, j: (i, 0)),
            pl.BlockSpec((tm, D_MODEL), lambda i, j: (i, 0)),
            pl.BlockSpec((D_MODEL, tf), lambda i, j: (0, j)),
            pl.BlockSpec((D_MODEL, tf), lambda i, j: (0, j)),
            pl.BlockSpec((tf, D_MODEL), lambda i, j: (j, 0)),
        ],
        out_specs=pl.BlockSpec((tm, D_MODEL), lambda i, j: (i, 0)),
        out_shape=jax.ShapeDtypeStruct((n, D_MODEL), F32),
        scratch_shapes=[pltpu.VMEM((tm, D_MODEL), F32)],
        compiler_params=_cparams(("parallel", "arbitrary")),
        name="dense_ffn",
    )(u, h, wg, wu, wd)


def _moe_ffn_kernel(be_ref, nu_ref, tok_ref, u_hbm, gate_ref, wg_ref, wu_ref, wd_ref, y_ref,
                    xbuf, xb, acc_ref, sem):
    i = pl.program_id(0)
    j = pl.program_id(1)
    active = i < nu_ref[0]
    tm = xbuf.shape[0]

    @pl.when(jnp.logical_and(active, j == 0))
    def _():
        def issue(r, carry):
            tok = tok_ref[0, 0, r]
            pltpu.make_async_copy(u_hbm.at[pl.ds(tok, 1), :], xbuf.at[pl.ds(r, 1), :], sem).start()
            return carry

        lax.fori_loop(0, tm, issue, 0)
        pltpu.make_async_copy(u_hbm.at[pl.ds(0, tm), :], xbuf, sem).wait()
        xb[...] = xbuf[...].astype(BF16)
        acc_ref[...] = jnp.zeros_like(acc_ref)

    @pl.when(active)
    def _():
        acc_ref[...] += _swiglu_step(xb[...], wg_ref, wu_ref, wd_ref)

    last = j == pl.num_programs(1) - 1

    @pl.when(jnp.logical_and(active, last))
    def _():
        y_ref[...] = acc_ref[...] * gate_ref[...]

    @pl.when(jnp.logical_and(jnp.logical_not(active), last))
    def _():
        y_ref[...] = jnp.zeros_like(y_ref)


def _moe_ffn(u, row_tok, row_gate, blk_expert, n_used, wg, wu, wd):
    n_rows = row_tok.shape[0]
    tm, tf = TM_MOE, TF_MOE
    nblk = n_rows // tm
    d_ff = wg.shape[2]
    nj = d_ff // tf

    def jeff(i, j, nu):
        return jnp.where(i < nu[0], j, nj - 1)

    grid_spec = pltpu.PrefetchScalarGridSpec(
        num_scalar_prefetch=2,
        grid=(nblk, nj),
        in_specs=[
            pl.BlockSpec((1, 1, tm), lambda i, j, be, nu: (i, 0, 0), memory_space=pltpu.SMEM),
            pl.BlockSpec(memory_space=pl.ANY),
            pl.BlockSpec((tm, 1), lambda i, j, be, nu: (i, 0)),
            pl.BlockSpec((None, D_MODEL, tf), lambda i, j, be, nu: (be[i], 0, jeff(i, j, nu))),
            pl.BlockSpec((None, D_MODEL, tf), lambda i, j, be, nu: (be[i], 0, jeff(i, j, nu))),
            pl.BlockSpec((None, tf, D_MODEL), lambda i, j, be, nu: (be[i], jeff(i, j, nu), 0)),
        ],
        out_specs=pl.BlockSpec((tm, D_MODEL), lambda i, j, be, nu: (i, 0)),
        scratch_shapes=[
            pltpu.VMEM((tm, D_MODEL), F32),
            pltpu.VMEM((tm, D_MODEL), BF16),
            pltpu.VMEM((tm, D_MODEL), F32),
            pltpu.SemaphoreType.DMA(()),
        ],
    )
    return pl.pallas_call(
        _moe_ffn_kernel,
        grid_spec=grid_spec,
        out_shape=jax.ShapeDtypeStruct((n_rows, D_MODEL), F32),
        compiler_params=_cparams(("arbitrary", "arbitrary")),
        name="moe_ffn",
    )(blk_expert, n_used, row_tok.reshape(nblk, 1, tm), u, row_gate.reshape(n_rows, 1), wg, wu, wd)


def _combine_kernel(p0_ref, p1_ref, h_ref, y_hbm, o_ref, buf0, buf1, sem):
    tc = buf0.shape[0]

    def issue(r, carry):
        pltpu.make_async_copy(y_hbm.at[pl.ds(p0_ref[0, 0, r], 1), :], buf0.at[pl.ds(r, 1), :],
                              sem.at[0]).start()
        pltpu.make_async_copy(y_hbm.at[pl.ds(p1_ref[0, 0, r], 1), :], buf1.at[pl.ds(r, 1), :],
                              sem.at[1]).start()
        return carry

    lax.fori_loop(0, tc, issue, 0)
    pltpu.make_async_copy(y_hbm.at[pl.ds(0, tc), :], buf0, sem.at[0]).wait()
    pltpu.make_async_copy(y_hbm.at[pl.ds(0, tc), :], buf1, sem.at[1]).wait()
    o_ref[...] = h_ref[...] + buf0[...] + buf1[...]


def _combine(h, y, pos):
    n = h.shape[0]
    tc = TC_COMBINE
    nblk = n // tc
    p0 = pos[:, 0].reshape(nblk, 1, tc)
    p1 = pos[:, 1].reshape(nblk, 1, tc)
    smem = pl.BlockSpec((1, 1, tc), lambda i: (i, 0, 0), memory_space=pltpu.SMEM)
    return pl.pallas_call(
        _combine_kernel,
        grid=(nblk,),
        in_specs=[smem, smem, pl.BlockSpec((tc, D_MODEL), lambda i: (i, 0)),
                  pl.BlockSpec(memory_space=pl.ANY)],
        out_specs=pl.BlockSpec((tc, D_MODEL), lambda i: (i, 0)),
        out_shape=jax.ShapeDtypeStruct((n, D_MODEL), F32),
        scratch_shapes=[pltpu.VMEM((tc, D_MODEL), F32), pltpu.VMEM((tc, D_MODEL), F32),
                        pltpu.SemaphoreType.DMA((2,))],
        compiler_params=_cparams(("arbitrary",)),
        name="moe_combine",
    )(p0, p1, h, y)


def _route(logits):
    n = logits.shape[0]
    top_logit, top_idx = lax.top_k(logits, TOP_K)
    gates = jax.nn.softmax(top_logit, axis=-1)
    e_flat = top_idx.reshape(n * TOP_K).astype(jnp.int32)
    onehot = (e_flat[:, None] == jnp.arange(N_EXPERTS, dtype=jnp.int32)[None, :]).astype(jnp.int32)
    csum = jnp.cumsum(onehot, axis=0)
    counts = csum[-1]
    rank = jnp.sum(onehot * csum, axis=1) - 1
    padded = (counts + TM_MOE - 1) // TM_MOE * TM_MOE
    pend = jnp.cumsum(padded)
    pstart = pend - padded
    dest = pstart[e_flat] + rank
    n_rows = n * TOP_K + N_EXPERTS * TM_MOE
    nblk = n_rows // TM_MOE
    tok_flat = jnp.arange(n * TOP_K, dtype=jnp.int32) // TOP_K
    row_tok = jnp.zeros((n_rows,), jnp.int32).at[dest].set(tok_flat)
    row_gate = jnp.zeros((n_rows,), F32).at[dest].set(gates.reshape(-1))
    n_used = (pend[-1] // TM_MOE).astype(jnp.int32)
    blk_start = jnp.arange(nblk, dtype=jnp.int32) * TM_MOE
    blk_start = jnp.minimum(blk_start, (n_used - 1) * TM_MOE)
    blk_expert = jnp.minimum(
        jnp.sum((blk_start[:, None] >= pend[None, :]).astype(jnp.int32), axis=1), N_EXPERTS - 1)
    return row_tok, row_gate, blk_expert.astype(jnp.int32), n_used.reshape(1), dest.reshape(n, TOP_K)


def _rope_tables(seq_len):
    half = HEAD_DIM // 2
    inv_freq = jnp.power(ROPE_THETA, -(2.0 / HEAD_DIM) * jnp.arange(half, dtype=F32))
    ang = jnp.arange(seq_len, dtype=F32)[:, None] * inv_freq[None, :]
    cos, sin = jnp.cos(ang), jnp.sin(ang)
    cos_t = jnp.tile(cos, (1, LANES // half))
    sin_t = jnp.tile(jnp.concatenate([-sin, sin], axis=1), (1, LANES // HEAD_DIM))
    return cos_t, sin_t


def kernel(x, mem, g_attn, w_in, g_qk_na, rpb_na, g_qk_dil, g_mem, w_mem_kv, g_qk_mem, g_out, w_out,
           g_ffn, w_gate_dense, w_up_dense, w_down_dense, w_router, w_gate_moe, w_up_moe, w_down_moe):
    batch, seq_len, _ = x.shape
    mem_len = mem.shape[1]
    depth = g_attn.shape[0]
    n = batch * seq_len
    scale = HEAD_DIM ** -0.5

    cos_t, sin_t = _rope_tables(seq_len)
    seg = jnp.asarray(np.kron(np.eye(LANES // HEAD_DIM), np.ones((HEAD_DIM, HEAD_DIM))), BF16)
    band_masks = _band_masks()
    ones = lambda w: jnp.ones((w,), F32)

    h = x.reshape(n, D_MODEL)
    mem2 = mem.reshape(batch * mem_len, D_MODEL)
    for layer in range(depth):
        gq_row = jnp.concatenate([
            jnp.tile(g_qk_na[layer, 0] * scale, H_NA), jnp.tile(g_qk_na[layer, 1], H_NA), ones(W_NA),
            jnp.tile(g_qk_dil[layer, 0] * scale, H_DIL), jnp.tile(g_qk_dil[layer, 1], H_DIL), ones(W_DIL),
            jnp.tile(g_qk_mem[layer, 0] * scale, H_MEM)]).reshape(1, IN_WIDTH)
        qa, ka, va, qb, kb, vb, qm = _inproj(
            h, g_attn[layer].reshape(1, D_MODEL), w_in[layer].astype(BF16), gq_row, cos_t, sin_t, seg,
            seq_len)

        o_na = _na_attention(qa, ka, va, _na_bias_tables(rpb_na[layer], seq_len // GRID_W),
                             batch, seq_len)

        dil = []
        for _, d in DIL_CFG:
            dil += _band_attention(qb, kb, vb, band_masks, batch, seq_len, d)

        km, vm = _memkv(mem2, g_mem[layer].reshape(1, D_MODEL), w_mem_kv[layer].astype(BF16),
                        jnp.tile(g_qk_mem[layer, 1], LANES // HEAD_DIM).reshape(1, LANES), seg)
        o_mem = _mem_attention(qm, km, vm, batch, seq_len, mem_len)

        i = layer // 2
        moe = layer % 2 == 1
        w_router_pad = None
        if moe:
            w_router_pad = jnp.pad(w_router[i], ((0, 0), (0, LANES - N_EXPERTS)))
        outs = _outproj(o_na, dil, o_mem, h, g_out[layer].reshape(1, D_MODEL), w_out[layer].astype(BF16),
                        g_ffn[layer].reshape(1, D_MODEL), w_router_pad)
        if not moe:
            h, u = outs
            h = _dense_ffn(u, h, w_gate_dense[i], w_up_dense[i], w_down_dense[i])
        else:
            h, u, logits = outs
            row_tok, row_gate, blk_expert, n_used, pos = _route(logits[:, :N_EXPERTS])
            y = _moe_ffn(u, row_tok, row_gate, blk_expert, n_used, w_gate_moe[i], w_up_moe[i], w_down_moe[i])
            h = _combine(h, y, pos)
    return h.reshape(batch, seq_len, D_MODEL)
```

```python
import functools

import numpy as np
import jax
import jax.numpy as jnp
from jax import lax
from jax.experimental import pallas as pl
from jax.experimental.pallas import tpu as pltpu

F32 = jnp.float32
BF16 = jnp.bfloat16

D_MODEL = 1024
HEAD_DIM = 64
H_NA, H_DIL, H_MEM = 6, 6, 4
W_NA, W_DIL, W_MEM = H_NA * HEAD_DIM, H_DIL * HEAD_DIM, H_MEM * HEAD_DIM
IN_WIDTH = 3 * W_NA + 3 * W_DIL + W_MEM
GRID_W = 64
NA_KH, NA_KW = 8, 16
DIL_CFG = ((128, 1), (512, 4), (2048, 16))
ROPE_THETA = 10000.0
N_EXPERTS = 8
TOP_K = 2
RMS_EPS = 1e-6
NEG_INF = -1e30

LANES = 128
TM_PROJ = 512
NA_QROWS = 4
NA_QB = NA_QROWS * GRID_W
NA_KROWS = 12
NA_KB = NA_KROWS * GRID_W
DIL_QB = 256
DIL_HALF = 64
DIL_KB = DIL_QB + 2 * DIL_HALF
TQ_MEM = 512
TM_FFN = 1024
TF_DENSE = 256
TM_MOE = 1024
TF_MOE = 512
MOE_CHUNK = 256
TC_COMBINE = 512
ZERO_ROWS = 256
VMEM_LIMIT = 56 * 1024 * 1024


def _cparams(sem):
    return pltpu.CompilerParams(dimension_semantics=sem, vmem_limit_bytes=VMEM_LIMIT)


def _rms(x, g):
    return x * lax.rsqrt(jnp.mean(x * x, axis=-1, keepdims=True) + RMS_EPS) * g


_CHUNKS = (
    [(0, c, True, False) for c in range(3)] + [(1, c, True, False) for c in range(3)]
    + [(2, c, False, False) for c in range(3)]
    + [(3, c, True, True) for c in range(3)] + [(4, c, True, True) for c in range(3)]
    + [(5, c, False, False) for c in range(3)]
    + [(6, c, True, False) for c in range(2)]
)


DILATIONS = tuple(d for _, d in DIL_CFG)
_RESIDUE_DILS = tuple(d for d in DILATIONS if d > 1)
_N_FLAT_OUT = 7


def _inproj_kernel(h_ref, ga_ref, w_ref, gq_ref, cos_ref, sin_ref, seg_ref, *rest):
    out_refs, stage_ref = rest[:-1], rest[-1]
    tm = h_ref.shape[0]
    u = _rms(h_ref[...], ga_ref[...]).astype(BF16)
    p = jnp.dot(u, w_ref[...], preferred_element_type=F32)
    seg = seg_ref[...]
    lane = lax.broadcasted_iota(jnp.int32, (1, LANES), 1)
    first_half = (lane % HEAD_DIM) < (HEAD_DIM // 2)
    for c, (oi, oc, norm, rope) in enumerate(_CHUNKS):
        x = p[:, c * LANES:(c + 1) * LANES]
        if norm:
            ss = jnp.dot((x * x).astype(BF16), seg, preferred_element_type=F32)
            x = x * lax.rsqrt(ss * (1.0 / HEAD_DIM) + RMS_EPS) * gq_ref[:, c * LANES:(c + 1) * LANES]
        if rope:
            swapped = jnp.where(first_half, pltpu.roll(x, LANES - HEAD_DIM // 2, 1),
                                pltpu.roll(x, HEAD_DIM // 2, 1))
            x = x * cos_ref[...] + swapped * sin_ref[...]
        out_refs[oi][:, oc * LANES:(oc + 1) * LANES] = x.astype(BF16)
        if 3 <= oi <= 5:
            st = (oi - 3) * (W_DIL // LANES) + oc
            stage_ref[st] = x
            for di, d in enumerate(_RESIDUE_DILS):
                dst = out_refs[_N_FLAT_OUT + 3 * di + (oi - 3)]
                for r in range(d):
                    dst[r, :, oc * LANES:(oc + 1) * LANES] = stage_ref[
                        st, pl.ds(r, tm // d, stride=d), :].astype(BF16)


def _inproj(h, g_attn, w_in_bf16, gq_row, cos_t, sin_t, seg, batch, seq_len):
    n = h.shape[0]
    tm = TM_PROJ
    tblocks = seq_len // tm
    widths = (W_NA, W_NA, W_NA, W_DIL, W_DIL, W_DIL, W_MEM)
    out_specs = [pl.BlockSpec((tm, w), lambda i: (i, 0)) for w in widths]
    out_shape = [jax.ShapeDtypeStruct((n, w), BF16) for w in widths]
    for d in _RESIDUE_DILS:
        out_specs += [pl.BlockSpec((None, d, tm // d, W_DIL),
                                   lambda i: (i // tblocks, 0, i % tblocks, 0))] * 3
        out_shape += [jax.ShapeDtypeStruct((batch, d, seq_len // d, W_DIL), BF16)] * 3
    return pl.pallas_call(
        _inproj_kernel,
        grid=(n // tm,),
        in_specs=[
            pl.BlockSpec((tm, D_MODEL), lambda i: (i, 0)),
            pl.BlockSpec((1, D_MODEL), lambda i: (0, 0)),
            pl.BlockSpec((D_MODEL, IN_WIDTH), lambda i: (0, 0)),
            pl.BlockSpec((1, IN_WIDTH), lambda i: (0, 0)),
            pl.BlockSpec((tm, LANES), lambda i: (i % tblocks, 0)),
            pl.BlockSpec((tm, LANES), lambda i: (i % tblocks, 0)),
            pl.BlockSpec((LANES, LANES), lambda i: (0, 0)),
        ],
        out_specs=out_specs,
        out_shape=out_shape,
        scratch_shapes=[pltpu.VMEM((3 * W_DIL // LANES, tm, LANES), F32)],
        compiler_params=_cparams(("parallel",)),
        name="inproj",
    )(h, g_attn, w_in_bf16, gq_row, cos_t, sin_t, seg)


def _pair_attention(q, k, v, bias_fn, want_lse):
    lane = lax.broadcasted_iota(jnp.int32, (1, LANES), 1)
    v_ones = jnp.concatenate([v, jnp.ones_like(v)], axis=1)
    out = None
    lse_out = None
    for hh in range(2):
        mine = (lane < HEAD_DIM) if hh == 0 else (lane >= HEAD_DIM)
        qh = jnp.where(mine, q, jnp.zeros_like(q))
        s = lax.dot_general(qh, k, (((1,), (1,)), ((), ())), preferred_element_type=F32)
        b = bias_fn(hh)
        if b is not None:
            s = s + b
        m = jnp.max(s, axis=-1, keepdims=True)
        p = jnp.exp(s - m).astype(BF16)
        ol = jnp.dot(p, v_ones, preferred_element_type=F32)
        l = ol[:, LANES:]
        o = ol[:, :LANES] / l
        out = o if out is None else jnp.where(mine, o, out)
        if want_lse:
            lse = m + jnp.log(l)
            lse_out = lse if lse_out is None else jnp.where(mine, lse, lse_out)
    return out, lse_out


def _na_kernel(q_ref, k_ref, v_ref, bias_ref, o_ref, *, rows):
    i = pl.program_id(1)
    srow = jnp.clip(i * NA_QROWS - NA_KH // 2, 0, rows - NA_KROWS)
    start = pl.multiple_of(srow * GRID_W, GRID_W)
    for p in range(W_NA // LANES):
        sl = slice(p * LANES, (p + 1) * LANES)
        k = k_ref[pl.ds(start, NA_KB), sl]
        v = v_ref[pl.ds(start, NA_KB), sl]
        o, _ = _pair_attention(q_ref[:, sl], k, v, lambda hh: bias_ref[0, 2 * p + hh], False)
        o_ref[:, sl] = o


def _na_attention(q, k, v, bias, batch, seq_len):
    rows = seq_len // GRID_W
    nblk = seq_len // NA_QB
    q3, k3, v3 = (a.reshape(batch, seq_len, W_NA) for a in (q, k, v))

    def variant(i):
        return (i > 0).astype(jnp.int32) + (i == nblk - 1).astype(jnp.int32)

    out = pl.pallas_call(
        functools.partial(_na_kernel, rows=rows),
        grid=(batch, nblk),
        in_specs=[
            pl.BlockSpec((None, NA_QB, W_NA), lambda b, i: (b, i, 0)),
            pl.BlockSpec((None, seq_len, W_NA), lambda b, i: (b, 0, 0)),
            pl.BlockSpec((None, seq_len, W_NA), lambda b, i: (b, 0, 0)),
            pl.BlockSpec((1, H_NA, NA_QB, NA_KB), lambda b, i: (variant(i), 0, 0, 0)),
        ],
        out_specs=pl.BlockSpec((None, NA_QB, W_NA), lambda b, i: (b, i, 0)),
        out_shape=jax.ShapeDtypeStruct((batch, seq_len, W_NA), F32),
        compiler_params=_cparams(("parallel", "arbitrary")),
        name="na_attention",
    )(q3, k3, v3, bias)
    return out.reshape(batch * seq_len, W_NA)


def _na_bias_tables(rpb, rows):
    nblk = rows // NA_QROWS
    hi = lax.Precision.HIGHEST
    qc = np.arange(GRID_W)[:, None]
    kc = np.arange(GRID_W)[None, :]
    cs = np.clip(qc - NA_KW // 2, 0, GRID_W - NA_KW)
    col_valid = (kc >= cs) & (kc < cs + NA_KW)
    col_rel = kc - qc + (NA_KW - 1)
    col_sel = (col_rel[None] == np.arange(2 * NA_KW - 1)[:, None, None]).astype(np.float32)
    t1 = jnp.einsum('hab,bqk->haqk', rpb, jnp.asarray(col_sel), precision=hi)
    row_sel, row_valid = [], []
    for blk in (0, 1, nblk - 1):
        r0 = blk * NA_QROWS
        srow = int(np.clip(r0 - NA_KH // 2, 0, rows - NA_KROWS))
        qr = (r0 + np.arange(NA_QROWS))[:, None]
        kr = (srow + np.arange(NA_KROWS))[None, :]
        rs = np.clip(qr - NA_KH // 2, 0, rows - NA_KH)
        row_valid.append((kr >= rs) & (kr < rs + NA_KH))
        row_rel = kr - qr + (NA_KH - 1)
        row_sel.append((row_rel[..., None] == np.arange(2 * NA_KH - 1)).astype(np.float32))
    vals = jnp.einsum('vrsa,haqk->vhrqsk', jnp.asarray(np.stack(row_sel)), t1, precision=hi)
    valid = (np.stack(row_valid)[:, :, None, :, None] & col_valid[None, None, :, None, :])
    out = jnp.where(jnp.asarray(valid)[:, None], vals, NEG_INF)
    return out.reshape(3, H_NA, NA_QB, NA_KB).astype(F32)


def _band_kernel(q_ref, k_ref, v_ref, mask_ref, o_ref, lse_ref, *, length):
    j = pl.program_id(1)
    start = jnp.clip(j * DIL_QB - DIL_HALF, 0, length - DIL_KB)
    start = pl.multiple_of(start, DIL_HALF)
    for p in range(W_DIL // LANES):
        sl = slice(p * LANES, (p + 1) * LANES)
        k = k_ref[pl.ds(start, DIL_KB), sl]
        v = v_ref[pl.ds(start, DIL_KB), sl]
        o, lse = _pair_attention(q_ref[:, sl], k, v, lambda hh: mask_ref[0], True)
        o_ref[:, sl] = o
        lse_ref[:, sl] = lse


def _band_attention(q, k, v, masks):
    nseq, length, _ = q.shape
    nblk = length // DIL_QB

    def variant(j):
        return (j > 0).astype(jnp.int32) + (j == nblk - 1).astype(jnp.int32)

    blk = pl.BlockSpec((None, DIL_QB, W_DIL), lambda s, j: (s, j, 0))
    seq = pl.BlockSpec((None, length, W_DIL), lambda s, j: (s, 0, 0))
    return pl.pallas_call(
        functools.partial(_band_kernel, length=length),
        grid=(nseq, nblk),
        in_specs=[blk, seq, seq, pl.BlockSpec((1, DIL_QB, DIL_KB), lambda s, j: (variant(j), 0, 0))],
        out_specs=[blk, blk],
        out_shape=[jax.ShapeDtypeStruct((nseq, length, W_DIL), F32)] * 2,
        compiler_params=_cparams(("parallel", "arbitrary")),
        name="band_attention",
    )(q, k, v, masks)


def _band_masks():
    qq = np.arange(DIL_QB)[:, None]
    kk = np.arange(DIL_KB)[None, :]
    out = []
    for off in (0, -DIL_HALF, DIL_QB - DIL_KB):
        rel = kk + off - qq
        out.append(np.where(np.abs(rel) <= DIL_HALF, 0.0, NEG_INF))
    return jnp.asarray(np.stack(out), F32)


def _memkv_kernel(mem_ref, g_ref, w_ref, gk_ref, seg_ref, km_ref, vm_ref):
    mn = _rms(mem_ref[...], g_ref[...]).astype(BF16)
    kv = jnp.dot(mn, w_ref[...], preferred_element_type=F32)
    for c in range(W_MEM // LANES):
        x = kv[:, c * LANES:(c + 1) * LANES]
        ss = jnp.dot((x * x).astype(BF16), seg_ref[...], preferred_element_type=F32)
        x = x * lax.rsqrt(ss * (1.0 / HEAD_DIM) + RMS_EPS) * gk_ref[...]
        km_ref[:, c * LANES:(c + 1) * LANES] = x.astype(BF16)
    vm_ref[...] = kv[:, W_MEM:].astype(BF16)


def _memkv(mem2, g_mem, w_kv_bf16, gk_row, seg):
    rows = mem2.shape[0]
    full = lambda shape: pl.BlockSpec(shape, lambda i: (0,) * len(shape))
    return pl.pallas_call(
        _memkv_kernel,
        grid=(1,),
        in_specs=[full((rows, D_MODEL)), full((1, D_MODEL)), full((D_MODEL, 2 * W_MEM)),
                  full((1, LANES)), full((LANES, LANES))],
        out_specs=[full((rows, W_MEM)), full((rows, W_MEM))],
        out_shape=[jax.ShapeDtypeStruct((rows, W_MEM), BF16)] * 2,
        compiler_params=_cparams(("arbitrary",)),
        name="mem_kv",
    )(mem2, g_mem, w_kv_bf16, gk_row, seg)


def _memattn_kernel(q_ref, k_ref, v_ref, o_ref):
    for c in range(W_MEM // LANES):
        sl = slice(c * LANES, (c + 1) * LANES)
        o, _ = _pair_attention(q_ref[:, sl], k_ref[:, sl], v_ref[:, sl], lambda hh: None, False)
        o_ref[:, sl] = o


def _mem_attention(qm, km, vm, batch, seq_len, mem_len):
    tq = TQ_MEM
    nq = seq_len // tq
    return pl.pallas_call(
        _memattn_kernel,
        grid=(batch, nq),
        in_specs=[
            pl.BlockSpec((tq, W_MEM), lambda b, i: (b * nq + i, 0)),
            pl.BlockSpec((mem_len, W_MEM), lambda b, i: (b, 0)),
            pl.BlockSpec((mem_len, W_MEM), lambda b, i: (b, 0)),
        ],
        out_specs=pl.BlockSpec((tq, W_MEM), lambda b, i: (b * nq + i, 0)),
        out_shape=jax.ShapeDtypeStruct((batch * seq_len, W_MEM), F32),
        compiler_params=_cparams(("parallel", "arbitrary")),
        name="mem_attention",
    )(qm, km, vm)


def _outproj_kernel(ona_ref, *rest, with_router):
    nd = 2 * len(DILATIONS)
    dil_refs, rest = rest[:nd], rest[nd:]
    om_ref, h_ref, go_ref, w_ref, gf_ref = rest[:5]
    rest = rest[5:]
    if with_router:
        wr_ref, hn_ref, u_ref, lg_ref = rest[:4]
        stage_refs = rest[4:]
    else:
        hn_ref, u_ref = rest[:2]
        stage_refs = rest[2:]
    tm = h_ref.shape[0]
    vals = []
    si = 0
    for di, d in enumerate(DILATIONS):
        for ref in dil_refs[2 * di:2 * di + 2]:
            if d == 1:
                vals.append(ref[...])
            else:
                st = stage_refs[si]
                si += 1
                for r in range(d):
                    for c in range(W_DIL // LANES):
                        st[c, pl.ds(r, tm // d, stride=d), :] = ref[r, :, c * LANES:(c + 1) * LANES]
                vals.append(jnp.concatenate([st[c] for c in range(W_DIL // LANES)], axis=-1))
    os_, ls_ = vals[0::2], vals[1::2]
    mx = functools.reduce(jnp.maximum, ls_)
    es = [jnp.exp(l - mx) for l in ls_]
    odil = sum(e * o for e, o in zip(es, os_)) / sum(es)
    go = go_ref[...]
    mixed = jnp.concatenate([
        _rms(ona_ref[...], go[:, :W_NA]),
        _rms(odil, go[:, W_NA:W_NA + W_DIL]),
        _rms(om_ref[...], go[:, W_NA + W_DIL:]),
    ], axis=-1).astype(BF16)
    hn = h_ref[...] + jnp.dot(mixed, w_ref[...], preferred_element_type=F32)
    hn_ref[...] = hn
    u = _rms(hn, gf_ref[...])
    if with_router:
        u_ref[...] = u
        lg_ref[...] = jnp.dot(u, wr_ref[...], preferred_element_type=F32,
                              precision=lax.Precision.HIGHEST)
    else:
        u_ref[...] = u.astype(BF16)


def _outproj(ona, dil, om, h, g_out, w_out_bf16, g_ffn, w_router_pad, batch, seq_len):
    n = h.shape[0]
    tm = TM_PROJ
    tblocks = seq_len // tm
    with_router = w_router_pad is not None
    row = lambda w: pl.BlockSpec((tm, w), lambda i: (i, 0))
    full = lambda shape: pl.BlockSpec(shape, lambda i: (0,) * len(shape))
    in_specs = [row(W_NA)]
    args = [ona]
    scratch = []
    for di, d in enumerate(DILATIONS):
        for a in dil[2 * di:2 * di + 2]:
            if d == 1:
                in_specs.append(row(W_DIL))
                args.append(a.reshape(n, W_DIL))
            else:
                in_specs.append(pl.BlockSpec((None, d, tm // d, W_DIL),
                                             lambda i: (i // tblocks, 0, i % tblocks, 0)))
                args.append(a.reshape(batch, d, seq_len // d, W_DIL))
                scratch.append(pltpu.VMEM((W_DIL // LANES, tm, LANES), F32))
    in_specs += [row(W_MEM), row(D_MODEL), full((1, D_MODEL)), full((D_MODEL, D_MODEL)),
                 full((1, D_MODEL))]
    args += [om, h, g_out, w_out_bf16, g_ffn]
    out_specs = [row(D_MODEL), row(D_MODEL)]
    out_shape = [jax.ShapeDtypeStruct((n, D_MODEL), F32),
                 jax.ShapeDtypeStruct((n, D_MODEL), F32 if with_router else BF16)]
    if with_router:
        in_specs.append(full((D_MODEL, LANES)))
        args.append(w_router_pad)
        out_specs.append(row(LANES))
        out_shape.append(jax.ShapeDtypeStruct((n, LANES), F32))
    return pl.pallas_call(
        functools.partial(_outproj_kernel, with_router=with_router),
        grid=(n // tm,),
        in_specs=in_specs,
        out_specs=out_specs,
        out_shape=out_shape,
        scratch_shapes=scratch,
        compiler_params=_cparams(("parallel",)),
        name="outproj",
    )(*args)


def _swiglu_step(x, wg_ref, wu_ref, wd_ref):
    g = jnp.dot(x, wg_ref[...].astype(BF16), preferred_element_type=F32)
    u = jnp.dot(x, wu_ref[...].astype(BF16), preferred_element_type=F32)
    hm = (g * jax.nn.sigmoid(g) * u).astype(BF16)
    return jnp.dot(hm, wd_ref[...].astype(BF16), preferred_element_type=F32)


def _dense_ffn_kernel(u_ref, h_ref, wg_ref, wu_ref, wd_ref, o_ref, acc_ref):
    j = pl.program_id(1)

    @pl.when(j == 0)
    def _():
        acc_ref[...] = h_ref[...]

    acc_ref[...] += _swiglu_step(u_ref[...], wg_ref, wu_ref, wd_ref)

    @pl.when(j == pl.num_programs(1) - 1)
    def _():
        o_ref[...] = acc_ref[...]


def _dense_ffn(u, h, wg, wu, wd):
    n = h.shape[0]
    d_ff = wg.shape[1]
    tm, tf = TM_FFN, TF_DENSE
    return pl.pallas_call(
        _dense_ffn_kernel,
        grid=(n // tm, d_ff // tf),
        in_specs=[
            pl.BlockSpec((tm, D_MODEL), lambda i, j: (i, 0)),
            pl.BlockSpec((tm, D_MODEL), lambda i, j: (i, 0)),
            pl.BlockSpec((D_MODEL, tf), lambda i, j: (0, j)),
            pl.BlockSpec((D_MODEL, tf), lambda i, j: (0, j)),
            pl.BlockSpec((tf, D_MODEL), lambda i, j: (j, 0)),
        ],
        out_specs=pl.BlockSpec((tm, D_MODEL), lambda i, j: (i, 0)),
        out_shape=jax.ShapeDtypeStruct((n, D_MODEL), F32),
        scratch_shapes=[pltpu.VMEM((tm, D_MODEL), F32)],
        compiler_params=_cparams(("parallel", "arbitrary")),
        name="dense_ffn",
    )(u, h, wg, wu, wd)


def _dispatch_kernel(pend_ref, nu_ref, d0_ref, d1_ref, u_ref, x_hbm, zbuf, sem, zsem):
    i = pl.program_id(0)
    tc = u_ref.shape[0]
    zr = zbuf.shape[0]
    n_rows = x_hbm.shape[0]

    def zero_block(first_row):
        copies = [pltpu.make_async_copy(
            zbuf, x_hbm.at[pl.ds(pl.multiple_of(first_row + c * zr, zr), zr), :], zsem)
            for c in range(TM_MOE // zr)]
        for cp in copies:
            cp.start()
        for cp in copies:
            cp.wait()

    @pl.when(i == 0)
    def _():
        zbuf[...] = jnp.zeros_like(zbuf)
        prev = jnp.int32(0)
        for e in range(N_EXPERTS):
            end = pend_ref[e]

            @pl.when(end > prev)
            def _():
                zero_block(end - TM_MOE)

            prev = end

        def idle(blk, carry):
            zero_block(blk * TM_MOE)
            return carry

        lax.fori_loop(nu_ref[0], n_rows // TM_MOE, idle, 0)

    def issue(r, carry):
        src = u_ref.at[pl.ds(r, 1), :]
        pltpu.make_async_copy(src, x_hbm.at[pl.ds(d0_ref[0, 0, r], 1), :], sem).start()
        pltpu.make_async_copy(src, x_hbm.at[pl.ds(d1_ref[0, 0, r], 1), :], sem).start()
        return carry

    lax.fori_loop(0, tc, issue, 0)
    for _ in range(TOP_K):
        pltpu.make_async_copy(u_ref, x_hbm.at[pl.ds(0, tc), :], sem).wait()


def _dispatch(u, dest, pend, n_used, n_rows):
    n = u.shape[0]
    tc = TC_COMBINE
    nblk = n // tc
    d0 = dest[:, 0].reshape(nblk, 1, tc)
    d1 = dest[:, 1].reshape(nblk, 1, tc)
    smem = pl.BlockSpec((1, 1, tc), lambda i, pe, nu: (i, 0, 0), memory_space=pltpu.SMEM)
    grid_spec = pltpu.PrefetchScalarGridSpec(
        num_scalar_prefetch=2,
        grid=(nblk,),
        in_specs=[smem, smem, pl.BlockSpec((tc, D_MODEL), lambda i, pe, nu: (i, 0))],
        out_specs=pl.BlockSpec(memory_space=pl.ANY),
        scratch_shapes=[pltpu.VMEM((ZERO_ROWS, D_MODEL), F32), pltpu.SemaphoreType.DMA(()),
                        pltpu.SemaphoreType.DMA(())],
    )
    return pl.pallas_call(
        _dispatch_kernel,
        grid_spec=grid_spec,
        out_shape=jax.ShapeDtypeStruct((n_rows, D_MODEL), F32),
        compiler_params=_cparams(("arbitrary",)),
        name="moe_dispatch",
    )(pend, n_used, d0, d1, u)


def _moe_ffn_kernel(be_ref, nv_ref, nu_ref, x_ref, wg_ref, wu_ref, wd_ref, y_ref,
                    xb, wgb, wub, wdb, acc_ref):
    i = pl.program_id(0)
    j = pl.program_id(1)
    active = i < nu_ref[0]
    last = j == pl.num_programs(1) - 1
    tm = x_ref.shape[0]

    @pl.when(active)
    def _():
        @pl.when(j == 0)
        def _():
            xb[...] = x_ref[...].astype(BF16)
            acc_ref[...] = jnp.zeros_like(acc_ref)

        wgb[...] = wg_ref[...].astype(BF16)
        wub[...] = wu_ref[...].astype(BF16)
        wdb[...] = wd_ref[...].astype(BF16)
        nvalid = nv_ref[i]
        for c in range(tm // MOE_CHUNK):
            rows = slice(c * MOE_CHUNK, (c + 1) * MOE_CHUNK)

            @pl.when(c * MOE_CHUNK < nvalid)
            def _():
                x = xb[rows, :]
                g = jnp.dot(x, wgb[...], preferred_element_type=F32)
                u = jnp.dot(x, wub[...], preferred_element_type=F32)
                hm = (g * jax.nn.sigmoid(g) * u).astype(BF16)
                acc_ref[rows, :] += jnp.dot(hm, wdb[...], preferred_element_type=F32)

        @pl.when(last)
        def _():
            y_ref[...] = acc_ref[...]

    @pl.when(jnp.logical_and(jnp.logical_not(active), last))
    def _():
        y_ref[...] = jnp.zeros_like(y_ref)


def _moe_ffn(x_sorted, blk_expert, blk_valid, n_used, wg, wu, wd):
    n_rows = x_sorted.shape[0]
    tm, tf = TM_MOE, TF_MOE
    nblk = n_rows // tm
    d_ff = wg.shape[2]
    nj = d_ff // tf

    def jeff(i, j, nu):
        return jnp.where(i < nu[0], j, nj - 1)

    grid_spec = pltpu.PrefetchScalarGridSpec(
        num_scalar_prefetch=3,
        grid=(nblk, nj),
        in_specs=[
            pl.BlockSpec((tm, D_MODEL), lambda i, j, be, nv, nu: (jnp.minimum(i, nu[0] - 1), 0)),
            pl.BlockSpec((None, D_MODEL, tf), lambda i, j, be, nv, nu: (be[i], 0, jeff(i, j, nu))),
            pl.BlockSpec((None, D_MODEL, tf), lambda i, j, be, nv, nu: (be[i], 0, jeff(i, j, nu))),
            pl.BlockSpec((None, tf, D_MODEL), lambda i, j, be, nv, nu: (be[i], jeff(i, j, nu), 0)),
        ],
        out_specs=pl.BlockSpec((tm, D_MODEL), lambda i, j, be, nv, nu: (i, 0)),
        scratch_shapes=[
            pltpu.VMEM((tm, D_MODEL), BF16),
            pltpu.VMEM((D_MODEL, tf), BF16),
            pltpu.VMEM((D_MODEL, tf), BF16),
            pltpu.VMEM((tf, D_MODEL), BF16),
            pltpu.VMEM((tm, D_MODEL), F32),
        ],
    )
    return pl.pallas_call(
        _moe_ffn_kernel,
        grid_spec=grid_spec,
        out_shape=jax.ShapeDtypeStruct((n_rows, D_MODEL), F32),
        compiler_params=_cparams(("arbitrary", "arbitrary")),
        name="moe_ffn",
    )(blk_expert, blk_valid, n_used, x_sorted, wg, wu, wd)


def _combine_kernel(p0_ref, p1_ref, h_ref, g_ref, y_hbm, o_ref, buf0, buf1, sem):
    tc = buf0.shape[0]

    def issue(r, carry):
        pltpu.make_async_copy(y_hbm.at[pl.ds(p0_ref[0, 0, r], 1), :], buf0.at[pl.ds(r, 1), :],
                              sem.at[0]).start()
        pltpu.make_async_copy(y_hbm.at[pl.ds(p1_ref[0, 0, r], 1), :], buf1.at[pl.ds(r, 1), :],
                              sem.at[1]).start()
        return carry

    lax.fori_loop(0, tc, issue, 0)
    pltpu.make_async_copy(y_hbm.at[pl.ds(0, tc), :], buf0, sem.at[0]).wait()
    pltpu.make_async_copy(y_hbm.at[pl.ds(0, tc), :], buf1, sem.at[1]).wait()
    g = g_ref[...]
    o_ref[...] = h_ref[...] + g[:, 0:1] * buf0[...] + g[:, 1:2] * buf1[...]


def _combine(h, y, dest, gates):
    n = h.shape[0]
    tc = TC_COMBINE
    nblk = n // tc
    p0 = dest[:, 0].reshape(nblk, 1, tc)
    p1 = dest[:, 1].reshape(nblk, 1, tc)
    smem = pl.BlockSpec((1, 1, tc), lambda i: (i, 0, 0), memory_space=pltpu.SMEM)
    return pl.pallas_call(
        _combine_kernel,
        grid=(nblk,),
        in_specs=[smem, smem, pl.BlockSpec((tc, D_MODEL), lambda i: (i, 0)),
                  pl.BlockSpec((tc, TOP_K), lambda i: (i, 0)), pl.BlockSpec(memory_space=pl.ANY)],
        out_specs=pl.BlockSpec((tc, D_MODEL), lambda i: (i, 0)),
        out_shape=jax.ShapeDtypeStruct((n, D_MODEL), F32),
        scratch_shapes=[pltpu.VMEM((tc, D_MODEL), F32), pltpu.VMEM((tc, D_MODEL), F32),
                        pltpu.SemaphoreType.DMA((2,))],
        compiler_params=_cparams(("arbitrary",)),
        name="moe_combine",
    )(p0, p1, h, gates, y)


def _route(logits):
    n = logits.shape[0]
    top_logit, top_idx = lax.top_k(logits, TOP_K)
    gates = jax.nn.softmax(top_logit, axis=-1)
    e_flat = top_idx.reshape(n * TOP_K).astype(jnp.int32)
    onehot = (e_flat[:, None] == jnp.arange(N_EXPERTS, dtype=jnp.int32)[None, :]).astype(jnp.int32)
    csum = jnp.cumsum(onehot, axis=0)
    counts = csum[-1]
    padded = (counts + TM_MOE - 1) // TM_MOE * TM_MOE
    pend = jnp.cumsum(padded)
    pstart = pend - padded
    dest = jnp.sum(onehot * (csum - 1 + pstart[None, :]), axis=1).reshape(n, TOP_K)
    n_rows = n * TOP_K + N_EXPERTS * TM_MOE
    nblk = n_rows // TM_MOE
    n_used = pend[-1] // TM_MOE
    blk_start = jnp.minimum(jnp.arange(nblk, dtype=jnp.int32) * TM_MOE, (n_used - 1) * TM_MOE)
    blk_expert = jnp.minimum(
        jnp.sum((blk_start[:, None] >= pend[None, :]).astype(jnp.int32), axis=1), N_EXPERTS - 1)
    blk_valid = jnp.clip(counts[blk_expert] - (blk_start - pstart[blk_expert]), 0, TM_MOE)
    i32 = lambda a: a.astype(jnp.int32)
    return (i32(dest), gates, i32(pend), i32(blk_expert), i32(blk_valid), i32(n_used).reshape(1),
            n_rows)


def _rope_tables(seq_len):
    half = HEAD_DIM // 2
    inv_freq = jnp.power(ROPE_THETA, -(2.0 / HEAD_DIM) * jnp.arange(half, dtype=F32))
    ang = jnp.arange(seq_len, dtype=F32)[:, None] * inv_freq[None, :]
    cos, sin = jnp.cos(ang), jnp.sin(ang)
    cos_t = jnp.tile(cos, (1, LANES // half))
    sin_t = jnp.tile(jnp.concatenate([-sin, sin], axis=1), (1, LANES // HEAD_DIM))
    return cos_t, sin_t


def kernel(x, mem, g_attn, w_in, g_qk_na, rpb_na, g_qk_dil, g_mem, w_mem_kv, g_qk_mem, g_out, w_out,
           g_ffn, w_gate_dense, w_up_dense, w_down_dense, w_router, w_gate_moe, w_up_moe, w_down_moe):
    batch, seq_len, _ = x.shape
    mem_len = mem.shape[1]
    depth = g_attn.shape[0]
    n = batch * seq_len
    scale = HEAD_DIM ** -0.5

    cos_t, sin_t = _rope_tables(seq_len)
    seg = jnp.asarray(np.kron(np.eye(LANES // HEAD_DIM), np.ones((HEAD_DIM, HEAD_DIM))), BF16)
    band_masks = _band_masks()
    ones = lambda w: jnp.ones((w,), F32)

    h = x.reshape(n, D_MODEL)
    mem2 = mem.reshape(batch * mem_len, D_MODEL)
    for layer in range(depth):
        gq_row = jnp.concatenate([
            jnp.tile(g_qk_na[layer, 0] * scale, H_NA), jnp.tile(g_qk_na[layer, 1], H_NA), ones(W_NA),
            jnp.tile(g_qk_dil[layer, 0] * scale, H_DIL), jnp.tile(g_qk_dil[layer, 1], H_DIL), ones(W_DIL),
            jnp.tile(g_qk_mem[layer, 0] * scale, H_MEM)]).reshape(1, IN_WIDTH)
        proj = _inproj(h, g_attn[layer].reshape(1, D_MODEL), w_in[layer].astype(BF16), gq_row, cos_t, sin_t,
                       seg, batch, seq_len)
        qa, ka, va, qb, kb, vb, qm = proj[:_N_FLAT_OUT]

        o_na = _na_attention(qa, ka, va, _na_bias_tables(rpb_na[layer], seq_len // GRID_W),
                             batch, seq_len)

        dil = []
        for d in DILATIONS:
            if d == 1:
                qkv = [a.reshape(batch, seq_len, W_DIL) for a in (qb, kb, vb)]
            else:
                base = _N_FLAT_OUT + 3 * _RESIDUE_DILS.index(d)
                qkv = [a.reshape(batch * d, seq_len // d, W_DIL) for a in proj[base:base + 3]]
            dil += _band_attention(*qkv, band_masks)

        km, vm = _memkv(mem2, g_mem[layer].reshape(1, D_MODEL), w_mem_kv[layer].astype(BF16),
                        jnp.tile(g_qk_mem[layer, 1], LANES // HEAD_DIM).reshape(1, LANES), seg)
        o_mem = _mem_attention(qm, km, vm, batch, seq_len, mem_len)

        i = layer // 2
        moe = layer % 2 == 1
        w_router_pad = None
        if moe:
            w_router_pad = jnp.pad(w_router[i], ((0, 0), (0, LANES - N_EXPERTS)))
        outs = _outproj(o_na, dil, o_mem, h, g_out[layer].reshape(1, D_MODEL), w_out[layer].astype(BF16),
                        g_ffn[layer].reshape(1, D_MODEL), w_router_pad, batch, seq_len)
        if not moe:
            h, u = outs
            h = _dense_ffn(u, h, w_gate_dense[i], w_up_dense[i], w_down_dense[i])
        else:
            h, u, logits = outs
            dest, gates, pend, blk_expert, blk_valid, n_used, n_rows = _route(logits[:, :N_EXPERTS])
            x_sorted = _dispatch(u, dest, pend, n_used, n_rows)
            y = _moe_ffn(x_sorted, blk_expert, blk_valid, n_used, w_gate_moe[i], w_up_moe[i], w_down_moe[i])
            h = _combine(h, y, dest, gates)
    return h.reshape(batch, seq_len, D_MODEL)
```

```python
import functools

import numpy as np
import jax
import jax.numpy as jnp
from jax import lax
from jax.experimental import pallas as pl
from jax.experimental.pallas import tpu as pltpu

F32 = jnp.float32
BF16 = jnp.bfloat16

D_MODEL = 1024
HEAD_DIM = 64
H_NA, H_DIL, H_MEM = 6, 6, 4
W_NA, W_DIL, W_MEM = H_NA * HEAD_DIM, H_DIL * HEAD_DIM, H_MEM * HEAD_DIM
IN_WIDTH = 3 * W_NA + 3 * W_DIL + W_MEM
GRID_W = 64
NA_KH, NA_KW = 8, 16
DIL_CFG = ((128, 1), (512, 4), (2048, 16))
ROPE_THETA = 10000.0
N_EXPERTS = 8
TOP_K = 2
RMS_EPS = 1e-6
NEG_INF = -1e30

LANES = 128
TM_PROJ = 512
NA_QROWS = 4
NA_QB = NA_QROWS * GRID_W
NA_KROWS = 12
NA_KB = NA_KROWS * GRID_W
DIL_QB = 128
DIL_HALF = 64
DIL_KB = DIL_QB + 2 * DIL_HALF
DIL_STEP = 256
TQ_MEM = 512
TM_FFN = 1024
TF_DENSE = 256
TM_MOE = 1024
TF_MOE = 512
MOE_CHUNK = 256
TC_COMBINE = 512
ZERO_ROWS = 256
ISSUE_UNROLL = 8
VMEM_LIMIT = 56 * 1024 * 1024


def _cparams(sem):
    return pltpu.CompilerParams(dimension_semantics=sem, vmem_limit_bytes=VMEM_LIMIT)


def _rms(x, g):
    return x * lax.rsqrt(jnp.mean(x * x, axis=-1, keepdims=True) + RMS_EPS) * g


_CHUNKS = (
    [(0, c, True, False) for c in range(3)] + [(1, c, True, False) for c in range(3)]
    + [(2, c, False, False) for c in range(3)]
    + [(3, c, True, True) for c in range(3)] + [(4, c, True, True) for c in range(3)]
    + [(5, c, False, False) for c in range(3)]
    + [(6, c, True, False) for c in range(2)]
)


DILATIONS = tuple(d for _, d in DIL_CFG)
_RESIDUE_DILS = tuple(d for d in DILATIONS if d > 1)
_N_FLAT_OUT = 7


def _inproj_kernel(h_ref, ga_ref, w_ref, gq_ref, cos_ref, sin_ref, seg_ref, *rest):
    out_refs, stage_ref = rest[:-1], rest[-1]
    tm = h_ref.shape[0]
    u = _rms(h_ref[...], ga_ref[...]).astype(BF16)
    p = jnp.dot(u, w_ref[...], preferred_element_type=F32)
    seg = seg_ref[...]
    lane = lax.broadcasted_iota(jnp.int32, (1, LANES), 1)
    first_half = (lane % HEAD_DIM) < (HEAD_DIM // 2)
    for c, (oi, oc, norm, rope) in enumerate(_CHUNKS):
        x = p[:, c * LANES:(c + 1) * LANES]
        if norm:
            ss = jnp.dot((x * x).astype(BF16), seg, preferred_element_type=F32)
            x = x * lax.rsqrt(ss * (1.0 / HEAD_DIM) + RMS_EPS) * gq_ref[:, c * LANES:(c + 1) * LANES]
        if rope:
            swapped = jnp.where(first_half, pltpu.roll(x, LANES - HEAD_DIM // 2, 1),
                                pltpu.roll(x, HEAD_DIM // 2, 1))
            x = x * cos_ref[...] + swapped * sin_ref[...]
        out_refs[oi][:, oc * LANES:(oc + 1) * LANES] = x.astype(BF16)
        if 3 <= oi <= 5:
            st = (oi - 3) * (W_DIL // LANES) + oc
            stage_ref[st] = x
            for di, d in enumerate(_RESIDUE_DILS):
                dst = out_refs[_N_FLAT_OUT + 3 * di + (oi - 3)]
                for r in range(d):
                    dst[r, :, oc * LANES:(oc + 1) * LANES] = stage_ref[
                        st, pl.ds(r, tm // d, stride=d), :].astype(BF16)


def _inproj(h, g_attn, w_in_bf16, gq_row, cos_t, sin_t, seg, batch, seq_len):
    n = h.shape[0]
    tm = TM_PROJ
    tblocks = seq_len // tm
    widths = (W_NA, W_NA, W_NA, W_DIL, W_DIL, W_DIL, W_MEM)
    out_specs = [pl.BlockSpec((tm, w), lambda i: (i, 0)) for w in widths]
    out_shape = [jax.ShapeDtypeStruct((n, w), BF16) for w in widths]
    for d in _RESIDUE_DILS:
        out_specs += [pl.BlockSpec((None, d, tm // d, W_DIL),
                                   lambda i: (i // tblocks, 0, i % tblocks, 0))] * 3
        out_shape += [jax.ShapeDtypeStruct((batch, d, seq_len // d, W_DIL), BF16)] * 3
    return pl.pallas_call(
        _inproj_kernel,
        grid=(n // tm,),
        in_specs=[
            pl.BlockSpec((tm, D_MODEL), lambda i: (i, 0)),
            pl.BlockSpec((1, D_MODEL), lambda i: (0, 0)),
            pl.BlockSpec((D_MODEL, IN_WIDTH), lambda i: (0, 0)),
            pl.BlockSpec((1, IN_WIDTH), lambda i: (0, 0)),
            pl.BlockSpec((tm, LANES), lambda i: (i % tblocks, 0)),
            pl.BlockSpec((tm, LANES), lambda i: (i % tblocks, 0)),
            pl.BlockSpec((LANES, LANES), lambda i: (0, 0)),
        ],
        out_specs=out_specs,
        out_shape=out_shape,
        scratch_shapes=[pltpu.VMEM((3 * W_DIL // LANES, tm, LANES), F32)],
        compiler_params=_cparams(("parallel",)),
        name="inproj",
    )(h, g_attn, w_in_bf16, gq_row, cos_t, sin_t, seg)


def _pair_attention(q, k, v, bias, want_lse):
    nq = q.shape[0]
    first = lax.broadcasted_iota(jnp.int32, (1, LANES), 1) < HEAD_DIM
    zero = jnp.zeros_like(q)
    q2 = jnp.concatenate([jnp.where(first, q, zero), jnp.where(first, zero, q)], axis=0)
    v_ones = jnp.concatenate([v, jnp.ones_like(v)], axis=1)
    o, lse = _softmax_pv(q2, k, v_ones, bias, want_lse)
    out = jnp.where(first, o[:nq], o[nq:])
    lse_out = jnp.where(first, lse[:nq], lse[nq:]) if want_lse else None
    return out, lse_out


def _softmax_pv(q, k, v_ones, bias, want_lse):
    s = lax.dot_general(q, k, (((1,), (1,)), ((), ())), preferred_element_type=F32)
    if bias is not None:
        s = s + bias
    m = jnp.max(s, axis=-1, keepdims=True)
    p = jnp.exp(s - m).astype(BF16)
    ol = jnp.dot(p, v_ones, preferred_element_type=F32)
    l = ol[:, LANES:]
    return ol[:, :LANES] / l, (m + jnp.log(l)) if want_lse else None


def _na_kernel(q_ref, k_ref, v_ref, bias_ref, o_ref, *, rows):
    i = pl.program_id(1)
    srow = jnp.clip(i * NA_QROWS - NA_KH // 2, 0, rows - NA_KROWS)
    start = pl.multiple_of(srow * GRID_W, GRID_W)
    for p in range(W_NA // LANES):
        sl = slice(p * LANES, (p + 1) * LANES)
        k = k_ref[pl.ds(start, NA_KB), sl]
        v = v_ref[pl.ds(start, NA_KB), sl]
        q = q_ref[:, sl]
        first = lax.broadcasted_iota(jnp.int32, (1, LANES), 1) < HEAD_DIM
        zero = jnp.zeros_like(q)
        v_ones = jnp.concatenate([v, jnp.ones_like(v)], axis=1)
        o0, _ = _softmax_pv(jnp.where(first, q, zero), k, v_ones, bias_ref[0, 2 * p], False)
        o1, _ = _softmax_pv(jnp.where(first, zero, q), k, v_ones, bias_ref[0, 2 * p + 1], False)
        o_ref[:, sl] = jnp.where(first, o0, o1)


def _na_attention(q, k, v, bias, batch, seq_len):
    rows = seq_len // GRID_W
    nblk = seq_len // NA_QB
    q3, k3, v3 = (a.reshape(batch, seq_len, W_NA) for a in (q, k, v))

    def variant(i):
        return (i > 0).astype(jnp.int32) + (i == nblk - 1).astype(jnp.int32)

    out = pl.pallas_call(
        functools.partial(_na_kernel, rows=rows),
        grid=(batch, nblk),
        in_specs=[
            pl.BlockSpec((None, NA_QB, W_NA), lambda b, i: (b, i, 0)),
            pl.BlockSpec((None, seq_len, W_NA), lambda b, i: (b, 0, 0)),
            pl.BlockSpec((None, seq_len, W_NA), lambda b, i: (b, 0, 0)),
            pl.BlockSpec((1, H_NA, NA_QB, NA_KB), lambda b, i: (variant(i), 0, 0, 0)),
        ],
        out_specs=pl.BlockSpec((None, NA_QB, W_NA), lambda b, i: (b, i, 0)),
        out_shape=jax.ShapeDtypeStruct((batch, seq_len, W_NA), F32),
        compiler_params=_cparams(("parallel", "arbitrary")),
        name="na_attention",
    )(q3, k3, v3, bias)
    return out.reshape(batch * seq_len, W_NA)


def _na_bias_tables(rpb, rows):
    nblk = rows // NA_QROWS
    hi = lax.Precision.HIGHEST
    qc = np.arange(GRID_W)[:, None]
    kc = np.arange(GRID_W)[None, :]
    cs = np.clip(qc - NA_KW // 2, 0, GRID_W - NA_KW)
    col_valid = (kc >= cs) & (kc < cs + NA_KW)
    col_rel = kc - qc + (NA_KW - 1)
    col_sel = (col_rel[None] == np.arange(2 * NA_KW - 1)[:, None, None]).astype(np.float32)
    t1 = jnp.einsum('hab,bqk->haqk', rpb, jnp.asarray(col_sel), precision=hi)
    row_sel, row_valid = [], []
    for blk in (0, 1, nblk - 1):
        r0 = blk * NA_QROWS
        srow = int(np.clip(r0 - NA_KH // 2, 0, rows - NA_KROWS))
        qr = (r0 + np.arange(NA_QROWS))[:, None]
        kr = (srow + np.arange(NA_KROWS))[None, :]
        rs = np.clip(qr - NA_KH // 2, 0, rows - NA_KH)
        row_valid.append((kr >= rs) & (kr < rs + NA_KH))
        row_rel = kr - qr + (NA_KH - 1)
        row_sel.append((row_rel[..., None] == np.arange(2 * NA_KH - 1)).astype(np.float32))
    vals = jnp.einsum('vrsa,haqk->vhrqsk', jnp.asarray(np.stack(row_sel)), t1, precision=hi)
    valid = (np.stack(row_valid)[:, :, None, :, None] & col_valid[None, None, :, None, :])
    out = jnp.where(jnp.asarray(valid)[:, None], vals, NEG_INF)
    return out.reshape(3, H_NA, NA_QB, NA_KB).astype(F32)


def _band_kernel(q_ref, k_ref, v_ref, mask_ref, o_ref, lse_ref, *, length):
    j = pl.program_id(1)
    for sb in range(DIL_STEP // DIL_QB):
        rows = slice(sb * DIL_QB, (sb + 1) * DIL_QB)
        q0 = j * DIL_STEP + sb * DIL_QB
        start = pl.multiple_of(jnp.clip(q0 - DIL_HALF, 0, length - DIL_KB), DIL_HALF)
        variant = (q0 > 0).astype(jnp.int32) + (q0 == length - DIL_QB).astype(jnp.int32)
        mask = mask_ref[variant]
        mask2 = jnp.concatenate([mask, mask], axis=0)
        for p in range(W_DIL // LANES):
            sl = slice(p * LANES, (p + 1) * LANES)
            k = k_ref[pl.ds(start, DIL_KB), sl]
            v = v_ref[pl.ds(start, DIL_KB), sl]
            o, lse = _pair_attention(q_ref[rows, sl], k, v, mask2, True)
            o_ref[rows, sl] = o
            lse_ref[rows, sl] = lse


def _band_attention(q, k, v, masks):
    nseq, length, _ = q.shape
    nblk = length // DIL_STEP
    blk = pl.BlockSpec((None, DIL_STEP, W_DIL), lambda s, j: (s, j, 0))
    seq = pl.BlockSpec((None, length, W_DIL), lambda s, j: (s, 0, 0))
    return pl.pallas_call(
        functools.partial(_band_kernel, length=length),
        grid=(nseq, nblk),
        in_specs=[blk, seq, seq, pl.BlockSpec((3, DIL_QB, DIL_KB), lambda s, j: (0, 0, 0))],
        out_specs=[blk, blk],
        out_shape=[jax.ShapeDtypeStruct((nseq, length, W_DIL), F32)] * 2,
        compiler_params=_cparams(("parallel", "arbitrary")),
        name="band_attention",
    )(q, k, v, masks)


def _band_masks():
    qq = np.arange(DIL_QB)[:, None]
    kk = np.arange(DIL_KB)[None, :]
    out = []
    for off in (0, -DIL_HALF, DIL_QB - DIL_KB):
        rel = kk + off - qq
        out.append(np.where(np.abs(rel) <= DIL_HALF, 0.0, NEG_INF))
    return jnp.asarray(np.stack(out), F32)


def _memkv_kernel(mem_ref, g_ref, w_ref, gk_ref, seg_ref, km_ref, vm_ref):
    mn = _rms(mem_ref[...], g_ref[...]).astype(BF16)
    kv = jnp.dot(mn, w_ref[...], preferred_element_type=F32)
    for c in range(W_MEM // LANES):
        x = kv[:, c * LANES:(c + 1) * LANES]
        ss = jnp.dot((x * x).astype(BF16), seg_ref[...], preferred_element_type=F32)
        x = x * lax.rsqrt(ss * (1.0 / HEAD_DIM) + RMS_EPS) * gk_ref[...]
        km_ref[:, c * LANES:(c + 1) * LANES] = x.astype(BF16)
    vm_ref[...] = kv[:, W_MEM:].astype(BF16)


def _memkv(mem2, g_mem, w_kv_bf16, gk_row, seg):
    rows = mem2.shape[0]
    full = lambda shape: pl.BlockSpec(shape, lambda i: (0,) * len(shape))
    return pl.pallas_call(
        _memkv_kernel,
        grid=(1,),
        in_specs=[full((rows, D_MODEL)), full((1, D_MODEL)), full((D_MODEL, 2 * W_MEM)),
                  full((1, LANES)), full((LANES, LANES))],
        out_specs=[full((rows, W_MEM)), full((rows, W_MEM))],
        out_shape=[jax.ShapeDtypeStruct((rows, W_MEM), BF16)] * 2,
        compiler_params=_cparams(("arbitrary",)),
        name="mem_kv",
    )(mem2, g_mem, w_kv_bf16, gk_row, seg)


def _memattn_kernel(q_ref, k_ref, v_ref, o_ref):
    for c in range(W_MEM // LANES):
        sl = slice(c * LANES, (c + 1) * LANES)
        o, _ = _pair_attention(q_ref[:, sl], k_ref[:, sl], v_ref[:, sl], None, False)
        o_ref[:, sl] = o


def _mem_attention(qm, km, vm, batch, seq_len, mem_len):
    tq = TQ_MEM
    nq = seq_len // tq
    return pl.pallas_call(
        _memattn_kernel,
        grid=(batch, nq),
        in_specs=[
            pl.BlockSpec((tq, W_MEM), lambda b, i: (b * nq + i, 0)),
            pl.BlockSpec((mem_len, W_MEM), lambda b, i: (b, 0)),
            pl.BlockSpec((mem_len, W_MEM), lambda b, i: (b, 0)),
        ],
        out_specs=pl.BlockSpec((tq, W_MEM), lambda b, i: (b * nq + i, 0)),
        out_shape=jax.ShapeDtypeStruct((batch * seq_len, W_MEM), F32),
        compiler_params=_cparams(("parallel", "arbitrary")),
        name="mem_attention",
    )(qm, km, vm)


def _outproj_kernel(ona_ref, *rest, with_router):
    nd = 2 * len(DILATIONS)
    dil_refs, rest = rest[:nd], rest[nd:]
    om_ref, h_ref, go_ref, w_ref, gf_ref = rest[:5]
    rest = rest[5:]
    if with_router:
        wr_ref, hn_ref, u_ref, lg_ref = rest[:4]
        stage_refs = rest[4:]
    else:
        hn_ref, u_ref = rest[:2]
        stage_refs = rest[2:]
    tm = h_ref.shape[0]
    vals = []
    si = 0
    for di, d in enumerate(DILATIONS):
        for ref in dil_refs[2 * di:2 * di + 2]:
            if d == 1:
                vals.append(ref[...])
            else:
                st = stage_refs[si]
                si += 1
                for r in range(d):
                    for c in range(W_DIL // LANES):
                        st[c, pl.ds(r, tm // d, stride=d), :] = ref[r, :, c * LANES:(c + 1) * LANES]
                vals.append(jnp.concatenate([st[c] for c in range(W_DIL // LANES)], axis=-1))
    os_, ls_ = vals[0::2], vals[1::2]
    mx = functools.reduce(jnp.maximum, ls_)
    es = [jnp.exp(l - mx) for l in ls_]
    odil = sum(e * o for e, o in zip(es, os_)) / sum(es)
    go = go_ref[...]
    mixed = jnp.concatenate([
        _rms(ona_ref[...], go[:, :W_NA]),
        _rms(odil, go[:, W_NA:W_NA + W_DIL]),
        _rms(om_ref[...], go[:, W_NA + W_DIL:]),
    ], axis=-1).astype(BF16)
    hn = h_ref[...] + jnp.dot(mixed, w_ref[...], preferred_element_type=F32)
    hn_ref[...] = hn
    u = _rms(hn, gf_ref[...])
    if with_router:
        u_ref[...] = u
        u_hi = u.astype(BF16)
        u_lo = (u - u_hi.astype(F32)).astype(BF16)
        lg_ref[...] = (jnp.dot(u_hi, wr_ref[...], preferred_element_type=F32)
                       + jnp.dot(u_lo, wr_ref[...], preferred_element_type=F32))
    else:
        u_ref[...] = u.astype(BF16)


def _outproj(ona, dil, om, h, g_out, w_out_bf16, g_ffn, w_router_pad, batch, seq_len):
    n = h.shape[0]
    tm = TM_PROJ
    tblocks = seq_len // tm
    with_router = w_router_pad is not None
    row = lambda w: pl.BlockSpec((tm, w), lambda i: (i, 0))
    full = lambda shape: pl.BlockSpec(shape, lambda i: (0,) * len(shape))
    in_specs = [row(W_NA)]
    args = [ona]
    scratch = []
    for di, d in enumerate(DILATIONS):
        for a in dil[2 * di:2 * di + 2]:
            if d == 1:
                in_specs.append(row(W_DIL))
                args.append(a.reshape(n, W_DIL))
            else:
                in_specs.append(pl.BlockSpec((None, d, tm // d, W_DIL),
                                             lambda i: (i // tblocks, 0, i % tblocks, 0)))
                args.append(a.reshape(batch, d, seq_len // d, W_DIL))
                scratch.append(pltpu.VMEM((W_DIL // LANES, tm, LANES), F32))
    in_specs += [row(W_MEM), row(D_MODEL), full((1, D_MODEL)), full((D_MODEL, D_MODEL)),
                 full((1, D_MODEL))]
    args += [om, h, g_out, w_out_bf16, g_ffn]
    out_specs = [row(D_MODEL), row(D_MODEL)]
    out_shape = [jax.ShapeDtypeStruct((n, D_MODEL), F32),
                 jax.ShapeDtypeStruct((n, D_MODEL), F32 if with_router else BF16)]
    if with_router:
        in_specs.append(full((D_MODEL, LANES)))
        args.append(w_router_pad)
        out_specs.append(row(LANES))
        out_shape.append(jax.ShapeDtypeStruct((n, LANES), F32))
    return pl.pallas_call(
        functools.partial(_outproj_kernel, with_router=with_router),
        grid=(n // tm,),
        in_specs=in_specs,
        out_specs=out_specs,
        out_shape=out_shape,
        scratch_shapes=scratch,
        compiler_params=_cparams(("parallel",)),
        name="outproj",
    )(*args)


def _swiglu(x, wg, wu, wd):
    g = jnp.dot(x, wg, preferred_element_type=F32)
    u = jnp.dot(x, wu, preferred_element_type=F32)
    hm = (g * jax.nn.sigmoid(g) * u).astype(BF16)
    return jnp.dot(hm, wd, preferred_element_type=F32)


def _dense_ffn_kernel(u_ref, h_ref, wg_ref, wu_ref, wd_ref, o_ref, acc_ref):
    j = pl.program_id(1)

    @pl.when(j == 0)
    def _():
        acc_ref[...] = h_ref[...]

    wg, wu, wd = (r[...].astype(BF16) for r in (wg_ref, wu_ref, wd_ref))
    half = u_ref.shape[0] // 2
    for r0 in (0, half):
        rows = slice(r0, r0 + half)
        acc_ref[rows, :] += _swiglu(u_ref[rows, :], wg, wu, wd)

    @pl.when(j == pl.num_programs(1) - 1)
    def _():
        o_ref[...] = acc_ref[...]


def _dense_ffn(u, h, wg, wu, wd):
    n = h.shape[0]
    d_ff = wg.shape[1]
    tm, tf = TM_FFN, TF_DENSE
    return pl.pallas_call(
        _dense_ffn_kernel,
        grid=(n // tm, d_ff // tf),
        in_specs=[
            pl.BlockSpec((tm, D_MODEL), lambda i, j: (i, 0)),
            pl.BlockSpec((tm, D_MODEL), lambda i, j: (i, 0)),
            pl.BlockSpec((D_MODEL, tf), lambda i, j: (0, j)),
            pl.BlockSpec((D_MODEL, tf), lambda i, j: (0, j)),
            pl.BlockSpec((tf, D_MODEL), lambda i, j: (j, 0)),
        ],
        out_specs=pl.BlockSpec((tm, D_MODEL), lambda i, j: (i, 0)),
        out_shape=jax.ShapeDtypeStruct((n, D_MODEL), F32),
        scratch_shapes=[pltpu.VMEM((tm, D_MODEL), F32)],
        compiler_params=_cparams(("parallel", "arbitrary")),
        name="dense_ffn",
    )(u, h, wg, wu, wd)


def _dispatch_kernel(pend_ref, nu_ref, d0_ref, d1_ref, u_ref, x_hbm, zbuf, sem, zsem):
    i = pl.program_id(0)
    tc = u_ref.shape[0]
    zr = zbuf.shape[0]
    n_rows = x_hbm.shape[0]

    def zero_block(first_row):
        copies = [pltpu.make_async_copy(
            zbuf, x_hbm.at[pl.ds(pl.multiple_of(first_row + c * zr, zr), zr), :], zsem)
            for c in range(TM_MOE // zr)]
        for cp in copies:
            cp.start()
        for cp in copies:
            cp.wait()

    @pl.when(i == 0)
    def _():
        zbuf[...] = jnp.zeros_like(zbuf)
        prev = jnp.int32(0)
        for e in range(N_EXPERTS):
            end = pend_ref[e]

            @pl.when(end > prev)
            def _():
                zero_block(end - TM_MOE)

            prev = end

        def idle(blk, carry):
            zero_block(blk * TM_MOE)
            return carry

        lax.fori_loop(nu_ref[0], n_rows // TM_MOE, idle, 0)

    def issue(r, carry):
        src = u_ref.at[pl.ds(r, 1), :]
        pltpu.make_async_copy(src, x_hbm.at[pl.ds(d0_ref[0, 0, r], 1), :], sem).start(priority=0)
        pltpu.make_async_copy(src, x_hbm.at[pl.ds(d1_ref[0, 0, r], 1), :], sem).start(priority=1)
        return carry

    lax.fori_loop(0, tc, issue, 0, unroll=ISSUE_UNROLL)
    for _ in range(TOP_K):
        pltpu.make_async_copy(u_ref, x_hbm.at[pl.ds(0, tc), :], sem).wait()


def _dispatch(u, dest, pend, n_used, n_rows):
    n = u.shape[0]
    tc = TC_COMBINE
    nblk = n // tc
    d0 = dest[:, 0].reshape(nblk, 1, tc)
    d1 = dest[:, 1].reshape(nblk, 1, tc)
    smem = pl.BlockSpec((1, 1, tc), lambda i, pe, nu: (i, 0, 0), memory_space=pltpu.SMEM)
    grid_spec = pltpu.PrefetchScalarGridSpec(
        num_scalar_prefetch=2,
        grid=(nblk,),
        in_specs=[smem, smem, pl.BlockSpec((tc, D_MODEL), lambda i, pe, nu: (i, 0))],
        out_specs=pl.BlockSpec(memory_space=pl.ANY),
        scratch_shapes=[pltpu.VMEM((ZERO_ROWS, D_MODEL), F32), pltpu.SemaphoreType.DMA(()),
                        pltpu.SemaphoreType.DMA(())],
    )
    return pl.pallas_call(
        _dispatch_kernel,
        grid_spec=grid_spec,
        out_shape=jax.ShapeDtypeStruct((n_rows, D_MODEL), F32),
        compiler_params=_cparams(("arbitrary",)),
        name="moe_dispatch",
    )(pend, n_used, d0, d1, u)


def _moe_ffn_kernel(be_ref, nv_ref, nu_ref, x_ref, wg_ref, wu_ref, wd_ref, y_ref, xb):
    i = pl.program_id(0)
    j = pl.program_id(1)
    active = i < nu_ref[0]
    tm = x_ref.shape[0]
    nchunk = tm // MOE_CHUNK

    def chunk(c, wg, wu, wd):
        rows = slice(c * MOE_CHUNK, (c + 1) * MOE_CHUNK)
        y_ref[rows, :] += _swiglu(xb[rows, :], wg, wu, wd)

    @pl.when(j == 0)
    def _():
        y_ref[...] = jnp.zeros_like(y_ref)

    @pl.when(active)
    def _():
        @pl.when(j == 0)
        def _():
            xb[...] = x_ref[...].astype(BF16)

        nvalid = nv_ref[i]

        @pl.when(nvalid == tm)
        def _():
            wg, wu, wd = (r[...].astype(BF16) for r in (wg_ref, wu_ref, wd_ref))
            for c in range(nchunk):
                chunk(c, wg, wu, wd)

        @pl.when(nvalid < tm)
        def _():
            wg, wu, wd = (r[...].astype(BF16) for r in (wg_ref, wu_ref, wd_ref))
            for c in range(nchunk):
                @pl.when(c * MOE_CHUNK < nvalid)
                def _():
                    chunk(c, wg, wu, wd)


def _moe_ffn(x_sorted, blk_expert, blk_valid, n_used, wg, wu, wd):
    n_rows = x_sorted.shape[0]
    tm, tf = TM_MOE, TF_MOE
    nblk = n_rows // tm
    d_ff = wg.shape[2]
    nj = d_ff // tf

    def jeff(i, j, nu):
        return jnp.where(i < nu[0], j, nj - 1)

    grid_spec = pltpu.PrefetchScalarGridSpec(
        num_scalar_prefetch=3,
        grid=(nblk, nj),
        in_specs=[
            pl.BlockSpec((tm, D_MODEL), lambda i, j, be, nv, nu: (jnp.minimum(i, nu[0] - 1), 0)),
            pl.BlockSpec((None, D_MODEL, tf), lambda i, j, be, nv, nu: (be[i], 0, jeff(i, j, nu))),
            pl.BlockSpec((None, D_MODEL, tf), lambda i, j, be, nv, nu: (be[i], 0, jeff(i, j, nu))),
            pl.BlockSpec((None, tf, D_MODEL), lambda i, j, be, nv, nu: (be[i], jeff(i, j, nu), 0)),
        ],
        out_specs=pl.BlockSpec((tm, D_MODEL), lambda i, j, be, nv, nu: (i, 0)),
        scratch_shapes=[pltpu.VMEM((tm, D_MODEL), BF16)],
    )
    return pl.pallas_call(
        _moe_ffn_kernel,
        grid_spec=grid_spec,
        out_shape=jax.ShapeDtypeStruct((n_rows, D_MODEL), F32),
        compiler_params=_cparams(("arbitrary", "arbitrary")),
        name="moe_ffn",
    )(blk_expert, blk_valid, n_used, x_sorted, wg, wu, wd)


def _combine_kernel(p0_ref, p1_ref, h_ref, g_ref, y_hbm, o_ref, buf0, buf1, sem):
    tc = buf0.shape[0]

    def issue(r, carry):
        pltpu.make_async_copy(y_hbm.at[pl.ds(p0_ref[0, 0, r], 1), :], buf0.at[pl.ds(r, 1), :],
                              sem.at[0]).start(priority=0)
        pltpu.make_async_copy(y_hbm.at[pl.ds(p1_ref[0, 0, r], 1), :], buf1.at[pl.ds(r, 1), :],
                              sem.at[1]).start(priority=1)
        return carry

    lax.fori_loop(0, tc, issue, 0, unroll=ISSUE_UNROLL)
    pltpu.make_async_copy(y_hbm.at[pl.ds(0, tc), :], buf0, sem.at[0]).wait()
    pltpu.make_async_copy(y_hbm.at[pl.ds(0, tc), :], buf1, sem.at[1]).wait()
    g = g_ref[...]
    o_ref[...] = h_ref[...] + g[:, 0:1] * buf0[...] + g[:, 1:2] * buf1[...]


def _combine(h, y, dest, gates):
    n = h.shape[0]
    tc = TC_COMBINE
    nblk = n // tc
    p0 = dest[:, 0].reshape(nblk, 1, tc)
    p1 = dest[:, 1].reshape(nblk, 1, tc)
    smem = pl.BlockSpec((1, 1, tc), lambda i: (i, 0, 0), memory_space=pltpu.SMEM)
    return pl.pallas_call(
        _combine_kernel,
        grid=(nblk,),
        in_specs=[smem, smem, pl.BlockSpec((tc, D_MODEL), lambda i: (i, 0)),
                  pl.BlockSpec((tc, TOP_K), lambda i: (i, 0)), pl.BlockSpec(memory_space=pl.ANY)],
        out_specs=pl.BlockSpec((tc, D_MODEL), lambda i: (i, 0)),
        out_shape=jax.ShapeDtypeStruct((n, D_MODEL), F32),
        scratch_shapes=[pltpu.VMEM((tc, D_MODEL), F32), pltpu.VMEM((tc, D_MODEL), F32),
                        pltpu.SemaphoreType.DMA((2,))],
        compiler_params=_cparams(("arbitrary",)),
        name="moe_combine",
    )(p0, p1, h, gates, y)


def _route(logits):
    n = logits.shape[0]
    top_logit, top_idx = lax.top_k(logits, TOP_K)
    gates = jax.nn.softmax(top_logit, axis=-1)
    e_flat = top_idx.reshape(n * TOP_K).astype(jnp.int32)
    onehot = (e_flat[:, None] == jnp.arange(N_EXPERTS, dtype=jnp.int32)[None, :]).astype(jnp.int32)
    csum = jnp.cumsum(onehot, axis=0)
    counts = csum[-1]
    padded = (counts + TM_MOE - 1) // TM_MOE * TM_MOE
    pend = jnp.cumsum(padded)
    pstart = pend - padded
    dest = jnp.sum(onehot * (csum - 1 + pstart[None, :]), axis=1).reshape(n, TOP_K)
    n_rows = n * TOP_K + N_EXPERTS * TM_MOE
    nblk = n_rows // TM_MOE
    n_used = pend[-1] // TM_MOE
    blk_start = jnp.minimum(jnp.arange(nblk, dtype=jnp.int32) * TM_MOE, (n_used - 1) * TM_MOE)
    blk_expert = jnp.minimum(
        jnp.sum((blk_start[:, None] >= pend[None, :]).astype(jnp.int32), axis=1), N_EXPERTS - 1)
    blk_valid = jnp.clip(counts[blk_expert] - (blk_start - pstart[blk_expert]), 0, TM_MOE)
    i32 = lambda a: a.astype(jnp.int32)
    return (i32(dest), gates, i32(pend), i32(blk_expert), i32(blk_valid), i32(n_used).reshape(1),
            n_rows)


def _rope_tables(seq_len):
    half = HEAD_DIM // 2
    inv_freq = jnp.power(ROPE_THETA, -(2.0 / HEAD_DIM) * jnp.arange(half, dtype=F32))
    ang = jnp.arange(seq_len, dtype=F32)[:, None] * inv_freq[None, :]
    cos, sin = jnp.cos(ang), jnp.sin(ang)
    cos_t = jnp.tile(cos, (1, LANES // half))
    sin_t = jnp.tile(jnp.concatenate([-sin, sin], axis=1), (1, LANES // HEAD_DIM))
    return cos_t, sin_t


def kernel(x, mem, g_attn, w_in, g_qk_na, rpb_na, g_qk_dil, g_mem, w_mem_kv, g_qk_mem, g_out, w_out,
           g_ffn, w_gate_dense, w_up_dense, w_down_dense, w_router, w_gate_moe, w_up_moe, w_down_moe):
    batch, seq_len, _ = x.shape
    mem_len = mem.shape[1]
    depth = g_attn.shape[0]
    n = batch * seq_len
    scale = HEAD_DIM ** -0.5

    cos_t, sin_t = _rope_tables(seq_len)
    seg = jnp.asarray(np.kron(np.eye(LANES // HEAD_DIM), np.ones((HEAD_DIM, HEAD_DIM))), BF16)
    band_masks = _band_masks()
    ones = lambda w: jnp.ones((w,), F32)

    h = x.reshape(n, D_MODEL)
    mem2 = mem.reshape(batch * mem_len, D_MODEL)
    for layer in range(depth):
        gq_row = jnp.concatenate([
            jnp.tile(g_qk_na[layer, 0] * scale, H_NA), jnp.tile(g_qk_na[layer, 1], H_NA), ones(W_NA),
            jnp.tile(g_qk_dil[layer, 0] * scale, H_DIL), jnp.tile(g_qk_dil[layer, 1], H_DIL), ones(W_DIL),
            jnp.tile(g_qk_mem[layer, 0] * scale, H_MEM)]).reshape(1, IN_WIDTH)
        proj = _inproj(h, g_attn[layer].reshape(1, D_MODEL), w_in[layer].astype(BF16), gq_row, cos_t, sin_t,
                       seg, batch, seq_len)
        qa, ka, va, qb, kb, vb, qm = proj[:_N_FLAT_OUT]

        o_na = _na_attention(qa, ka, va, _na_bias_tables(rpb_na[layer], seq_len // GRID_W),
                             batch, seq_len)

        dil = []
        for d in DILATIONS:
            if d == 1:
                qkv = [a.reshape(batch, seq_len, W_DIL) for a in (qb, kb, vb)]
            else:
                base = _N_FLAT_OUT + 3 * _RESIDUE_DILS.index(d)
                qkv = [a.reshape(batch * d, seq_len // d, W_DIL) for a in proj[base:base + 3]]
            dil += _band_attention(*qkv, band_masks)

        km, vm = _memkv(mem2, g_mem[layer].reshape(1, D_MODEL), w_mem_kv[layer].astype(BF16),
                        jnp.tile(g_qk_mem[layer, 1], LANES // HEAD_DIM).reshape(1, LANES), seg)
        o_mem = _mem_attention(qm, km, vm, batch, seq_len, mem_len)

        i = layer // 2
        moe = layer % 2 == 1
        w_router_pad = None
        if moe:
            wr_hi = w_router[i].astype(BF16)
            wr_lo = (w_router[i] - wr_hi.astype(F32)).astype(BF16)
            w_router_pad = jnp.pad(jnp.concatenate([wr_hi, wr_lo], axis=1),
                                   ((0, 0), (0, LANES - 2 * N_EXPERTS)))
        outs = _outproj(o_na, dil, o_mem, h, g_out[layer].reshape(1, D_MODEL), w_out[layer].astype(BF16),
                        g_ffn[layer].reshape(1, D_MODEL), w_router_pad, batch, seq_len)
        if not moe:
            h, u = outs
            h = _dense_ffn(u, h, w_gate_dense[i], w_up_dense[i], w_down_dense[i])
        else:
            h, u, logits = outs
            dest, gates, pend, blk_expert, blk_valid, n_used, n_rows = _route(
                logits[:, :N_EXPERTS] + logits[:, N_EXPERTS:2 * N_EXPERTS])
            x_sorted = _dispatch(u, dest, pend, n_used, n_rows)
            y = _moe_ffn(x_sorted, blk_expert, blk_valid, n_used, w_gate_moe[i], w_up_moe[i], w_down_moe[i])
            h = _combine(h, y, dest, gates)
    return h.reshape(batch, seq_len, D_MODEL)
```

```python
import functools

import numpy as np
import jax
import jax.numpy as jnp
from jax import lax
from jax.experimental import pallas as pl
from jax.experimental.pallas import tpu as pltpu

F32 = jnp.float32
BF16 = jnp.bfloat16

D_MODEL = 1024
HEAD_DIM = 64
H_NA, H_DIL, H_MEM = 6, 6, 4
W_NA, W_DIL, W_MEM = H_NA * HEAD_DIM, H_DIL * HEAD_DIM, H_MEM * HEAD_DIM
IN_WIDTH = 3 * W_NA + 3 * W_DIL + W_MEM
GRID_W = 64
NA_KH, NA_KW = 8, 16
DIL_CFG = ((128, 1), (512, 4), (2048, 16))
ROPE_THETA = 10000.0
N_EXPERTS = 8
TOP_K = 2
RMS_EPS = 1e-6
NEG_INF = -1e30

LANES = 128
TM_PROJ = 512
NA_QROWS = 4
NA_QB = NA_QROWS * GRID_W
NA_KROWS = 12
NA_KB = NA_KROWS * GRID_W
DIL_QB = 128
DIL_HALF = 64
DIL_KB = DIL_QB + 2 * DIL_HALF
DIL_STEP = 512
TQ_MEM = 512
TM_FFN = 512
TF_DENSE = 256
TM_MOE = 1024
TF_MOE = 512
MOE_CHUNK = 256
TC_COMBINE = 512
ZERO_ROWS = 256
ISSUE_UNROLL = 8
VMEM_LIMIT = 56 * 1024 * 1024


def _cparams(sem):
    return pltpu.CompilerParams(dimension_semantics=sem, vmem_limit_bytes=VMEM_LIMIT)


def _rms(x, g):
    return x * lax.rsqrt(jnp.mean(x * x, axis=-1, keepdims=True) + RMS_EPS) * g


_CHUNKS = (
    [(0, c, True, False) for c in range(3)] + [(1, c, True, False) for c in range(3)]
    + [(2, c, False, False) for c in range(3)]
    + [(3, c, True, True) for c in range(3)] + [(4, c, True, True) for c in range(3)]
    + [(5, c, False, False) for c in range(3)]
    + [(6, c, True, False) for c in range(2)]
)


DILATIONS = tuple(d for _, d in DIL_CFG)
_RESIDUE_DILS = tuple(d for d in DILATIONS if d > 1)
assert len(_RESIDUE_DILS) == 2 and _RESIDUE_DILS[1] == _RESIDUE_DILS[0] ** 2
_N_FLAT_OUT = 7


def _inproj_kernel(h_ref, ga_ref, w_ref, gq_ref, cos_ref, sin_ref, seg_ref, *rest):
    out_refs, stage_ref, stage2_ref = rest[:-2], rest[-2], rest[-1]
    tm = h_ref.shape[0]
    u = _rms(h_ref[...], ga_ref[...]).astype(BF16)
    p = jnp.dot(u, w_ref[...], preferred_element_type=F32)
    seg = seg_ref[...]
    lane = lax.broadcasted_iota(jnp.int32, (1, LANES), 1)
    first_half = (lane % HEAD_DIM) < (HEAD_DIM // 2)
    for c, (oi, oc, norm, rope) in enumerate(_CHUNKS):
        x = p[:, c * LANES:(c + 1) * LANES]
        if norm:
            ss = jnp.dot((x * x).astype(BF16), seg, preferred_element_type=F32)
            x = x * lax.rsqrt(ss * (1.0 / HEAD_DIM) + RMS_EPS) * gq_ref[:, c * LANES:(c + 1) * LANES]
        if rope:
            swapped = jnp.where(first_half, pltpu.roll(x, LANES - HEAD_DIM // 2, 1),
                                pltpu.roll(x, HEAD_DIM // 2, 1))
            x = x * cos_ref[...] + swapped * sin_ref[...]
        out_refs[oi][:, oc * LANES:(oc + 1) * LANES] = x.astype(BF16)
        if 3 <= oi <= 5:
            st = (oi - 3) * (W_DIL // LANES) + oc
            cols = slice(oc * LANES, (oc + 1) * LANES)
            f = _RESIDUE_DILS[0]
            dst1 = out_refs[_N_FLAT_OUT + (oi - 3)]
            dst2 = out_refs[_N_FLAT_OUT + 3 + (oi - 3)]
            stage_ref[st] = x
            for r in range(f):
                xr = stage_ref[st, pl.ds(r, tm // f, stride=f), :]
                dst1[r, :, cols] = xr.astype(BF16)
                stage2_ref[st * f + r] = xr
                for r2 in range(f):
                    dst2[r + f * r2, :, cols] = stage2_ref[
                        st * f + r, pl.ds(r2, tm // (f * f), stride=f), :].astype(BF16)


def _inproj(h, g_attn, w_in_bf16, gq_row, cos_t, sin_t, seg, batch, seq_len):
    n = h.shape[0]
    tm = TM_PROJ
    tblocks = seq_len // tm
    widths = (W_NA, W_NA, W_NA, W_DIL, W_DIL, W_DIL, W_MEM)
    nst, f = 3 * W_DIL // LANES, _RESIDUE_DILS[0]
    out_specs = [pl.BlockSpec((tm, w), lambda i: (i, 0)) for w in widths]
    out_shape = [jax.ShapeDtypeStruct((n, w), BF16) for w in widths]
    for d in _RESIDUE_DILS:
        out_specs += [pl.BlockSpec((None, d, tm // d, W_DIL),
                                   lambda i: (i // tblocks, 0, i % tblocks, 0))] * 3
        out_shape += [jax.ShapeDtypeStruct((batch, d, seq_len // d, W_DIL), BF16)] * 3
    return pl.pallas_call(
        _inproj_kernel,
        grid=(n // tm,),
        in_specs=[
            pl.BlockSpec((tm, D_MODEL), lambda i: (i, 0)),
            pl.BlockSpec((1, D_MODEL), lambda i: (0, 0)),
            pl.BlockSpec((D_MODEL, IN_WIDTH), lambda i: (0, 0)),
            pl.BlockSpec((1, IN_WIDTH), lambda i: (0, 0)),
            pl.BlockSpec((tm, LANES), lambda i: (i % tblocks, 0)),
            pl.BlockSpec((tm, LANES), lambda i: (i % tblocks, 0)),
            pl.BlockSpec((LANES, LANES), lambda i: (0, 0)),
        ],
        out_specs=out_specs,
        out_shape=out_shape,
        scratch_shapes=[pltpu.VMEM((nst, tm, LANES), F32), pltpu.VMEM((nst * f, tm // f, LANES), F32)],
        compiler_params=_cparams(("parallel",)),
        name="inproj",
    )(h, g_attn, w_in_bf16, gq_row, cos_t, sin_t, seg)


def _pair_attention(q, k, v, bias, want_lse):
    nq = q.shape[0]
    first = lax.broadcasted_iota(jnp.int32, (1, LANES), 1) < HEAD_DIM
    zero = jnp.zeros_like(q)
    q2 = jnp.concatenate([jnp.where(first, q, zero), jnp.where(first, zero, q)], axis=0)
    v_ones = jnp.concatenate([v, jnp.ones_like(v)], axis=1)
    o, lse = _softmax_pv(q2, k, v_ones, bias, want_lse)
    out = jnp.where(first, o[:nq], o[nq:])
    lse_out = jnp.where(first, lse[:nq], lse[nq:]) if want_lse else None
    return out, lse_out


def _softmax_pv(q, k, v_ones, bias, want_lse):
    s = lax.dot_general(q, k, (((1,), (1,)), ((), ())), preferred_element_type=F32)
    if bias is not None:
        s = s + bias
    m = jnp.max(s, axis=-1, keepdims=True)
    p = jnp.exp(s - m).astype(BF16)
    ol = jnp.dot(p, v_ones, preferred_element_type=F32)
    l = ol[:, LANES:]
    return ol[:, :LANES] / l, (m + jnp.log(l)) if want_lse else None


def _na_kernel(q_ref, k_ref, v_ref, bias_ref, o_ref, *, rows):
    i = pl.program_id(1)
    srow = jnp.clip(i * NA_QROWS - NA_KH // 2, 0, rows - NA_KROWS)
    start = pl.multiple_of(srow * GRID_W, GRID_W)
    for p in range(W_NA // LANES):
        sl = slice(p * LANES, (p + 1) * LANES)
        k = k_ref[pl.ds(start, NA_KB), sl]
        v = v_ref[pl.ds(start, NA_KB), sl]
        q = q_ref[:, sl]
        first = lax.broadcasted_iota(jnp.int32, (1, LANES), 1) < HEAD_DIM
        zero = jnp.zeros_like(q)
        v_ones = jnp.concatenate([v, jnp.ones_like(v)], axis=1)
        o0, _ = _softmax_pv(jnp.where(first, q, zero), k, v_ones, bias_ref[0, 2 * p], False)
        o1, _ = _softmax_pv(jnp.where(first, zero, q), k, v_ones, bias_ref[0, 2 * p + 1], False)
        o_ref[:, sl] = jnp.where(first, o0, o1)


def _na_attention(q, k, v, bias, batch, seq_len):
    rows = seq_len // GRID_W
    nblk = seq_len // NA_QB
    q3, k3, v3 = (a.reshape(batch, seq_len, W_NA) for a in (q, k, v))

    def variant(i):
        return (i > 0).astype(jnp.int32) + (i == nblk - 1).astype(jnp.int32)

    out = pl.pallas_call(
        functools.partial(_na_kernel, rows=rows),
        grid=(batch, nblk),
        in_specs=[
            pl.BlockSpec((None, NA_QB, W_NA), lambda b, i: (b, i, 0)),
            pl.BlockSpec((None, seq_len, W_NA), lambda b, i: (b, 0, 0)),
            pl.BlockSpec((None, seq_len, W_NA), lambda b, i: (b, 0, 0)),
            pl.BlockSpec((1, H_NA, NA_QB, NA_KB), lambda b, i: (variant(i), 0, 0, 0)),
        ],
        out_specs=pl.BlockSpec((None, NA_QB, W_NA), lambda b, i: (b, i, 0)),
        out_shape=jax.ShapeDtypeStruct((batch, seq_len, W_NA), F32),
        compiler_params=_cparams(("parallel", "arbitrary")),
        name="na_attention",
    )(q3, k3, v3, bias)
    return out.reshape(batch * seq_len, W_NA)


def _na_bias_tables(rpb, rows):
    nblk = rows // NA_QROWS
    qc = np.arange(GRID_W)[:, None]
    kc = np.arange(GRID_W)[None, :]
    cs = np.clip(qc - NA_KW // 2, 0, GRID_W - NA_KW)
    col_valid = (kc >= cs) & (kc < cs + NA_KW)
    col_rel = kc - qc + (NA_KW - 1)
    col_sel = (col_rel[None] == np.arange(2 * NA_KW - 1)[:, None, None]).astype(np.float32)
    t1 = jnp.einsum('hab,bqk->haqk', rpb, jnp.asarray(col_sel), precision=lax.Precision.HIGHEST)
    t1 = jnp.where(jnp.asarray(col_valid)[None, None], t1, NEG_INF)
    masked = jnp.full((H_NA, GRID_W, GRID_W), NEG_INF, F32)
    tables = []
    for blk in (0, 1, nblk - 1):
        r0 = blk * NA_QROWS
        srow = int(np.clip(r0 - NA_KH // 2, 0, rows - NA_KROWS))
        slabs = []
        for qr in range(r0, r0 + NA_QROWS):
            rs = int(np.clip(qr - NA_KH // 2, 0, rows - NA_KH))
            slabs.append(jnp.concatenate(
                [t1[:, kr - qr + NA_KH - 1] if rs <= kr < rs + NA_KH else masked
                 for kr in range(srow, srow + NA_KROWS)], axis=-1))
        tables.append(jnp.stack(slabs, axis=1))
    return jnp.stack(tables).reshape(3, H_NA, NA_QB, NA_KB)


def _band_kernel(q_ref, k_ref, v_ref, mask_ref, o_ref, lse_ref, *, length):
    j = pl.program_id(1)
    for sb in range(DIL_STEP // DIL_QB):
        rows = slice(sb * DIL_QB, (sb + 1) * DIL_QB)
        q0 = j * DIL_STEP + sb * DIL_QB
        start = pl.multiple_of(jnp.clip(q0 - DIL_HALF, 0, length - DIL_KB), DIL_HALF)
        variant = (q0 > 0).astype(jnp.int32) + (q0 == length - DIL_QB).astype(jnp.int32)
        mask = mask_ref[variant]
        mask2 = jnp.concatenate([mask, mask], axis=0)
        for p in range(W_DIL // LANES):
            sl = slice(p * LANES, (p + 1) * LANES)
            k = k_ref[pl.ds(start, DIL_KB), sl]
            v = v_ref[pl.ds(start, DIL_KB), sl]
            o, lse = _pair_attention(q_ref[rows, sl], k, v, mask2, True)
            o_ref[rows, sl] = o
            lse_ref[rows, sl] = lse


def _band_attention(q, k, v, masks):
    nseq, length, _ = q.shape
    nblk = length // DIL_STEP
    blk = pl.BlockSpec((None, DIL_STEP, W_DIL), lambda s, j: (s, j, 0))
    seq = pl.BlockSpec((None, length, W_DIL), lambda s, j: (s, 0, 0))
    return pl.pallas_call(
        functools.partial(_band_kernel, length=length),
        grid=(nseq, nblk),
        in_specs=[blk, seq, seq, pl.BlockSpec((3, DIL_QB, DIL_KB), lambda s, j: (0, 0, 0))],
        out_specs=[blk, blk],
        out_shape=[jax.ShapeDtypeStruct((nseq, length, W_DIL), F32)] * 2,
        compiler_params=_cparams(("parallel", "arbitrary")),
        name="band_attention",
    )(q, k, v, masks)


def _band_masks():
    qq = np.arange(DIL_QB)[:, None]
    kk = np.arange(DIL_KB)[None, :]
    out = []
    for off in (0, -DIL_HALF, DIL_QB - DIL_KB):
        rel = kk + off - qq
        out.append(np.where(np.abs(rel) <= DIL_HALF, 0.0, NEG_INF))
    return jnp.asarray(np.stack(out), F32)


def _memkv_kernel(mem_ref, g_ref, w_ref, gk_ref, seg_ref, km_ref, vm_ref):
    mn = _rms(mem_ref[...], g_ref[...]).astype(BF16)
    kv = jnp.dot(mn, w_ref[...], preferred_element_type=F32)
    for c in range(W_MEM // LANES):
        x = kv[:, c * LANES:(c + 1) * LANES]
        ss = jnp.dot((x * x).astype(BF16), seg_ref[...], preferred_element_type=F32)
        x = x * lax.rsqrt(ss * (1.0 / HEAD_DIM) + RMS_EPS) * gk_ref[...]
        km_ref[:, c * LANES:(c + 1) * LANES] = x.astype(BF16)
    vm_ref[...] = kv[:, W_MEM:].astype(BF16)


def _memkv(mem2, g_mem, w_kv_bf16, gk_row, seg):
    rows = mem2.shape[0]
    full = lambda shape: pl.BlockSpec(shape, lambda i: (0,) * len(shape))
    return pl.pallas_call(
        _memkv_kernel,
        grid=(1,),
        in_specs=[full((rows, D_MODEL)), full((1, D_MODEL)), full((D_MODEL, 2 * W_MEM)),
                  full((1, LANES)), full((LANES, LANES))],
        out_specs=[full((rows, W_MEM)), full((rows, W_MEM))],
        out_shape=[jax.ShapeDtypeStruct((rows, W_MEM), BF16)] * 2,
        compiler_params=_cparams(("arbitrary",)),
        name="mem_kv",
    )(mem2, g_mem, w_kv_bf16, gk_row, seg)


def _memattn_kernel(q_ref, k_ref, v_ref, o_ref):
    for c in range(W_MEM // LANES):
        sl = slice(c * LANES, (c + 1) * LANES)
        o, _ = _pair_attention(q_ref[:, sl], k_ref[:, sl], v_ref[:, sl], None, False)
        o_ref[:, sl] = o


def _mem_attention(qm, km, vm, batch, seq_len, mem_len):
    tq = TQ_MEM
    nq = seq_len // tq
    return pl.pallas_call(
        _memattn_kernel,
        grid=(batch, nq),
        in_specs=[
            pl.BlockSpec((tq, W_MEM), lambda b, i: (b * nq + i, 0)),
            pl.BlockSpec((mem_len, W_MEM), lambda b, i: (b, 0)),
            pl.BlockSpec((mem_len, W_MEM), lambda b, i: (b, 0)),
        ],
        out_specs=pl.BlockSpec((tq, W_MEM), lambda b, i: (b * nq + i, 0)),
        out_shape=jax.ShapeDtypeStruct((batch * seq_len, W_MEM), F32),
        compiler_params=_cparams(("parallel", "arbitrary")),
        name="mem_attention",
    )(qm, km, vm)


def _outproj_kernel(ona_ref, *rest, with_router):
    nd = 2 * len(DILATIONS)
    dil_refs, rest = rest[:nd], rest[nd:]
    om_ref, h_ref, go_ref, w_ref, gf_ref = rest[:5]
    rest = rest[5:]
    if with_router:
        wr_ref, hn_ref, u_ref, lg_ref = rest[:4]
        stage_refs = rest[4:]
    else:
        hn_ref, u_ref = rest[:2]
        stage_refs = rest[2:]
    stage_refs, tmp_ref = stage_refs[:-1], stage_refs[-1]
    tm = h_ref.shape[0]
    f = _RESIDUE_DILS[0]
    nc = W_DIL // LANES
    vals = []
    si = 0
    for di, d in enumerate(DILATIONS):
        for ref in dil_refs[2 * di:2 * di + 2]:
            if d == 1:
                vals.append(ref[...])
                continue
            st = stage_refs[si]
            si += 1
            for c in range(nc):
                cols = slice(c * LANES, (c + 1) * LANES)
                for r in range(f):
                    if d == f:
                        st[c, pl.ds(r, tm // f, stride=f), :] = ref[r, :, cols]
                    else:
                        for r2 in range(f):
                            tmp_ref[c * f + r, pl.ds(r2, tm // d, stride=f), :] = ref[r + f * r2, :, cols]
                        st[c, pl.ds(r, tm // f, stride=f), :] = tmp_ref[c * f + r]
            vals.append(jnp.concatenate([st[c] for c in range(nc)], axis=-1))
    os_, ls_ = vals[0::2], vals[1::2]
    mx = functools.reduce(jnp.maximum, ls_)
    es = [jnp.exp(l - mx) for l in ls_]
    odil = sum(e * o for e, o in zip(es, os_)) / sum(es)
    go = go_ref[...]
    mixed = jnp.concatenate([
        _rms(ona_ref[...], go[:, :W_NA]),
        _rms(odil, go[:, W_NA:W_NA + W_DIL]),
        _rms(om_ref[...], go[:, W_NA + W_DIL:]),
    ], axis=-1).astype(BF16)
    hn = h_ref[...] + jnp.dot(mixed, w_ref[...], preferred_element_type=F32)
    hn_ref[...] = hn
    u = _rms(hn, gf_ref[...])
    if with_router:
        u_ref[...] = u
        u_hi = u.astype(BF16)
        u_lo = (u - u_hi.astype(F32)).astype(BF16)
        lg_ref[...] = (jnp.dot(u_hi, wr_ref[...], preferred_element_type=F32)
                       + jnp.dot(u_lo, wr_ref[...], preferred_element_type=F32))
    else:
        u_ref[...] = u.astype(BF16)


def _outproj(ona, dil, om, h, g_out, w_out_bf16, g_ffn, w_router_pad, batch, seq_len):
    n = h.shape[0]
    tm = TM_PROJ
    tblocks = seq_len // tm
    with_router = w_router_pad is not None
    row = lambda w: pl.BlockSpec((tm, w), lambda i: (i, 0))
    full = lambda shape: pl.BlockSpec(shape, lambda i: (0,) * len(shape))
    in_specs = [row(W_NA)]
    args = [ona]
    scratch = []
    for di, d in enumerate(DILATIONS):
        for a in dil[2 * di:2 * di + 2]:
            if d == 1:
                in_specs.append(row(W_DIL))
                args.append(a.reshape(n, W_DIL))
            else:
                in_specs.append(pl.BlockSpec((None, d, tm // d, W_DIL),
                                             lambda i: (i // tblocks, 0, i % tblocks, 0)))
                args.append(a.reshape(batch, d, seq_len // d, W_DIL))
                scratch.append(pltpu.VMEM((W_DIL // LANES, tm, LANES), F32))
    f = _RESIDUE_DILS[0]
    scratch.append(pltpu.VMEM((W_DIL // LANES * f, tm // f, LANES), F32))
    in_specs += [row(W_MEM), row(D_MODEL), full((1, D_MODEL)), full((D_MODEL, D_MODEL)),
                 full((1, D_MODEL))]
    args += [om, h, g_out, w_out_bf16, g_ffn]
    out_specs = [row(D_MODEL), row(D_MODEL)]
    out_shape = [jax.ShapeDtypeStruct((n, D_MODEL), F32),
                 jax.ShapeDtypeStruct((n, D_MODEL), F32 if with_router else BF16)]
    if with_router:
        in_specs.append(full((D_MODEL, LANES)))
        args.append(w_router_pad)
        out_specs.append(row(LANES))
        out_shape.append(jax.ShapeDtypeStruct((n, LANES), F32))
    return pl.pallas_call(
        functools.partial(_outproj_kernel, with_router=with_router),
        grid=(n // tm,),
        in_specs=in_specs,
        out_specs=out_specs,
        out_shape=out_shape,
        scratch_shapes=scratch,
        compiler_params=_cparams(("parallel",)),
        name="outproj",
    )(*args)


def _swiglu(x, wg, wu, wd):
    g = jnp.dot(x, wg, preferred_element_type=F32)
    u = jnp.dot(x, wu, preferred_element_type=F32)
    hm = (g * jax.nn.sigmoid(g) * u).astype(BF16)
    return jnp.dot(hm, wd, preferred_element_type=F32)


def _dense_ffn_kernel(u_ref, h_ref, wg_ref, wu_ref, wd_ref, o_ref, hm_ref):
    x = u_ref[...]
    for c in range(hm_ref.shape[1] // TF_DENSE):
        cols = slice(c * TF_DENSE, (c + 1) * TF_DENSE)
        g = jnp.dot(x, wg_ref[:, cols], preferred_element_type=F32)
        u = jnp.dot(x, wu_ref[:, cols], preferred_element_type=F32)
        hm_ref[:, cols] = (g * jax.nn.sigmoid(g) * u).astype(BF16)
    o_ref[...] = h_ref[...] + jnp.dot(hm_ref[...], wd_ref[...], preferred_element_type=F32)


def _dense_ffn(u, h, wg_bf16, wu_bf16, wd_bf16):
    n = h.shape[0]
    d_ff = wg_bf16.shape[1]
    tm = TM_FFN
    full = lambda shape: pl.BlockSpec(shape, lambda i: (0,) * len(shape))
    return pl.pallas_call(
        _dense_ffn_kernel,
        grid=(n // tm,),
        in_specs=[
            pl.BlockSpec((tm, D_MODEL), lambda i: (i, 0)),
            pl.BlockSpec((tm, D_MODEL), lambda i: (i, 0)),
            full((D_MODEL, d_ff)), full((D_MODEL, d_ff)), full((d_ff, D_MODEL)),
        ],
        out_specs=pl.BlockSpec((tm, D_MODEL), lambda i: (i, 0)),
        out_shape=jax.ShapeDtypeStruct((n, D_MODEL), F32),
        scratch_shapes=[pltpu.VMEM((tm, d_ff), BF16)],
        compiler_params=_cparams(("parallel",)),
        name="dense_ffn",
    )(u, h, wg_bf16, wu_bf16, wd_bf16)


def _dispatch_kernel(pend_ref, nu_ref, d0_ref, d1_ref, u_ref, x_hbm, zbuf, sem, zsem):
    i = pl.program_id(0)
    tc = u_ref.shape[0]
    zr = zbuf.shape[0]
    n_rows = x_hbm.shape[0]

    def zero_block(first_row):
        copies = [pltpu.make_async_copy(
            zbuf, x_hbm.at[pl.ds(pl.multiple_of(first_row + c * zr, zr), zr), :], zsem)
            for c in range(TM_MOE // zr)]
        for cp in copies:
            cp.start()
        for cp in copies:
            cp.wait()

    @pl.when(i == 0)
    def _():
        zbuf[...] = jnp.zeros_like(zbuf)
        prev = jnp.int32(0)
        for e in range(N_EXPERTS):
            end = pend_ref[e]

            @pl.when(end > prev)
            def _():
                zero_block(end - TM_MOE)

            prev = end

        def idle(blk, carry):
            zero_block(blk * TM_MOE)
            return carry

        lax.fori_loop(nu_ref[0], n_rows // TM_MOE, idle, 0)

    def issue(r, carry):
        src = u_ref.at[pl.ds(r, 1), :]
        pltpu.make_async_copy(src, x_hbm.at[pl.ds(d0_ref[0, 0, r], 1), :], sem).start(priority=0)
        pltpu.make_async_copy(src, x_hbm.at[pl.ds(d1_ref[0, 0, r], 1), :], sem).start(priority=1)
        return carry

    lax.fori_loop(0, tc, issue, 0, unroll=ISSUE_UNROLL)
    for _ in range(TOP_K):
        pltpu.make_async_copy(u_ref, x_hbm.at[pl.ds(0, tc), :], sem).wait()


def _dispatch(u, dest, pend, n_used, n_rows):
    n = u.shape[0]
    tc = TC_COMBINE
    nblk = n // tc
    d0 = dest[:, 0].reshape(nblk, 1, tc)
    d1 = dest[:, 1].reshape(nblk, 1, tc)
    smem = pl.BlockSpec((1, 1, tc), lambda i, pe, nu: (i, 0, 0), memory_space=pltpu.SMEM)
    grid_spec = pltpu.PrefetchScalarGridSpec(
        num_scalar_prefetch=2,
        grid=(nblk,),
        in_specs=[smem, smem, pl.BlockSpec((tc, D_MODEL), lambda i, pe, nu: (i, 0))],
        out_specs=pl.BlockSpec(memory_space=pl.ANY),
        scratch_shapes=[pltpu.VMEM((ZERO_ROWS, D_MODEL), F32), pltpu.SemaphoreType.DMA(()),
                        pltpu.SemaphoreType.DMA(())],
    )
    return pl.pallas_call(
        _dispatch_kernel,
        grid_spec=grid_spec,
        out_shape=jax.ShapeDtypeStruct((n_rows, D_MODEL), F32),
        compiler_params=_cparams(("arbitrary",)),
        name="moe_dispatch",
    )(pend, n_used, d0, d1, u)


def _moe_ffn_kernel(be_ref, nv_ref, nu_ref, x_ref, wg_ref, wu_ref, wd_ref, y_ref, xb):
    i = pl.program_id(0)
    j = pl.program_id(1)
    active = i < nu_ref[0]
    tm = x_ref.shape[0]
    nchunk = tm // MOE_CHUNK

    def chunk(c, wg, wu, wd):
        rows = slice(c * MOE_CHUNK, (c + 1) * MOE_CHUNK)
        y_ref[rows, :] += _swiglu(xb[rows, :], wg, wu, wd)

    @pl.when(j == 0)
    def _():
        y_ref[...] = jnp.zeros_like(y_ref)

    @pl.when(active)
    def _():
        @pl.when(j == 0)
        def _():
            xb[...] = x_ref[...].astype(BF16)

        nvalid = nv_ref[i]

        @pl.when(nvalid == tm)
        def _():
            wg, wu, wd = (r[...].astype(BF16) for r in (wg_ref, wu_ref, wd_ref))
            for c in range(nchunk):
                chunk(c, wg, wu, wd)

        @pl.when(nvalid < tm)
        def _():
            wg, wu, wd = (r[...].astype(BF16) for r in (wg_ref, wu_ref, wd_ref))
            for c in range(nchunk):
                @pl.when(c * MOE_CHUNK < nvalid)
                def _():
                    chunk(c, wg, wu, wd)


def _moe_ffn(x_sorted, blk_expert, blk_valid, n_used, wg, wu, wd):
    n_rows = x_sorted.shape[0]
    tm, tf = TM_MOE, TF_MOE
    nblk = n_rows // tm
    d_ff = wg.shape[2]
    nj = d_ff // tf

    def jeff(i, j, nu):
        return jnp.where(i < nu[0], j, nj - 1)

    grid_spec = pltpu.PrefetchScalarGridSpec(
        num_scalar_prefetch=3,
        grid=(nblk, nj),
        in_specs=[
            pl.BlockSpec((tm, D_MODEL), lambda i, j, be, nv, nu: (jnp.minimum(i, nu[0] - 1), 0)),
            pl.BlockSpec((None, D_MODEL, tf), lambda i, j, be, nv, nu: (be[i], 0, jeff(i, j, nu))),
            pl.BlockSpec((None, D_MODEL, tf), lambda i, j, be, nv, nu: (be[i], 0, jeff(i, j, nu))),
            pl.BlockSpec((None, tf, D_MODEL), lambda i, j, be, nv, nu: (be[i], jeff(i, j, nu), 0)),
        ],
        out_specs=pl.BlockSpec((tm, D_MODEL), lambda i, j, be, nv, nu: (i, 0)),
        scratch_shapes=[pltpu.VMEM((tm, D_MODEL), BF16)],
    )
    return pl.pallas_call(
        _moe_ffn_kernel,
        grid_spec=grid_spec,
        out_shape=jax.ShapeDtypeStruct((n_rows, D_MODEL), F32),
        compiler_params=_cparams(("arbitrary", "arbitrary")),
        name="moe_ffn",
    )(blk_expert, blk_valid, n_used, x_sorted, wg, wu, wd)


def _combine_kernel(p0_ref, p1_ref, h_ref, g_ref, y_hbm, o_ref, buf0, buf1, sem):
    tc = buf0.shape[0]

    def issue(r, carry):
        pltpu.make_async_copy(y_hbm.at[pl.ds(p0_ref[0, 0, r], 1), :], buf0.at[pl.ds(r, 1), :],
                              sem.at[0]).start(priority=0)
        pltpu.make_async_copy(y_hbm.at[pl.ds(p1_ref[0, 0, r], 1), :], buf1.at[pl.ds(r, 1), :],
                              sem.at[1]).start(priority=1)
        return carry

    lax.fori_loop(0, tc, issue, 0, unroll=ISSUE_UNROLL)
    pltpu.make_async_copy(y_hbm.at[pl.ds(0, tc), :], buf0, sem.at[0]).wait()
    pltpu.make_async_copy(y_hbm.at[pl.ds(0, tc), :], buf1, sem.at[1]).wait()
    g = g_ref[...]
    o_ref[...] = h_ref[...] + g[:, 0:1] * buf0[...] + g[:, 1:2] * buf1[...]


def _combine(h, y, dest, gates):
    n = h.shape[0]
    tc = TC_COMBINE
    nblk = n // tc
    p0 = dest[:, 0].reshape(nblk, 1, tc)
    p1 = dest[:, 1].reshape(nblk, 1, tc)
    smem = pl.BlockSpec((1, 1, tc), lambda i: (i, 0, 0), memory_space=pltpu.SMEM)
    return pl.pallas_call(
        _combine_kernel,
        grid=(nblk,),
        in_specs=[smem, smem, pl.BlockSpec((tc, D_MODEL), lambda i: (i, 0)),
                  pl.BlockSpec((tc, TOP_K), lambda i: (i, 0)), pl.BlockSpec(memory_space=pl.ANY)],
        out_specs=pl.BlockSpec((tc, D_MODEL), lambda i: (i, 0)),
        out_shape=jax.ShapeDtypeStruct((n, D_MODEL), F32),
        scratch_shapes=[pltpu.VMEM((tc, D_MODEL), F32), pltpu.VMEM((tc, D_MODEL), F32),
                        pltpu.SemaphoreType.DMA((2,))],
        compiler_params=_cparams(("arbitrary",)),
        name="moe_combine",
    )(p0, p1, h, gates, y)


def _route(logits):
    n = logits.shape[0]
    top_logit, top_idx = lax.top_k(logits, TOP_K)
    gates = jax.nn.softmax(top_logit, axis=-1)
    e_flat = top_idx.reshape(n * TOP_K).astype(jnp.int32)
    onehot = (e_flat[:, None] == jnp.arange(N_EXPERTS, dtype=jnp.int32)[None, :]).astype(jnp.int32)
    csum = jnp.cumsum(onehot, axis=0)
    counts = csum[-1]
    padded = (counts + TM_MOE - 1) // TM_MOE * TM_MOE
    pend = jnp.cumsum(padded)
    pstart = pend - padded
    dest = jnp.sum(onehot * (csum - 1 + pstart[None, :]), axis=1).reshape(n, TOP_K)
    n_rows = n * TOP_K + N_EXPERTS * TM_MOE
    nblk = n_rows // TM_MOE
    n_used = pend[-1] // TM_MOE
    blk_start = jnp.minimum(jnp.arange(nblk, dtype=jnp.int32) * TM_MOE, (n_used - 1) * TM_MOE)
    blk_expert = jnp.minimum(
        jnp.sum((blk_start[:, None] >= pend[None, :]).astype(jnp.int32), axis=1), N_EXPERTS - 1)
    blk_valid = jnp.clip(counts[blk_expert] - (blk_start - pstart[blk_expert]), 0, TM_MOE)
    i32 = lambda a: a.astype(jnp.int32)
    return (i32(dest), gates, i32(pend), i32(blk_expert), i32(blk_valid), i32(n_used).reshape(1),
            n_rows)


def _rope_tables(seq_len):
    half = HEAD_DIM // 2
    inv_freq = jnp.power(ROPE_THETA, -(2.0 / HEAD_DIM) * jnp.arange(half, dtype=F32))
    ang = jnp.arange(seq_len, dtype=F32)[:, None] * inv_freq[None, :]
    cos, sin = jnp.cos(ang), jnp.sin(ang)
    cos_t = jnp.tile(cos, (1, LANES // half))
    sin_t = jnp.tile(jnp.concatenate([-sin, sin], axis=1), (1, LANES // HEAD_DIM))
    return cos_t, sin_t


def kernel(x, mem, g_attn, w_in, g_qk_na, rpb_na, g_qk_dil, g_mem, w_mem_kv, g_qk_mem, g_out, w_out,
           g_ffn, w_gate_dense, w_up_dense, w_down_dense, w_router, w_gate_moe, w_up_moe, w_down_moe):
    batch, seq_len, _ = x.shape
    mem_len = mem.shape[1]
    depth = g_attn.shape[0]
    n = batch * seq_len
    scale = HEAD_DIM ** -0.5

    cos_t, sin_t = _rope_tables(seq_len)
    seg = jnp.asarray(np.kron(np.eye(LANES // HEAD_DIM), np.ones((HEAD_DIM, HEAD_DIM))), BF16)
    band_masks = _band_masks()
    ones = lambda w: jnp.ones((w,), F32)

    h = x.reshape(n, D_MODEL)
    mem2 = mem.reshape(batch * mem_len, D_MODEL)
    for layer in range(depth):
        gq_row = jnp.concatenate([
            jnp.tile(g_qk_na[layer, 0] * scale, H_NA), jnp.tile(g_qk_na[layer, 1], H_NA), ones(W_NA),
            jnp.tile(g_qk_dil[layer, 0] * scale, H_DIL), jnp.tile(g_qk_dil[layer, 1], H_DIL), ones(W_DIL),
            jnp.tile(g_qk_mem[layer, 0] * scale, H_MEM)]).reshape(1, IN_WIDTH)
        proj = _inproj(h, g_attn[layer].reshape(1, D_MODEL), w_in[layer].astype(BF16), gq_row, cos_t, sin_t,
                       seg, batch, seq_len)
        qa, ka, va, qb, kb, vb, qm = proj[:_N_FLAT_OUT]

        o_na = _na_attention(qa, ka, va, _na_bias_tables(rpb_na[layer], seq_len // GRID_W),
                             batch, seq_len)

        dil = []
        for d in DILATIONS:
            if d == 1:
                qkv = [a.reshape(batch, seq_len, W_DIL) for a in (qb, kb, vb)]
            else:
                base = _N_FLAT_OUT + 3 * _RESIDUE_DILS.index(d)
                qkv = [a.reshape(batch * d, seq_len // d, W_DIL) for a in proj[base:base + 3]]
            dil += _band_attention(*qkv, band_masks)

        km, vm = _memkv(mem2, g_mem[layer].reshape(1, D_MODEL), w_mem_kv[layer].astype(BF16),
                        jnp.tile(g_qk_mem[layer, 1], LANES // HEAD_DIM).reshape(1, LANES), seg)
        o_mem = _mem_attention(qm, km, vm, batch, seq_len, mem_len)

        i = layer // 2
        moe = layer % 2 == 1
        w_router_pad = None
        if moe:
            wr_hi = w_router[i].astype(BF16)
            wr_lo = (w_router[i] - wr_hi.astype(F32)).astype(BF16)
            w_router_pad = jnp.pad(jnp.concatenate([wr_hi, wr_lo], axis=1),
                                   ((0, 0), (0, LANES - 2 * N_EXPERTS)))
        outs = _outproj(o_na, dil, o_mem, h, g_out[layer].reshape(1, D_MODEL), w_out[layer].astype(BF16),
                        g_ffn[layer].reshape(1, D_MODEL), w_router_pad, batch, seq_len)
        if not moe:
            h, u = outs
            h = _dense_ffn(u, h, w_gate_dense[i].astype(BF16), w_up_dense[i].astype(BF16),
                           w_down_dense[i].astype(BF16))
        else:
            h, u, logits = outs
            dest, gates, pend, blk_expert, blk_valid, n_used, n_rows = _route(
                logits[:, :N_EXPERTS] + logits[:, N_EXPERTS:2 * N_EXPERTS])
            x_sorted = _dispatch(u, dest, pend, n_used, n_rows)
            y = _moe_ffn(x_sorted, blk_expert, blk_valid, n_used, w_gate_moe[i], w_up_moe[i], w_down_moe[i])
            h = _combine(h, y, dest, gates)
    return h.reshape(batch, seq_len, D_MODEL)
```

```python
import functools

import numpy as np
import jax
import jax.numpy as jnp
from jax import lax
from jax.experimental import pallas as pl
from jax.experimental.pallas import tpu as pltpu

F32 = jnp.float32
BF16 = jnp.bfloat16

D_MODEL = 1024
HEAD_DIM = 64
H_NA, H_DIL, H_MEM = 6, 6, 4
W_NA, W_DIL, W_MEM = H_NA * HEAD_DIM, H_DIL * HEAD_DIM, H_MEM * HEAD_DIM
IN_WIDTH = 3 * W_NA + 3 * W_DIL + W_MEM
GRID_W = 64
NA_KH, NA_KW = 8, 16
DIL_CFG = ((128, 1), (512, 4), (2048, 16))
ROPE_THETA = 10000.0
N_EXPERTS = 8
TOP_K = 2
RMS_EPS = 1e-6
NEG_INF = -1e30

LANES = 128
TM_PROJ = 512
NA_QROWS = 4
NA_QB = NA_QROWS * GRID_W
NA_KROWS = 12
NA_KB = NA_KROWS * GRID_W
DIL_QB = 128
DIL_HALF = 64
DIL_KB = DIL_QB + 2 * DIL_HALF
DIL_STEP = 512
TQ_MEM = 512
TM_FFN = 512
TF_DENSE = 256
TM_MOE = 2048
TF_MOE = 512
MOE_CHUNK = 256
TC_COMBINE = 512
ZERO_ROWS = 256
ISSUE_UNROLL = 8
VMEM_LIMIT = 56 * 1024 * 1024


def _cparams(sem):
    return pltpu.CompilerParams(dimension_semantics=sem, vmem_limit_bytes=VMEM_LIMIT)


def _rms(x, g):
    return x * lax.rsqrt(jnp.mean(x * x, axis=-1, keepdims=True) + RMS_EPS) * g


_CHUNKS = (
    [(0, c, True, False) for c in range(3)] + [(1, c, True, False) for c in range(3)]
    + [(2, c, False, False) for c in range(3)]
    + [(3, c, True, True) for c in range(3)] + [(4, c, True, True) for c in range(3)]
    + [(5, c, False, False) for c in range(3)]
    + [(6, c, True, False) for c in range(2)]
)


DILATIONS = tuple(d for _, d in DIL_CFG)
_RESIDUE_DILS = tuple(d for d in DILATIONS if d > 1)
assert len(_RESIDUE_DILS) == 2 and _RESIDUE_DILS[1] == _RESIDUE_DILS[0] ** 2
_N_FLAT_OUT = 7


def _inproj_kernel(h_ref, ga_ref, w_ref, gq_ref, cos_ref, sin_ref, seg_ref, *rest):
    out_refs, stage_ref, stage2_ref = rest[:-2], rest[-2], rest[-1]
    tm = h_ref.shape[0]
    u = _rms(h_ref[...], ga_ref[...]).astype(BF16)
    p = jnp.dot(u, w_ref[...], preferred_element_type=F32)
    seg = seg_ref[...]
    lane = lax.broadcasted_iota(jnp.int32, (1, LANES), 1)
    first_half = (lane % HEAD_DIM) < (HEAD_DIM // 2)
    for c, (oi, oc, norm, rope) in enumerate(_CHUNKS):
        x = p[:, c * LANES:(c + 1) * LANES]
        if norm:
            ss = jnp.dot((x * x).astype(BF16), seg, preferred_element_type=F32)
            x = x * lax.rsqrt(ss * (1.0 / HEAD_DIM) + RMS_EPS) * gq_ref[:, c * LANES:(c + 1) * LANES]
        if rope:
            swapped = jnp.where(first_half, pltpu.roll(x, LANES - HEAD_DIM // 2, 1),
                                pltpu.roll(x, HEAD_DIM // 2, 1))
            x = x * cos_ref[...] + swapped * sin_ref[...]
        out_refs[oi][:, oc * LANES:(oc + 1) * LANES] = x.astype(BF16)
        if 3 <= oi <= 5:
            st = (oi - 3) * (W_DIL // LANES) + oc
            cols = slice(oc * LANES, (oc + 1) * LANES)
            f = _RESIDUE_DILS[0]
            dst1 = out_refs[_N_FLAT_OUT + (oi - 3)]
            dst2 = out_refs[_N_FLAT_OUT + 3 + (oi - 3)]
            stage_ref[st] = x
            for r in range(f):
                xr = stage_ref[st, pl.ds(r, tm // f, stride=f), :]
                dst1[r, :, cols] = xr.astype(BF16)
                stage2_ref[st * f + r] = xr
                for r2 in range(f):
                    dst2[r + f * r2, :, cols] = stage2_ref[
                        st * f + r, pl.ds(r2, tm // (f * f), stride=f), :].astype(BF16)


def _inproj(h, g_attn, w_in_bf16, gq_row, cos_t, sin_t, seg, batch, seq_len):
    n = h.shape[0]
    tm = TM_PROJ
    tblocks = seq_len // tm
    widths = (W_NA, W_NA, W_NA, W_DIL, W_DIL, W_DIL, W_MEM)
    nst, f = 3 * W_DIL // LANES, _RESIDUE_DILS[0]
    out_specs = [pl.BlockSpec((tm, w), lambda i: (i, 0)) for w in widths]
    out_shape = [jax.ShapeDtypeStruct((n, w), BF16) for w in widths]
    for d in _RESIDUE_DILS:
        out_specs += [pl.BlockSpec((None, d, tm // d, W_DIL),
                                   lambda i: (i // tblocks, 0, i % tblocks, 0))] * 3
        out_shape += [jax.ShapeDtypeStruct((batch, d, seq_len // d, W_DIL), BF16)] * 3
    return pl.pallas_call(
        _inproj_kernel,
        grid=(n // tm,),
        in_specs=[
            pl.BlockSpec((tm, D_MODEL), lambda i: (i, 0)),
            pl.BlockSpec((1, D_MODEL), lambda i: (0, 0)),
            pl.BlockSpec((D_MODEL, IN_WIDTH), lambda i: (0, 0)),
            pl.BlockSpec((1, IN_WIDTH), lambda i: (0, 0)),
            pl.BlockSpec((tm, LANES), lambda i: (i % tblocks, 0)),
            pl.BlockSpec((tm, LANES), lambda i: (i % tblocks, 0)),
            pl.BlockSpec((LANES, LANES), lambda i: (0, 0)),
        ],
        out_specs=out_specs,
        out_shape=out_shape,
        scratch_shapes=[pltpu.VMEM((nst, tm, LANES), F32), pltpu.VMEM((nst * f, tm // f, LANES), F32)],
        compiler_params=_cparams(("parallel",)),
        name="inproj",
    )(h, g_attn, w_in_bf16, gq_row, cos_t, sin_t, seg)


def _pair_attention(q, k, v, bias, want_lse):
    nq = q.shape[0]
    first = lax.broadcasted_iota(jnp.int32, (1, LANES), 1) < HEAD_DIM
    zero = jnp.zeros_like(q)
    q2 = jnp.concatenate([jnp.where(first, q, zero), jnp.where(first, zero, q)], axis=0)
    v_ones = jnp.concatenate([v, jnp.ones_like(v)], axis=1)
    o, lse = _softmax_pv(q2, k, v_ones, bias, want_lse)
    out = jnp.where(first, o[:nq], o[nq:])
    lse_out = jnp.where(first, lse[:nq], lse[nq:]) if want_lse else None
    return out, lse_out


def _softmax_pv(q, k, v_ones, bias, want_lse):
    s = lax.dot_general(q, k, (((1,), (1,)), ((), ())), preferred_element_type=F32)
    if bias is not None:
        s = s + bias
    m = jnp.max(s, axis=-1, keepdims=True)
    p = jnp.exp(s - m).astype(BF16)
    ol = jnp.dot(p, v_ones, preferred_element_type=F32)
    l = ol[:, LANES:]
    return ol[:, :LANES] / l, (m + jnp.log(l)) if want_lse else None


def _na_kernel(q_ref, k_ref, v_ref, bias_ref, o_ref, *, rows):
    i = pl.program_id(1)
    srow = jnp.clip(i * NA_QROWS - NA_KH // 2, 0, rows - NA_KROWS)
    start = pl.multiple_of(srow * GRID_W, GRID_W)
    for p in range(W_NA // LANES):
        sl = slice(p * LANES, (p + 1) * LANES)
        k = k_ref[pl.ds(start, NA_KB), sl]
        v = v_ref[pl.ds(start, NA_KB), sl]
        q = q_ref[:, sl]
        first = lax.broadcasted_iota(jnp.int32, (1, LANES), 1) < HEAD_DIM
        zero = jnp.zeros_like(q)
        v_ones = jnp.concatenate([v, jnp.ones_like(v)], axis=1)
        o0, _ = _softmax_pv(jnp.where(first, q, zero), k, v_ones, bias_ref[0, 2 * p], False)
        o1, _ = _softmax_pv(jnp.where(first, zero, q), k, v_ones, bias_ref[0, 2 * p + 1], False)
        o_ref[:, sl] = jnp.where(first, o0, o1)


def _na_attention(q, k, v, bias, batch, seq_len):
    rows = seq_len // GRID_W
    nblk = seq_len // NA_QB
    q3, k3, v3 = (a.reshape(batch, seq_len, W_NA) for a in (q, k, v))

    def variant(i):
        return (i > 0).astype(jnp.int32) + (i == nblk - 1).astype(jnp.int32)

    out = pl.pallas_call(
        functools.partial(_na_kernel, rows=rows),
        grid=(batch, nblk),
        in_specs=[
            pl.BlockSpec((None, NA_QB, W_NA), lambda b, i: (b, i, 0)),
            pl.BlockSpec((None, seq_len, W_NA), lambda b, i: (b, 0, 0)),
            pl.BlockSpec((None, seq_len, W_NA), lambda b, i: (b, 0, 0)),
            pl.BlockSpec((1, H_NA, NA_QB, NA_KB), lambda b, i: (variant(i), 0, 0, 0)),
        ],
        out_specs=pl.BlockSpec((None, NA_QB, W_NA), lambda b, i: (b, i, 0)),
        out_shape=jax.ShapeDtypeStruct((batch, seq_len, W_NA), F32),
        compiler_params=_cparams(("parallel", "arbitrary")),
        name="na_attention",
    )(q3, k3, v3, bias)
    return out.reshape(batch * seq_len, W_NA)


def _na_bias_tables(rpb, rows):
    nblk = rows // NA_QROWS
    qc = np.arange(GRID_W)[:, None]
    kc = np.arange(GRID_W)[None, :]
    cs = np.clip(qc - NA_KW // 2, 0, GRID_W - NA_KW)
    col_valid = (kc >= cs) & (kc < cs + NA_KW)
    col_rel = kc - qc + (NA_KW - 1)
    col_sel = (col_rel[None] == np.arange(2 * NA_KW - 1)[:, None, None]).astype(np.float32)
    t1 = jnp.einsum('hab,bqk->haqk', rpb, jnp.asarray(col_sel), precision=lax.Precision.HIGHEST)
    t1 = jnp.where(jnp.asarray(col_valid)[None, None], t1, NEG_INF)
    masked = jnp.full((H_NA, GRID_W, GRID_W), NEG_INF, F32)
    tables = []
    for blk in (0, 1, nblk - 1):
        r0 = blk * NA_QROWS
        srow = int(np.clip(r0 - NA_KH // 2, 0, rows - NA_KROWS))
        slabs = []
        for qr in range(r0, r0 + NA_QROWS):
            rs = int(np.clip(qr - NA_KH // 2, 0, rows - NA_KH))
            slabs.append(jnp.concatenate(
                [t1[:, kr - qr + NA_KH - 1] if rs <= kr < rs + NA_KH else masked
                 for kr in range(srow, srow + NA_KROWS)], axis=-1))
        tables.append(jnp.stack(slabs, axis=1))
    return jnp.stack(tables).reshape(3, H_NA, NA_QB, NA_KB)


def _band_kernel(q_ref, k_ref, v_ref, mask_ref, o_ref, lse_ref, *, length):
    j = pl.program_id(1)
    for sb in range(DIL_STEP // DIL_QB):
        rows = slice(sb * DIL_QB, (sb + 1) * DIL_QB)
        q0 = j * DIL_STEP + sb * DIL_QB
        start = pl.multiple_of(jnp.clip(q0 - DIL_HALF, 0, length - DIL_KB), DIL_HALF)
        variant = (q0 > 0).astype(jnp.int32) + (q0 == length - DIL_QB).astype(jnp.int32)
        mask = mask_ref[variant]
        mask2 = jnp.concatenate([mask, mask], axis=0)
        for p in range(W_DIL // LANES):
            sl = slice(p * LANES, (p + 1) * LANES)
            k = k_ref[pl.ds(start, DIL_KB), sl]
            v = v_ref[pl.ds(start, DIL_KB), sl]
            o, lse = _pair_attention(q_ref[rows, sl], k, v, mask2, True)
            o_ref[rows, sl] = o
            lse_ref[rows, sl] = lse


def _band_attention(q, k, v, masks):
    nseq, length, _ = q.shape
    nblk = length // DIL_STEP
    blk = pl.BlockSpec((None, DIL_STEP, W_DIL), lambda s, j: (s, j, 0))
    seq = pl.BlockSpec((None, length, W_DIL), lambda s, j: (s, 0, 0))
    return pl.pallas_call(
        functools.partial(_band_kernel, length=length),
        grid=(nseq, nblk),
        in_specs=[blk, seq, seq, pl.BlockSpec((3, DIL_QB, DIL_KB), lambda s, j: (0, 0, 0))],
        out_specs=[blk, blk],
        out_shape=[jax.ShapeDtypeStruct((nseq, length, W_DIL), F32)] * 2,
        compiler_params=_cparams(("parallel", "arbitrary")),
        name="band_attention",
    )(q, k, v, masks)


def _band_masks():
    qq = np.arange(DIL_QB)[:, None]
    kk = np.arange(DIL_KB)[None, :]
    out = []
    for off in (0, -DIL_HALF, DIL_QB - DIL_KB):
        rel = kk + off - qq
        out.append(np.where(np.abs(rel) <= DIL_HALF, 0.0, NEG_INF))
    return jnp.asarray(np.stack(out), F32)


def _memkv_kernel(mem_ref, g_ref, w_ref, gk_ref, seg_ref, km_ref, vm_ref):
    mn = _rms(mem_ref[...], g_ref[...]).astype(BF16)
    kv = jnp.dot(mn, w_ref[...], preferred_element_type=F32)
    for c in range(W_MEM // LANES):
        x = kv[:, c * LANES:(c + 1) * LANES]
        ss = jnp.dot((x * x).astype(BF16), seg_ref[...], preferred_element_type=F32)
        x = x * lax.rsqrt(ss * (1.0 / HEAD_DIM) + RMS_EPS) * gk_ref[...]
        km_ref[:, c * LANES:(c + 1) * LANES] = x.astype(BF16)
    vm_ref[...] = kv[:, W_MEM:].astype(BF16)


def _memkv(mem2, g_mem, w_kv_bf16, gk_row, seg):
    rows = mem2.shape[0]
    full = lambda shape: pl.BlockSpec(shape, lambda i: (0,) * len(shape))
    return pl.pallas_call(
        _memkv_kernel,
        grid=(1,),
        in_specs=[full((rows, D_MODEL)), full((1, D_MODEL)), full((D_MODEL, 2 * W_MEM)),
                  full((1, LANES)), full((LANES, LANES))],
        out_specs=[full((rows, W_MEM)), full((rows, W_MEM))],
        out_shape=[jax.ShapeDtypeStruct((rows, W_MEM), BF16)] * 2,
        compiler_params=_cparams(("arbitrary",)),
        name="mem_kv",
    )(mem2, g_mem, w_kv_bf16, gk_row, seg)


def _memattn_kernel(q_ref, k_ref, v_ref, o_ref):
    for c in range(W_MEM // LANES):
        sl = slice(c * LANES, (c + 1) * LANES)
        o, _ = _pair_attention(q_ref[:, sl], k_ref[:, sl], v_ref[:, sl], None, False)
        o_ref[:, sl] = o


def _mem_attention(qm, km, vm, batch, seq_len, mem_len):
    tq = TQ_MEM
    nq = seq_len // tq
    return pl.pallas_call(
        _memattn_kernel,
        grid=(batch, nq),
        in_specs=[
            pl.BlockSpec((tq, W_MEM), lambda b, i: (b * nq + i, 0)),
            pl.BlockSpec((mem_len, W_MEM), lambda b, i: (b, 0)),
            pl.BlockSpec((mem_len, W_MEM), lambda b, i: (b, 0)),
        ],
        out_specs=pl.BlockSpec((tq, W_MEM), lambda b, i: (b * nq + i, 0)),
        out_shape=jax.ShapeDtypeStruct((batch * seq_len, W_MEM), F32),
        compiler_params=_cparams(("parallel", "arbitrary")),
        name="mem_attention",
    )(qm, km, vm)


def _outproj_kernel(ona_ref, *rest, with_router):
    nd = 2 * len(DILATIONS)
    dil_refs, rest = rest[:nd], rest[nd:]
    om_ref, h_ref, go_ref, w_ref, gf_ref = rest[:5]
    rest = rest[5:]
    if with_router:
        wr_ref, hn_ref, u_ref, lg_ref = rest[:4]
        stage_refs = rest[4:]
    else:
        hn_ref, u_ref = rest[:2]
        stage_refs = rest[2:]
    stage_refs, tmp_ref = stage_refs[:-1], stage_refs[-1]
    tm = h_ref.shape[0]
    f = _RESIDUE_DILS[0]
    nc = W_DIL // LANES
    vals = []
    si = 0
    for di, d in enumerate(DILATIONS):
        for ref in dil_refs[2 * di:2 * di + 2]:
            if d == 1:
                vals.append(ref[...])
                continue
            st = stage_refs[si]
            si += 1
            for c in range(nc):
                cols = slice(c * LANES, (c + 1) * LANES)
                for r in range(f):
                    if d == f:
                        st[c, pl.ds(r, tm // f, stride=f), :] = ref[r, :, cols]
                    else:
                        for r2 in range(f):
                            tmp_ref[c * f + r, pl.ds(r2, tm // d, stride=f), :] = ref[r + f * r2, :, cols]
                        st[c, pl.ds(r, tm // f, stride=f), :] = tmp_ref[c * f + r]
            vals.append(jnp.concatenate([st[c] for c in range(nc)], axis=-1))
    os_, ls_ = vals[0::2], vals[1::2]
    mx = functools.reduce(jnp.maximum, ls_)
    es = [jnp.exp(l - mx) for l in ls_]
    odil = sum(e * o for e, o in zip(es, os_)) / sum(es)
    go = go_ref[...]
    mixed = jnp.concatenate([
        _rms(ona_ref[...], go[:, :W_NA]),
        _rms(odil, go[:, W_NA:W_NA + W_DIL]),
        _rms(om_ref[...], go[:, W_NA + W_DIL:]),
    ], axis=-1).astype(BF16)
    hn = h_ref[...] + jnp.dot(mixed, w_ref[...], preferred_element_type=F32)
    hn_ref[...] = hn
    u = _rms(hn, gf_ref[...])
    if with_router:
        u_ref[...] = u
        u_hi = u.astype(BF16)
        u_lo = (u - u_hi.astype(F32)).astype(BF16)
        lg_ref[...] = (jnp.dot(u_hi, wr_ref[...], preferred_element_type=F32)
                       + jnp.dot(u_lo, wr_ref[...], preferred_element_type=F32))
    else:
        u_ref[...] = u.astype(BF16)


def _outproj(ona, dil, om, h, g_out, w_out_bf16, g_ffn, w_router_pad, batch, seq_len):
    n = h.shape[0]
    tm = TM_PROJ
    tblocks = seq_len // tm
    with_router = w_router_pad is not None
    row = lambda w: pl.BlockSpec((tm, w), lambda i: (i, 0))
    full = lambda shape: pl.BlockSpec(shape, lambda i: (0,) * len(shape))
    in_specs = [row(W_NA)]
    args = [ona]
    scratch = []
    for di, d in enumerate(DILATIONS):
        for a in dil[2 * di:2 * di + 2]:
            if d == 1:
                in_specs.append(row(W_DIL))
                args.append(a.reshape(n, W_DIL))
            else:
                in_specs.append(pl.BlockSpec((None, d, tm // d, W_DIL),
                                             lambda i: (i // tblocks, 0, i % tblocks, 0)))
                args.append(a.reshape(batch, d, seq_len // d, W_DIL))
                scratch.append(pltpu.VMEM((W_DIL // LANES, tm, LANES), F32))
    f = _RESIDUE_DILS[0]
    scratch.append(pltpu.VMEM((W_DIL // LANES * f, tm // f, LANES), F32))
    in_specs += [row(W_MEM), row(D_MODEL), full((1, D_MODEL)), full((D_MODEL, D_MODEL)),
                 full((1, D_MODEL))]
    args += [om, h, g_out, w_out_bf16, g_ffn]
    out_specs = [row(D_MODEL), row(D_MODEL)]
    out_shape = [jax.ShapeDtypeStruct((n, D_MODEL), F32),
                 jax.ShapeDtypeStruct((n, D_MODEL), F32 if with_router else BF16)]
    if with_router:
        in_specs.append(full((D_MODEL, LANES)))
        args.append(w_router_pad)
        out_specs.append(row(LANES))
        out_shape.append(jax.ShapeDtypeStruct((n, LANES), F32))
    return pl.pallas_call(
        functools.partial(_outproj_kernel, with_router=with_router),
        grid=(n // tm,),
        in_specs=in_specs,
        out_specs=out_specs,
        out_shape=out_shape,
        scratch_shapes=scratch,
        compiler_params=_cparams(("parallel",)),
        name="outproj",
    )(*args)


def _swiglu(x, wg, wu, wd):
    g = jnp.dot(x, wg, preferred_element_type=F32)
    u = jnp.dot(x, wu, preferred_element_type=F32)
    hm = (g * jax.nn.sigmoid(g) * u).astype(BF16)
    return jnp.dot(hm, wd, preferred_element_type=F32)


def _dense_ffn_kernel(u_ref, h_ref, wg_ref, wu_ref, wd_ref, o_ref, hm_ref):
    x = u_ref[...]
    for c in range(hm_ref.shape[1] // TF_DENSE):
        cols = slice(c * TF_DENSE, (c + 1) * TF_DENSE)
        g = jnp.dot(x, wg_ref[:, cols], preferred_element_type=F32)
        u = jnp.dot(x, wu_ref[:, cols], preferred_element_type=F32)
        hm_ref[:, cols] = (g * jax.nn.sigmoid(g) * u).astype(BF16)
    o_ref[...] = h_ref[...] + jnp.dot(hm_ref[...], wd_ref[...], preferred_element_type=F32)


def _dense_ffn(u, h, wg_bf16, wu_bf16, wd_bf16):
    n = h.shape[0]
    d_ff = wg_bf16.shape[1]
    tm = TM_FFN
    full = lambda shape: pl.BlockSpec(shape, lambda i: (0,) * len(shape))
    return pl.pallas_call(
        _dense_ffn_kernel,
        grid=(n // tm,),
        in_specs=[
            pl.BlockSpec((tm, D_MODEL), lambda i: (i, 0)),
            pl.BlockSpec((tm, D_MODEL), lambda i: (i, 0)),
            full((D_MODEL, d_ff)), full((D_MODEL, d_ff)), full((d_ff, D_MODEL)),
        ],
        out_specs=pl.BlockSpec((tm, D_MODEL), lambda i: (i, 0)),
        out_shape=jax.ShapeDtypeStruct((n, D_MODEL), F32),
        scratch_shapes=[pltpu.VMEM((tm, d_ff), BF16)],
        compiler_params=_cparams(("parallel",)),
        name="dense_ffn",
    )(u, h, wg_bf16, wu_bf16, wd_bf16)


def _dispatch_kernel(vend_ref, pend_ref, d0_ref, d1_ref, u_ref, x_hbm, zbuf, sem, zsem):
    i = pl.program_id(0)
    tc = u_ref.shape[0]
    zr = zbuf.shape[0]
    n_rows = x_hbm.shape[0]

    def zero_pieces(lo, hi):
        def start(p, carry):
            pltpu.make_async_copy(zbuf, x_hbm.at[pl.ds(pl.multiple_of(p * zr, zr), zr), :], zsem).start()
            return carry

        def wait(p, carry):
            pltpu.make_async_copy(zbuf, x_hbm.at[pl.ds(0, zr), :], zsem).wait()
            return carry

        lax.fori_loop(lo, hi, start, 0)
        lax.fori_loop(lo, hi, wait, 0)

    @pl.when(i == 0)
    def _():
        zbuf[...] = jnp.zeros_like(zbuf)
        for e in range(N_EXPERTS):
            zero_pieces(vend_ref[e] // zr, pend_ref[e] // zr)
        zero_pieces(pend_ref[N_EXPERTS - 1] // zr, n_rows // zr)

    def issue(r, carry):
        src = u_ref.at[pl.ds(r, 1), :]
        pltpu.make_async_copy(src, x_hbm.at[pl.ds(d0_ref[0, 0, r], 1), :], sem).start(priority=0)
        pltpu.make_async_copy(src, x_hbm.at[pl.ds(d1_ref[0, 0, r], 1), :], sem).start(priority=1)
        return carry

    lax.fori_loop(0, tc, issue, 0, unroll=ISSUE_UNROLL)
    for _ in range(TOP_K):
        pltpu.make_async_copy(u_ref, x_hbm.at[pl.ds(0, tc), :], sem).wait()


def _dispatch(u, dest, vend, pend, n_rows):
    n = u.shape[0]
    tc = TC_COMBINE
    nblk = n // tc
    d0 = dest[:, 0].reshape(nblk, 1, tc)
    d1 = dest[:, 1].reshape(nblk, 1, tc)
    smem = pl.BlockSpec((1, 1, tc), lambda i, ve, pe: (i, 0, 0), memory_space=pltpu.SMEM)
    grid_spec = pltpu.PrefetchScalarGridSpec(
        num_scalar_prefetch=2,
        grid=(nblk,),
        in_specs=[smem, smem, pl.BlockSpec((tc, D_MODEL), lambda i, ve, pe: (i, 0))],
        out_specs=pl.BlockSpec(memory_space=pl.ANY),
        scratch_shapes=[pltpu.VMEM((ZERO_ROWS, D_MODEL), F32), pltpu.SemaphoreType.DMA(()),
                        pltpu.SemaphoreType.DMA(())],
    )
    return pl.pallas_call(
        _dispatch_kernel,
        grid_spec=grid_spec,
        out_shape=jax.ShapeDtypeStruct((n_rows, D_MODEL), F32),
        compiler_params=_cparams(("arbitrary",)),
        name="moe_dispatch",
    )(vend, pend, d0, d1, u)


def _moe_ffn_kernel(be_ref, nv_ref, nu_ref, x_ref, wg_ref, wu_ref, wd_ref, y_ref, xb):
    i = pl.program_id(0)
    j = pl.program_id(1)
    active = i < nu_ref[0]
    tm = x_ref.shape[0]
    nchunk = tm // MOE_CHUNK

    def chunk(c, wg, wu, wd):
        rows = slice(c * MOE_CHUNK, (c + 1) * MOE_CHUNK)
        y_ref[rows, :] += _swiglu(xb[rows, :], wg, wu, wd)

    @pl.when(j == 0)
    def _():
        y_ref[...] = jnp.zeros_like(y_ref)

    @pl.when(active)
    def _():
        @pl.when(j == 0)
        def _():
            xb[...] = x_ref[...].astype(BF16)

        nvalid = nv_ref[i]

        @pl.when(nvalid == tm)
        def _():
            wg, wu, wd = (r[...].astype(BF16) for r in (wg_ref, wu_ref, wd_ref))
            for c in range(nchunk):
                chunk(c, wg, wu, wd)

        @pl.when(nvalid < tm)
        def _():
            wg, wu, wd = (r[...].astype(BF16) for r in (wg_ref, wu_ref, wd_ref))
            for c in range(nchunk):
                @pl.when(c * MOE_CHUNK < nvalid)
                def _():
                    chunk(c, wg, wu, wd)


def _moe_ffn(x_sorted, blk_expert, blk_valid, n_used, wg, wu, wd):
    n_rows = x_sorted.shape[0]
    tm, tf = TM_MOE, TF_MOE
    nblk = n_rows // tm
    d_ff = wg.shape[2]
    nj = d_ff // tf

    def jeff(i, j, nu):
        return jnp.where(i < nu[0], j, nj - 1)

    grid_spec = pltpu.PrefetchScalarGridSpec(
        num_scalar_prefetch=3,
        grid=(nblk, nj),
        in_specs=[
            pl.BlockSpec((tm, D_MODEL), lambda i, j, be, nv, nu: (jnp.minimum(i, nu[0] - 1), 0)),
            pl.BlockSpec((None, D_MODEL, tf), lambda i, j, be, nv, nu: (be[i], 0, jeff(i, j, nu))),
            pl.BlockSpec((None, D_MODEL, tf), lambda i, j, be, nv, nu: (be[i], 0, jeff(i, j, nu))),
            pl.BlockSpec((None, tf, D_MODEL), lambda i, j, be, nv, nu: (be[i], jeff(i, j, nu), 0)),
        ],
        out_specs=pl.BlockSpec((tm, D_MODEL), lambda i, j, be, nv, nu: (i, 0)),
        scratch_shapes=[pltpu.VMEM((tm, D_MODEL), BF16)],
    )
    return pl.pallas_call(
        _moe_ffn_kernel,
        grid_spec=grid_spec,
        out_shape=jax.ShapeDtypeStruct((n_rows, D_MODEL), F32),
        compiler_params=_cparams(("arbitrary", "arbitrary")),
        name="moe_ffn",
    )(blk_expert, blk_valid, n_used, x_sorted, wg, wu, wd)


def _combine_kernel(p0_ref, p1_ref, h_ref, g_ref, y_hbm, o_ref, buf0, buf1, sem):
    tc = buf0.shape[0]

    def issue(r, carry):
        pltpu.make_async_copy(y_hbm.at[pl.ds(p0_ref[0, 0, r], 1), :], buf0.at[pl.ds(r, 1), :],
                              sem.at[0]).start(priority=0)
        pltpu.make_async_copy(y_hbm.at[pl.ds(p1_ref[0, 0, r], 1), :], buf1.at[pl.ds(r, 1), :],
                              sem.at[1]).start(priority=1)
        return carry

    lax.fori_loop(0, tc, issue, 0, unroll=ISSUE_UNROLL)
    pltpu.make_async_copy(y_hbm.at[pl.ds(0, tc), :], buf0, sem.at[0]).wait()
    pltpu.make_async_copy(y_hbm.at[pl.ds(0, tc), :], buf1, sem.at[1]).wait()
    g = g_ref[...]
    o_ref[...] = h_ref[...] + g[:, 0:1] * buf0[...] + g[:, 1:2] * buf1[...]


def _combine(h, y, dest, gates):
    n = h.shape[0]
    tc = TC_COMBINE
    nblk = n // tc
    p0 = dest[:, 0].reshape(nblk, 1, tc)
    p1 = dest[:, 1].reshape(nblk, 1, tc)
    smem = pl.BlockSpec((1, 1, tc), lambda i: (i, 0, 0), memory_space=pltpu.SMEM)
    return pl.pallas_call(
        _combine_kernel,
        grid=(nblk,),
        in_specs=[smem, smem, pl.BlockSpec((tc, D_MODEL), lambda i: (i, 0)),
                  pl.BlockSpec((tc, TOP_K), lambda i: (i, 0)), pl.BlockSpec(memory_space=pl.ANY)],
        out_specs=pl.BlockSpec((tc, D_MODEL), lambda i: (i, 0)),
        out_shape=jax.ShapeDtypeStruct((n, D_MODEL), F32),
        scratch_shapes=[pltpu.VMEM((tc, D_MODEL), F32), pltpu.VMEM((tc, D_MODEL), F32),
                        pltpu.SemaphoreType.DMA((2,))],
        compiler_params=_cparams(("arbitrary",)),
        name="moe_combine",
    )(p0, p1, h, gates, y)


def _route(logits):
    n = logits.shape[0]
    top_logit, top_idx = lax.top_k(logits, TOP_K)
    gates = jax.nn.softmax(top_logit, axis=-1)
    e_flat = top_idx.reshape(n * TOP_K).astype(jnp.int32)
    onehot = (e_flat[:, None] == jnp.arange(N_EXPERTS, dtype=jnp.int32)[None, :]).astype(jnp.int32)
    csum = jnp.cumsum(onehot, axis=0)
    counts = csum[-1]
    padded = (counts + TM_MOE - 1) // TM_MOE * TM_MOE
    pend = jnp.cumsum(padded)
    pstart = pend - padded
    dest = jnp.sum(onehot * (csum - 1 + pstart[None, :]), axis=1).reshape(n, TOP_K)
    n_rows = n * TOP_K + N_EXPERTS * TM_MOE
    nblk = n_rows // TM_MOE
    n_used = pend[-1] // TM_MOE
    blk_start = jnp.minimum(jnp.arange(nblk, dtype=jnp.int32) * TM_MOE, (n_used - 1) * TM_MOE)
    blk_expert = jnp.minimum(
        jnp.sum((blk_start[:, None] >= pend[None, :]).astype(jnp.int32), axis=1), N_EXPERTS - 1)
    blk_valid = jnp.clip(counts[blk_expert] - (blk_start - pstart[blk_expert]), 0, TM_MOE)
    i32 = lambda a: a.astype(jnp.int32)
    return (i32(dest), gates, i32(pstart + counts), i32(pend), i32(blk_expert), i32(blk_valid),
            i32(n_used).reshape(1), n_rows)


def _rope_tables(seq_len):
    half = HEAD_DIM // 2
    inv_freq = jnp.power(ROPE_THETA, -(2.0 / HEAD_DIM) * jnp.arange(half, dtype=F32))
    ang = jnp.arange(seq_len, dtype=F32)[:, None] * inv_freq[None, :]
    cos, sin = jnp.cos(ang), jnp.sin(ang)
    cos_t = jnp.tile(cos, (1, LANES // half))
    sin_t = jnp.tile(jnp.concatenate([-sin, sin], axis=1), (1, LANES // HEAD_DIM))
    return cos_t, sin_t


def kernel(x, mem, g_attn, w_in, g_qk_na, rpb_na, g_qk_dil, g_mem, w_mem_kv, g_qk_mem, g_out, w_out,
           g_ffn, w_gate_dense, w_up_dense, w_down_dense, w_router, w_gate_moe, w_up_moe, w_down_moe):
    batch, seq_len, _ = x.shape
    mem_len = mem.shape[1]
    depth = g_attn.shape[0]
    n = batch * seq_len
    scale = HEAD_DIM ** -0.5

    cos_t, sin_t = _rope_tables(seq_len)
    seg = jnp.asarray(np.kron(np.eye(LANES // HEAD_DIM), np.ones((HEAD_DIM, HEAD_DIM))), BF16)
    band_masks = _band_masks()
    ones = lambda w: jnp.ones((w,), F32)

    h = x.reshape(n, D_MODEL)
    mem2 = mem.reshape(batch * mem_len, D_MODEL)
    for layer in range(depth):
        gq_row = jnp.concatenate([
            jnp.tile(g_qk_na[layer, 0] * scale, H_NA), jnp.tile(g_qk_na[layer, 1], H_NA), ones(W_NA),
            jnp.tile(g_qk_dil[layer, 0] * scale, H_DIL), jnp.tile(g_qk_dil[layer, 1], H_DIL), ones(W_DIL),
            jnp.tile(g_qk_mem[layer, 0] * scale, H_MEM)]).reshape(1, IN_WIDTH)
        proj = _inproj(h, g_attn[layer].reshape(1, D_MODEL), w_in[layer].astype(BF16), gq_row, cos_t, sin_t,
                       seg, batch, seq_len)
        qa, ka, va, qb, kb, vb, qm = proj[:_N_FLAT_OUT]

        o_na = _na_attention(qa, ka, va, _na_bias_tables(rpb_na[layer], seq_len // GRID_W),
                             batch, seq_len)

        dil = []
        for d in DILATIONS:
            if d == 1:
                qkv = [a.reshape(batch, seq_len, W_DIL) for a in (qb, kb, vb)]
            else:
                base = _N_FLAT_OUT + 3 * _RESIDUE_DILS.index(d)
                qkv = [a.reshape(batch * d, seq_len // d, W_DIL) for a in proj[base:base + 3]]
            dil += _band_attention(*qkv, band_masks)

        km, vm = _memkv(mem2, g_mem[layer].reshape(1, D_MODEL), w_mem_kv[layer].astype(BF16),
                        jnp.tile(g_qk_mem[layer, 1], LANES // HEAD_DIM).reshape(1, LANES), seg)
        o_mem = _mem_attention(qm, km, vm, batch, seq_len, mem_len)

        i = layer // 2
        moe = layer % 2 == 1
        w_router_pad = None
        if moe:
            wr_hi = w_router[i].astype(BF16)
            wr_lo = (w_router[i] - wr_hi.astype(F32)).astype(BF16)
            w_router_pad = jnp.pad(jnp.concatenate([wr_hi, wr_lo], axis=1),
                                   ((0, 0), (0, LANES - 2 * N_EXPERTS)))
        outs = _outproj(o_na, dil, o_mem, h, g_out[layer].reshape(1, D_MODEL), w_out[layer].astype(BF16),
                        g_ffn[layer].reshape(1, D_MODEL), w_router_pad, batch, seq_len)
        if not moe:
            h, u = outs
            h = _dense_ffn(u, h, w_gate_dense[i].astype(BF16), w_up_dense[i].astype(BF16),
                           w_down_dense[i].astype(BF16))
        else:
            h, u, logits = outs
            dest, gates, vend, pend, blk_expert, blk_valid, n_used, n_rows = _route(
                logits[:, :N_EXPERTS] + logits[:, N_EXPERTS:2 * N_EXPERTS])
            x_sorted = _dispatch(u, dest, vend, pend, n_rows)
            y = _moe_ffn(x_sorted, blk_expert, blk_valid, n_used, w_gate_moe[i], w_up_moe[i], w_down_moe[i])
            h = _combine(h, y, dest, gates)
    return h.reshape(batch, seq_len, D_MODEL)
```

```python
import functools

import numpy as np
import jax
import jax.numpy as jnp
from jax import lax
from jax.experimental import pallas as pl
from jax.experimental.pallas import tpu as pltpu

F32 = jnp.float32
BF16 = jnp.bfloat16

D_MODEL = 1024
HEAD_DIM = 64
H_NA, H_DIL, H_MEM = 6, 6, 4
W_NA, W_DIL, W_MEM = H_NA * HEAD_DIM, H_DIL * HEAD_DIM, H_MEM * HEAD_DIM
IN_WIDTH = 3 * W_NA + 3 * W_DIL + W_MEM
GRID_W = 64
NA_KH, NA_KW = 8, 16
DIL_CFG = ((128, 1), (512, 4), (2048, 16))
ROPE_THETA = 10000.0
N_EXPERTS = 8
TOP_K = 2
RMS_EPS = 1e-6
NEG_INF = -1e30

LANES = 128
TM_PROJ = 512
NA_QROWS = 4
NA_QB = NA_QROWS * GRID_W
NA_KROWS = 12
NA_KB = NA_KROWS * GRID_W
DIL_QB = 128
DIL_HALF = 64
DIL_KB = DIL_QB + 2 * DIL_HALF
DIL_STEP = 512
TM_FFN = 512
TF_DENSE = 256
TM_MOE = 2048
TF_MOE = 512
MOE_CHUNK = 256
TC_COMBINE = 512
ZERO_ROWS = 256
ISSUE_UNROLL = 8
VMEM_LIMIT = 56 * 1024 * 1024


def _cparams(sem):
    return pltpu.CompilerParams(dimension_semantics=sem, vmem_limit_bytes=VMEM_LIMIT)


def _rms(x, g):
    return x * lax.rsqrt(jnp.mean(x * x, axis=-1, keepdims=True) + RMS_EPS) * g


_CHUNKS = (
    [(0, c, True, False) for c in range(3)] + [(1, c, True, False) for c in range(3)]
    + [(2, c, False, False) for c in range(3)]
    + [(3, c, True, True) for c in range(3)] + [(4, c, True, True) for c in range(3)]
    + [(5, c, False, False) for c in range(3)]
    + [(6, c, True, False) for c in range(2)]
)


DILATIONS = tuple(d for _, d in DIL_CFG)
_RESIDUE_DILS = tuple(d for d in DILATIONS if d > 1)
assert len(_RESIDUE_DILS) == 2 and _RESIDUE_DILS[1] == _RESIDUE_DILS[0] ** 2
_N_FLAT_OUT = 7


def _inproj_kernel(h_ref, ga_ref, w_ref, gq_ref, cos_ref, sin_ref, seg_ref, *rest):
    out_refs, stage_ref, stage2_ref = rest[:-2], rest[-2], rest[-1]
    tm = h_ref.shape[0]
    u = _rms(h_ref[...], ga_ref[...]).astype(BF16)
    p = jnp.dot(u, w_ref[...], preferred_element_type=F32)
    seg = seg_ref[...]
    lane = lax.broadcasted_iota(jnp.int32, (1, LANES), 1)
    first_half = (lane % HEAD_DIM) < (HEAD_DIM // 2)
    normed = {}
    c = 0
    while c < len(_CHUNKS):
        if _CHUNKS[c][2]:
            assert _CHUNKS[c + 1][2]
            cols = slice(c * LANES, (c + 2) * LANES)
            x2 = p[:, cols]
            ss = jnp.dot((x2 * x2).astype(BF16), seg, preferred_element_type=F32)
            x2 = x2 * lax.rsqrt(ss * (1.0 / HEAD_DIM) + RMS_EPS) * gq_ref[:, cols]
            normed[c], normed[c + 1] = x2[:, :LANES], x2[:, LANES:]
            c += 2
        else:
            normed[c] = p[:, c * LANES:(c + 1) * LANES]
            c += 1
    for c, (oi, oc, norm, rope) in enumerate(_CHUNKS):
        x = normed[c]
        if rope:
            swapped = jnp.where(first_half, pltpu.roll(x, LANES - HEAD_DIM // 2, 1),
                                pltpu.roll(x, HEAD_DIM // 2, 1))
            x = x * cos_ref[...] + swapped * sin_ref[...]
        out_refs[oi][:, oc * LANES:(oc + 1) * LANES] = x.astype(BF16)
        if 3 <= oi <= 5:
            st = (oi - 3) * (W_DIL // LANES) + oc
            cols = slice(oc * LANES, (oc + 1) * LANES)
            f = _RESIDUE_DILS[0]
            dst1 = out_refs[_N_FLAT_OUT + (oi - 3)]
            dst2 = out_refs[_N_FLAT_OUT + 3 + (oi - 3)]
            stage_ref[st] = x
            for r in range(f):
                xr = stage_ref[st, pl.ds(r, tm // f, stride=f), :]
                dst1[r, :, cols] = xr.astype(BF16)
                stage2_ref[st * f + r] = xr
                for r2 in range(f):
                    dst2[r + f * r2, :, cols] = stage2_ref[
                        st * f + r, pl.ds(r2, tm // (f * f), stride=f), :].astype(BF16)


def _inproj(h, g_attn, w_in_bf16, gq_row, cos_t, sin_t, seg, batch, seq_len):
    n = h.shape[0]
    tm = TM_PROJ
    tblocks = seq_len // tm
    widths = (W_NA, W_NA, W_NA, W_DIL, W_DIL, W_DIL, W_MEM)
    nst, f = 3 * W_DIL // LANES, _RESIDUE_DILS[0]
    out_specs = [pl.BlockSpec((tm, w), lambda i: (i, 0)) for w in widths]
    out_shape = [jax.ShapeDtypeStruct((n, w), BF16) for w in widths]
    for d in _RESIDUE_DILS:
        out_specs += [pl.BlockSpec((None, d, tm // d, W_DIL),
                                   lambda i: (i // tblocks, 0, i % tblocks, 0))] * 3
        out_shape += [jax.ShapeDtypeStruct((batch, d, seq_len // d, W_DIL), BF16)] * 3
    return pl.pallas_call(
        _inproj_kernel,
        grid=(n // tm,),
        in_specs=[
            pl.BlockSpec((tm, D_MODEL), lambda i: (i, 0)),
            pl.BlockSpec((1, D_MODEL), lambda i: (0, 0)),
            pl.BlockSpec((D_MODEL, IN_WIDTH), lambda i: (0, 0)),
            pl.BlockSpec((1, IN_WIDTH), lambda i: (0, 0)),
            pl.BlockSpec((tm, LANES), lambda i: (i % tblocks, 0)),
            pl.BlockSpec((tm, LANES), lambda i: (i % tblocks, 0)),
            pl.BlockSpec((2 * LANES, 2 * LANES), lambda i: (0, 0)),
        ],
        out_specs=out_specs,
        out_shape=out_shape,
        scratch_shapes=[pltpu.VMEM((nst, tm, LANES), F32), pltpu.VMEM((nst * f, tm // f, LANES), F32)],
        compiler_params=_cparams(("parallel",)),
        name="inproj",
    )(h, g_attn, w_in_bf16, gq_row, cos_t, sin_t, seg)


def _pair_attention(q, k, v, bias, want_lse):
    nq = q.shape[0]
    first = lax.broadcasted_iota(jnp.int32, (1, LANES), 1) < HEAD_DIM
    zero = jnp.zeros_like(q)
    q2 = jnp.concatenate([jnp.where(first, q, zero), jnp.where(first, zero, q)], axis=0)
    v_ones = jnp.concatenate([v, jnp.ones_like(v)], axis=1)
    o, lse = _softmax_pv(q2, k, v_ones, bias, want_lse)
    out = jnp.where(first, o[:nq], o[nq:])
    lse_out = jnp.where(first, lse[:nq], lse[nq:]) if want_lse else None
    return out, lse_out


def _softmax_pv(q, k, v_ones, bias, want_lse):
    s = lax.dot_general(q, k, (((1,), (1,)), ((), ())), preferred_element_type=F32)
    if bias is not None:
        s = s + bias
    m = jnp.max(s, axis=-1, keepdims=True)
    p = jnp.exp(s - m).astype(BF16)
    ol = jnp.dot(p, v_ones, preferred_element_type=F32)
    l = ol[:, LANES:]
    return ol[:, :LANES] / l, (m + jnp.log(l)) if want_lse else None


def _na_mem_kernel(q_ref, k_ref, v_ref, bias_ref, qm_ref, km_ref, vm_ref, o_ref, om_ref, *, rows):
    i = pl.program_id(1)
    srow = jnp.clip(i * NA_QROWS - NA_KH // 2, 0, rows - NA_KROWS)
    start = pl.multiple_of(srow * GRID_W, GRID_W)
    first = lax.broadcasted_iota(jnp.int32, (1, LANES), 1) < HEAD_DIM
    for p in range(W_NA // LANES):
        sl = slice(p * LANES, (p + 1) * LANES)
        k = k_ref[pl.ds(start, NA_KB), sl]
        v = v_ref[pl.ds(start, NA_KB), sl]
        q = q_ref[:, sl]
        zero = jnp.zeros_like(q)
        v_ones = jnp.concatenate([v, jnp.ones_like(v)], axis=1)
        o0, _ = _softmax_pv(jnp.where(first, q, zero), k, v_ones, bias_ref[0, 2 * p], False)
        o1, _ = _softmax_pv(jnp.where(first, zero, q), k, v_ones, bias_ref[0, 2 * p + 1], False)
        o_ref[:, sl] = jnp.where(first, o0, o1)
    for c in range(W_MEM // LANES):
        sl = slice(c * LANES, (c + 1) * LANES)
        om_ref[:, sl], _ = _pair_attention(qm_ref[:, sl], km_ref[:, sl], vm_ref[:, sl], None, False)


def _na_mem_attention(q, k, v, bias, qm, km, vm, batch, seq_len, mem_len):
    rows = seq_len // GRID_W
    nblk = seq_len // NA_QB
    q3, k3, v3 = (a.reshape(batch, seq_len, W_NA) for a in (q, k, v))

    def variant(i):
        return (i > 0).astype(jnp.int32) + (i == nblk - 1).astype(jnp.int32)

    o, om = pl.pallas_call(
        functools.partial(_na_mem_kernel, rows=rows),
        grid=(batch, nblk),
        in_specs=[
            pl.BlockSpec((None, NA_QB, W_NA), lambda b, i: (b, i, 0)),
            pl.BlockSpec((None, seq_len, W_NA), lambda b, i: (b, 0, 0)),
            pl.BlockSpec((None, seq_len, W_NA), lambda b, i: (b, 0, 0)),
            pl.BlockSpec((1, H_NA, NA_QB, NA_KB), lambda b, i: (variant(i), 0, 0, 0)),
            pl.BlockSpec((None, NA_QB, W_MEM), lambda b, i: (b, i, 0)),
            pl.BlockSpec((None, mem_len, W_MEM), lambda b, i: (b, 0, 0)),
            pl.BlockSpec((None, mem_len, W_MEM), lambda b, i: (b, 0, 0)),
        ],
        out_specs=[pl.BlockSpec((None, NA_QB, W_NA), lambda b, i: (b, i, 0)),
                   pl.BlockSpec((None, NA_QB, W_MEM), lambda b, i: (b, i, 0))],
        out_shape=[jax.ShapeDtypeStruct((batch, seq_len, W_NA), F32),
                   jax.ShapeDtypeStruct((batch, seq_len, W_MEM), F32)],
        compiler_params=_cparams(("parallel", "arbitrary")),
        name="na_mem_attention",
    )(q3, k3, v3, bias, qm.reshape(batch, seq_len, W_MEM), km.reshape(batch, mem_len, W_MEM),
      vm.reshape(batch, mem_len, W_MEM))
    return o.reshape(batch * seq_len, W_NA), om.reshape(batch * seq_len, W_MEM)


def _na_bias_tables(rpb, rows):
    nblk = rows // NA_QROWS
    qc = np.arange(GRID_W)[:, None]
    kc = np.arange(GRID_W)[None, :]
    cs = np.clip(qc - NA_KW // 2, 0, GRID_W - NA_KW)
    col_valid = (kc >= cs) & (kc < cs + NA_KW)
    col_rel = kc - qc + (NA_KW - 1)
    col_sel = (col_rel[None] == np.arange(2 * NA_KW - 1)[:, None, None]).astype(np.float32)
    t1 = jnp.einsum('hab,bqk->haqk', rpb, jnp.asarray(col_sel), precision=lax.Precision.HIGHEST)
    t1 = jnp.where(jnp.asarray(col_valid)[None, None], t1, NEG_INF)
    masked = jnp.full((H_NA, GRID_W, GRID_W), NEG_INF, F32)
    tables = []
    for blk in (0, 1, nblk - 1):
        r0 = blk * NA_QROWS
        srow = int(np.clip(r0 - NA_KH // 2, 0, rows - NA_KROWS))
        slabs = []
        for qr in range(r0, r0 + NA_QROWS):
            rs = int(np.clip(qr - NA_KH // 2, 0, rows - NA_KH))
            slabs.append(jnp.concatenate(
                [t1[:, kr - qr + NA_KH - 1] if rs <= kr < rs + NA_KH else masked
                 for kr in range(srow, srow + NA_KROWS)], axis=-1))
        tables.append(jnp.stack(slabs, axis=1))
    return jnp.stack(tables).reshape(3, H_NA, NA_QB, NA_KB)


def _band_kernel(q_ref, k_ref, v_ref, mask_ref, o_ref, lse_ref, *, length):
    j = pl.program_id(1)
    for sb in range(DIL_STEP // DIL_QB):
        rows = slice(sb * DIL_QB, (sb + 1) * DIL_QB)
        q0 = j * DIL_STEP + sb * DIL_QB
        start = pl.multiple_of(jnp.clip(q0 - DIL_HALF, 0, length - DIL_KB), DIL_HALF)
        variant = (q0 > 0).astype(jnp.int32) + (q0 == length - DIL_QB).astype(jnp.int32)
        mask = mask_ref[variant]
        mask2 = jnp.concatenate([mask, mask], axis=0)
        for p in range(W_DIL // LANES):
            sl = slice(p * LANES, (p + 1) * LANES)
            k = k_ref[pl.ds(start, DIL_KB), sl]
            v = v_ref[pl.ds(start, DIL_KB), sl]
            o, lse = _pair_attention(q_ref[rows, sl], k, v, mask2, True)
            o_ref[rows, sl] = o
            lse_ref[rows, sl] = lse


def _band_attention(q, k, v, masks):
    nseq, length, _ = q.shape
    nblk = length // DIL_STEP
    blk = pl.BlockSpec((None, DIL_STEP, W_DIL), lambda s, j: (s, j, 0))
    seq = pl.BlockSpec((None, length, W_DIL), lambda s, j: (s, 0, 0))
    return pl.pallas_call(
        functools.partial(_band_kernel, length=length),
        grid=(nseq, nblk),
        in_specs=[blk, seq, seq, pl.BlockSpec((3, DIL_QB, DIL_KB), lambda s, j: (0, 0, 0))],
        out_specs=[blk, blk],
        out_shape=[jax.ShapeDtypeStruct((nseq, length, W_DIL), F32)] * 2,
        compiler_params=_cparams(("parallel", "arbitrary")),
        name="band_attention",
    )(q, k, v, masks)


def _band_masks():
    qq = np.arange(DIL_QB)[:, None]
    kk = np.arange(DIL_KB)[None, :]
    out = []
    for off in (0, -DIL_HALF, DIL_QB - DIL_KB):
        rel = kk + off - qq
        out.append(np.where(np.abs(rel) <= DIL_HALF, 0.0, NEG_INF))
    return jnp.asarray(np.stack(out), F32)


def _memkv_kernel(mem_ref, g_ref, w_ref, gk_ref, seg_ref, km_ref, vm_ref):
    mn = _rms(mem_ref[...], g_ref[...]).astype(BF16)
    kv = jnp.dot(mn, w_ref[...], preferred_element_type=F32)
    for c in range(W_MEM // LANES):
        x = kv[:, c * LANES:(c + 1) * LANES]
        ss = jnp.dot((x * x).astype(BF16), seg_ref[:LANES, :LANES], preferred_element_type=F32)
        x = x * lax.rsqrt(ss * (1.0 / HEAD_DIM) + RMS_EPS) * gk_ref[...]
        km_ref[:, c * LANES:(c + 1) * LANES] = x.astype(BF16)
    vm_ref[...] = kv[:, W_MEM:].astype(BF16)


def _memkv(mem2, g_mem, w_kv_bf16, gk_row, seg):
    rows = mem2.shape[0]
    full = lambda shape: pl.BlockSpec(shape, lambda i: (0,) * len(shape))
    return pl.pallas_call(
        _memkv_kernel,
        grid=(1,),
        in_specs=[full((rows, D_MODEL)), full((1, D_MODEL)), full((D_MODEL, 2 * W_MEM)),
                  full((1, LANES)), full((2 * LANES, 2 * LANES))],
        out_specs=[full((rows, W_MEM)), full((rows, W_MEM))],
        out_shape=[jax.ShapeDtypeStruct((rows, W_MEM), BF16)] * 2,
        compiler_params=_cparams(("arbitrary",)),
        name="mem_kv",
    )(mem2, g_mem, w_kv_bf16, gk_row, seg)


def _outproj_kernel(ona_ref, *rest, with_router):
    nd = 2 * len(DILATIONS)
    dil_refs, rest = rest[:nd], rest[nd:]
    om_ref, h_ref, go_ref, w_ref, gf_ref = rest[:5]
    rest = rest[5:]
    if with_router:
        wr_ref, hn_ref, u_ref, lg_ref = rest[:4]
        stage_refs = rest[4:]
    else:
        hn_ref, u_ref = rest[:2]
        stage_refs = rest[2:]
    stage_refs, tmp_ref = stage_refs[:-1], stage_refs[-1]
    tm = h_ref.shape[0]
    f = _RESIDUE_DILS[0]
    nc = W_DIL // LANES
    vals = []
    si = 0
    for di, d in enumerate(DILATIONS):
        for ref in dil_refs[2 * di:2 * di + 2]:
            if d == 1:
                vals.append(ref[...])
                continue
            st = stage_refs[si]
            si += 1
            for c in range(nc):
                cols = slice(c * LANES, (c + 1) * LANES)
                for r in range(f):
                    if d == f:
                        st[c, pl.ds(r, tm // f, stride=f), :] = ref[r, :, cols]
                    else:
                        for r2 in range(f):
                            tmp_ref[c * f + r, pl.ds(r2, tm // d, stride=f), :] = ref[r + f * r2, :, cols]
                        st[c, pl.ds(r, tm // f, stride=f), :] = tmp_ref[c * f + r]
            vals.append(jnp.concatenate([st[c] for c in range(nc)], axis=-1))
    os_, ls_ = vals[0::2], vals[1::2]
    mx = functools.reduce(jnp.maximum, ls_)
    es = [jnp.exp(l - mx) for l in ls_]
    odil = sum(e * o for e, o in zip(es, os_)) / sum(es)
    go = go_ref[...]
    mixed = jnp.concatenate([
        _rms(ona_ref[...], go[:, :W_NA]),
        _rms(odil, go[:, W_NA:W_NA + W_DIL]),
        _rms(om_ref[...], go[:, W_NA + W_DIL:]),
    ], axis=-1).astype(BF16)
    hn = h_ref[...] + jnp.dot(mixed, w_ref[...], preferred_element_type=F32)
    hn_ref[...] = hn
    u = _rms(hn, gf_ref[...])
    if with_router:
        u_ref[...] = u
        u_hi = u.astype(BF16)
        u_lo = (u - u_hi.astype(F32)).astype(BF16)
        lg_ref[...] = (jnp.dot(u_hi, wr_ref[...], preferred_element_type=F32)
                       + jnp.dot(u_lo, wr_ref[...], preferred_element_type=F32))
    else:
        u_ref[...] = u.astype(BF16)


def _outproj(ona, dil, om, h, g_out, w_out_bf16, g_ffn, w_router_pad, batch, seq_len):
    n = h.shape[0]
    tm = TM_PROJ
    tblocks = seq_len // tm
    with_router = w_router_pad is not None
    row = lambda w: pl.BlockSpec((tm, w), lambda i: (i, 0))
    full = lambda shape: pl.BlockSpec(shape, lambda i: (0,) * len(shape))
    in_specs = [row(W_NA)]
    args = [ona]
    scratch = []
    for di, d in enumerate(DILATIONS):
        for a in dil[2 * di:2 * di + 2]:
            if d == 1:
                in_specs.append(row(W_DIL))
                args.append(a.reshape(n, W_DIL))
            else:
                in_specs.append(pl.BlockSpec((None, d, tm // d, W_DIL),
                                             lambda i: (i // tblocks, 0, i % tblocks, 0)))
                args.append(a.reshape(batch, d, seq_len // d, W_DIL))
                scratch.append(pltpu.VMEM((W_DIL // LANES, tm, LANES), F32))
    f = _RESIDUE_DILS[0]
    scratch.append(pltpu.VMEM((W_DIL // LANES * f, tm // f, LANES), F32))
    in_specs += [row(W_MEM), row(D_MODEL), full((1, D_MODEL)), full((D_MODEL, D_MODEL)),
                 full((1, D_MODEL))]
    args += [om, h, g_out, w_out_bf16, g_ffn]
    out_specs = [row(D_MODEL), row(D_MODEL)]
    out_shape = [jax.ShapeDtypeStruct((n, D_MODEL), F32),
                 jax.ShapeDtypeStruct((n, D_MODEL), F32 if with_router else BF16)]
    if with_router:
        in_specs.append(full((D_MODEL, LANES)))
        args.append(w_router_pad)
        out_specs.append(row(LANES))
        out_shape.append(jax.ShapeDtypeStruct((n, LANES), F32))
    return pl.pallas_call(
        functools.partial(_outproj_kernel, with_router=with_router),
        grid=(n // tm,),
        in_specs=in_specs,
        out_specs=out_specs,
        out_shape=out_shape,
        scratch_shapes=scratch,
        compiler_params=_cparams(("parallel",)),
        name="outproj",
    )(*args)


def _swiglu(x, wg, wu, wd):
    g = jnp.dot(x, wg, preferred_element_type=F32)
    u = jnp.dot(x, wu, preferred_element_type=F32)
    hm = (g * jax.nn.sigmoid(g) * u).astype(BF16)
    return jnp.dot(hm, wd, preferred_element_type=F32)


def _dense_ffn_kernel(u_ref, h_ref, wg_ref, wu_ref, wd_ref, o_ref, hm_ref):
    x = u_ref[...]
    for c in range(hm_ref.shape[1] // TF_DENSE):
        cols = slice(c * TF_DENSE, (c + 1) * TF_DENSE)
        g = jnp.dot(x, wg_ref[:, cols], preferred_element_type=F32)
        u = jnp.dot(x, wu_ref[:, cols], preferred_element_type=F32)
        hm_ref[:, cols] = (g * jax.nn.sigmoid(g) * u).astype(BF16)
    o_ref[...] = h_ref[...] + jnp.dot(hm_ref[...], wd_ref[...], preferred_element_type=F32)


def _dense_ffn(u, h, wg_bf16, wu_bf16, wd_bf16):
    n = h.shape[0]
    d_ff = wg_bf16.shape[1]
    tm = TM_FFN
    full = lambda shape: pl.BlockSpec(shape, lambda i: (0,) * len(shape))
    return pl.pallas_call(
        _dense_ffn_kernel,
        grid=(n // tm,),
        in_specs=[
            pl.BlockSpec((tm, D_MODEL), lambda i: (i, 0)),
            pl.BlockSpec((tm, D_MODEL), lambda i: (i, 0)),
            full((D_MODEL, d_ff)), full((D_MODEL, d_ff)), full((d_ff, D_MODEL)),
        ],
        out_specs=pl.BlockSpec((tm, D_MODEL), lambda i: (i, 0)),
        out_shape=jax.ShapeDtypeStruct((n, D_MODEL), F32),
        scratch_shapes=[pltpu.VMEM((tm, d_ff), BF16)],
        compiler_params=_cparams(("parallel",)),
        name="dense_ffn",
    )(u, h, wg_bf16, wu_bf16, wd_bf16)


def _dispatch_kernel(vend_ref, pend_ref, d0_ref, d1_ref, u_ref, x_hbm, zbuf, sem, zsem):
    i = pl.program_id(0)
    tc = u_ref.shape[0]
    zr = zbuf.shape[0]
    n_rows = x_hbm.shape[0]

    def zero_pieces(lo, hi):
        def start(p, carry):
            pltpu.make_async_copy(zbuf, x_hbm.at[pl.ds(pl.multiple_of(p * zr, zr), zr), :], zsem).start()
            return carry

        def wait(p, carry):
            pltpu.make_async_copy(zbuf, x_hbm.at[pl.ds(0, zr), :], zsem).wait()
            return carry

        lax.fori_loop(lo, hi, start, 0)
        lax.fori_loop(lo, hi, wait, 0)

    @pl.when(i == 0)
    def _():
        zbuf[...] = jnp.zeros_like(zbuf)
        for e in range(N_EXPERTS):
            zero_pieces(vend_ref[e] // zr, pend_ref[e] // zr)
        zero_pieces(pend_ref[N_EXPERTS - 1] // zr, n_rows // zr)

    def issue(r, carry):
        src = u_ref.at[pl.ds(r, 1), :]
        pltpu.make_async_copy(src, x_hbm.at[pl.ds(d0_ref[0, 0, r], 1), :], sem).start(priority=0)
        pltpu.make_async_copy(src, x_hbm.at[pl.ds(d1_ref[0, 0, r], 1), :], sem).start(priority=1)
        return carry

    lax.fori_loop(0, tc, issue, 0, unroll=ISSUE_UNROLL)
    for _ in range(TOP_K):
        pltpu.make_async_copy(u_ref, x_hbm.at[pl.ds(0, tc), :], sem).wait()


def _dispatch(u, dest, vend, pend, n_rows):
    n = u.shape[0]
    tc = TC_COMBINE
    nblk = n // tc
    d0 = dest[:, 0].reshape(nblk, 1, tc)
    d1 = dest[:, 1].reshape(nblk, 1, tc)
    smem = pl.BlockSpec((1, 1, tc), lambda i, ve, pe: (i, 0, 0), memory_space=pltpu.SMEM)
    grid_spec = pltpu.PrefetchScalarGridSpec(
        num_scalar_prefetch=2,
        grid=(nblk,),
        in_specs=[smem, smem, pl.BlockSpec((tc, D_MODEL), lambda i, ve, pe: (i, 0))],
        out_specs=pl.BlockSpec(memory_space=pl.ANY),
        scratch_shapes=[pltpu.VMEM((ZERO_ROWS, D_MODEL), F32), pltpu.SemaphoreType.DMA(()),
                        pltpu.SemaphoreType.DMA(())],
    )
    return pl.pallas_call(
        _dispatch_kernel,
        grid_spec=grid_spec,
        out_shape=jax.ShapeDtypeStruct((n_rows, D_MODEL), F32),
        compiler_params=_cparams(("arbitrary",)),
        name="moe_dispatch",
    )(vend, pend, d0, d1, u)


def _moe_ffn_kernel(be_ref, nv_ref, nu_ref, x_ref, wg_ref, wu_ref, wd_ref, y_ref, xb):
    i = pl.program_id(0)
    j = pl.program_id(1)
    active = i < nu_ref[0]
    tm = x_ref.shape[0]
    nchunk = tm // MOE_CHUNK

    def chunk(c, wg, wu, wd):
        rows = slice(c * MOE_CHUNK, (c + 1) * MOE_CHUNK)
        y_ref[rows, :] += _swiglu(xb[rows, :], wg, wu, wd)

    @pl.when(j == 0)
    def _():
        y_ref[...] = jnp.zeros_like(y_ref)

    @pl.when(active)
    def _():
        @pl.when(j == 0)
        def _():
            xb[...] = x_ref[...].astype(BF16)

        nvalid = nv_ref[i]

        @pl.when(nvalid > tm - MOE_CHUNK)
        def _():
            wg, wu, wd = (r[...].astype(BF16) for r in (wg_ref, wu_ref, wd_ref))
            for c in range(nchunk):
                chunk(c, wg, wu, wd)

        @pl.when(nvalid <= tm - MOE_CHUNK)
        def _():
            wg, wu, wd = (r[...].astype(BF16) for r in (wg_ref, wu_ref, wd_ref))
            for c in range(nchunk):
                @pl.when(c * MOE_CHUNK < nvalid)
                def _():
                    chunk(c, wg, wu, wd)


def _moe_ffn(x_sorted, blk_expert, blk_valid, n_used, wg, wu, wd):
    n_rows = x_sorted.shape[0]
    tm, tf = TM_MOE, TF_MOE
    nblk = n_rows // tm
    d_ff = wg.shape[2]
    nj = d_ff // tf

    def jeff(i, j, nu):
        return jnp.where(i < nu[0], j, nj - 1)

    grid_spec = pltpu.PrefetchScalarGridSpec(
        num_scalar_prefetch=3,
        grid=(nblk, nj),
        in_specs=[
            pl.BlockSpec((tm, D_MODEL), lambda i, j, be, nv, nu: (jnp.minimum(i, nu[0] - 1), 0)),
            pl.BlockSpec((None, D_MODEL, tf), lambda i, j, be, nv, nu: (be[i], 0, jeff(i, j, nu))),
            pl.BlockSpec((None, D_MODEL, tf), lambda i, j, be, nv, nu: (be[i], 0, jeff(i, j, nu))),
            pl.BlockSpec((None, tf, D_MODEL), lambda i, j, be, nv, nu: (be[i], jeff(i, j, nu), 0)),
        ],
        out_specs=pl.BlockSpec((tm, D_MODEL), lambda i, j, be, nv, nu: (i, 0)),
        scratch_shapes=[pltpu.VMEM((tm, D_MODEL), BF16)],
    )
    return pl.pallas_call(
        _moe_ffn_kernel,
        grid_spec=grid_spec,
        out_shape=jax.ShapeDtypeStruct((n_rows, D_MODEL), F32),
        compiler_params=_cparams(("arbitrary", "arbitrary")),
        name="moe_ffn",
    )(blk_expert, blk_valid, n_used, x_sorted, wg, wu, wd)


def _combine_kernel(p0_ref, p1_ref, h_ref, g_ref, y_hbm, o_ref, buf0, buf1, sem):
    tc = buf0.shape[0]

    def issue(r, carry):
        pltpu.make_async_copy(y_hbm.at[pl.ds(p0_ref[0, 0, r], 1), :], buf0.at[pl.ds(r, 1), :],
                              sem.at[0]).start(priority=0)
        pltpu.make_async_copy(y_hbm.at[pl.ds(p1_ref[0, 0, r], 1), :], buf1.at[pl.ds(r, 1), :],
                              sem.at[1]).start(priority=1)
        return carry

    lax.fori_loop(0, tc, issue, 0, unroll=ISSUE_UNROLL)
    pltpu.make_async_copy(y_hbm.at[pl.ds(0, tc), :], buf0, sem.at[0]).wait()
    pltpu.make_async_copy(y_hbm.at[pl.ds(0, tc), :], buf1, sem.at[1]).wait()
    g = g_ref[...]
    o_ref[...] = h_ref[...] + g[:, 0:1] * buf0[...] + g[:, 1:2] * buf1[...]


def _combine(h, y, dest, gates):
    n = h.shape[0]
    tc = TC_COMBINE
    nblk = n // tc
    p0 = dest[:, 0].reshape(nblk, 1, tc)
    p1 = dest[:, 1].reshape(nblk, 1, tc)
    smem = pl.BlockSpec((1, 1, tc), lambda i: (i, 0, 0), memory_space=pltpu.SMEM)
    return pl.pallas_call(
        _combine_kernel,
        grid=(nblk,),
        in_specs=[smem, smem, pl.BlockSpec((tc, D_MODEL), lambda i: (i, 0)),
                  pl.BlockSpec((tc, TOP_K), lambda i: (i, 0)), pl.BlockSpec(memory_space=pl.ANY)],
        out_specs=pl.BlockSpec((tc, D_MODEL), lambda i: (i, 0)),
        out_shape=jax.ShapeDtypeStruct((n, D_MODEL), F32),
        scratch_shapes=[pltpu.VMEM((tc, D_MODEL), F32), pltpu.VMEM((tc, D_MODEL), F32),
                        pltpu.SemaphoreType.DMA((2,))],
        compiler_params=_cparams(("arbitrary",)),
        name="moe_combine",
    )(p0, p1, h, gates, y)


def _route(logits):
    n = logits.shape[0]
    top_logit, top_idx = lax.top_k(logits, TOP_K)
    gates = jax.nn.softmax(top_logit, axis=-1)
    e_flat = top_idx.reshape(n * TOP_K).astype(jnp.int32)
    onehot = (e_flat[:, None] == jnp.arange(N_EXPERTS, dtype=jnp.int32)[None, :]).astype(jnp.int32)
    csum = jnp.cumsum(onehot, axis=0)
    counts = csum[-1]
    padded = (counts + TM_MOE - 1) // TM_MOE * TM_MOE
    pend = jnp.cumsum(padded)
    pstart = pend - padded
    dest = jnp.sum(onehot * (csum - 1 + pstart[None, :]), axis=1).reshape(n, TOP_K)
    n_rows = n * TOP_K + N_EXPERTS * TM_MOE
    nblk = n_rows // TM_MOE
    n_used = pend[-1] // TM_MOE
    blk_start = jnp.minimum(jnp.arange(nblk, dtype=jnp.int32) * TM_MOE, (n_used - 1) * TM_MOE)
    blk_expert = jnp.minimum(
        jnp.sum((blk_start[:, None] >= pend[None, :]).astype(jnp.int32), axis=1), N_EXPERTS - 1)
    blk_valid = jnp.clip(counts[blk_expert] - (blk_start - pstart[blk_expert]), 0, TM_MOE)
    i32 = lambda a: a.astype(jnp.int32)
    return (i32(dest), gates, i32(pstart + counts), i32(pend), i32(blk_expert), i32(blk_valid),
            i32(n_used).reshape(1), n_rows)


def _rope_tables(seq_len):
    half = HEAD_DIM // 2
    inv_freq = jnp.power(ROPE_THETA, -(2.0 / HEAD_DIM) * jnp.arange(half, dtype=F32))
    ang = jnp.arange(seq_len, dtype=F32)[:, None] * inv_freq[None, :]
    cos, sin = jnp.cos(ang), jnp.sin(ang)
    cos_t = jnp.tile(cos, (1, LANES // half))
    sin_t = jnp.tile(jnp.concatenate([-sin, sin], axis=1), (1, LANES // HEAD_DIM))
    return cos_t, sin_t


def kernel(x, mem, g_attn, w_in, g_qk_na, rpb_na, g_qk_dil, g_mem, w_mem_kv, g_qk_mem, g_out, w_out,
           g_ffn, w_gate_dense, w_up_dense, w_down_dense, w_router, w_gate_moe, w_up_moe, w_down_moe):
    batch, seq_len, _ = x.shape
    mem_len = mem.shape[1]
    depth = g_attn.shape[0]
    n = batch * seq_len
    scale = HEAD_DIM ** -0.5

    cos_t, sin_t = _rope_tables(seq_len)
    seg = jnp.asarray(np.kron(np.eye(2 * LANES // HEAD_DIM), np.ones((HEAD_DIM, HEAD_DIM))), BF16)
    band_masks = _band_masks()
    ones = lambda w: jnp.ones((w,), F32)

    h = x.reshape(n, D_MODEL)
    mem2 = mem.reshape(batch * mem_len, D_MODEL)
    for layer in range(depth):
        gq_row = jnp.concatenate([
            jnp.tile(g_qk_na[layer, 0] * scale, H_NA), jnp.tile(g_qk_na[layer, 1], H_NA), ones(W_NA),
            jnp.tile(g_qk_dil[layer, 0] * scale, H_DIL), jnp.tile(g_qk_dil[layer, 1], H_DIL), ones(W_DIL),
            jnp.tile(g_qk_mem[layer, 0] * scale, H_MEM)]).reshape(1, IN_WIDTH)
        proj = _inproj(h, g_attn[layer].reshape(1, D_MODEL), w_in[layer].astype(BF16), gq_row, cos_t, sin_t,
                       seg, batch, seq_len)
        qa, ka, va, qb, kb, vb, qm = proj[:_N_FLAT_OUT]

        dil = []
        for d in DILATIONS:
            if d == 1:
                qkv = [a.reshape(batch, seq_len, W_DIL) for a in (qb, kb, vb)]
            else:
                base = _N_FLAT_OUT + 3 * _RESIDUE_DILS.index(d)
                qkv = [a.reshape(batch * d, seq_len // d, W_DIL) for a in proj[base:base + 3]]
            dil += _band_attention(*qkv, band_masks)

        km, vm = _memkv(mem2, g_mem[layer].reshape(1, D_MODEL), w_mem_kv[layer].astype(BF16),
                        jnp.tile(g_qk_mem[layer, 1], LANES // HEAD_DIM).reshape(1, LANES), seg)
        o_na, o_mem = _na_mem_attention(qa, ka, va, _na_bias_tables(rpb_na[layer], seq_len // GRID_W),
                                        qm, km, vm, batch, seq_len, mem_len)

        i = layer // 2
        moe = layer % 2 == 1
        w_router_pad = None
        if moe:
            wr_hi = w_router[i].astype(BF16)
            wr_lo = (w_router[i] - wr_hi.astype(F32)).astype(BF16)
            w_router_pad = jnp.pad(jnp.concatenate([wr_hi, wr_lo], axis=1),
                                   ((0, 0), (0, LANES - 2 * N_EXPERTS)))
        outs = _outproj(o_na, dil, o_mem, h, g_out[layer].reshape(1, D_MODEL), w_out[layer].astype(BF16),
                        g_ffn[layer].reshape(1, D_MODEL), w_router_pad, batch, seq_len)
        if not moe:
            h, u = outs
            h = _dense_ffn(u, h, w_gate_dense[i].astype(BF16), w_up_dense[i].astype(BF16),
                           w_down_dense[i].astype(BF16))
        else:
            h, u, logits = outs
            dest, gates, vend, pend, blk_expert, blk_valid, n_used, n_rows = _route(
                logits[:, :N_EXPERTS] + logits[:, N_EXPERTS:2 * N_EXPERTS])
            x_sorted = _dispatch(u, dest, vend, pend, n_rows)
            y = _moe_ffn(x_sorted, blk_expert, blk_valid, n_used, w_gate_moe[i], w_up_moe[i], w_down_moe[i])
            h = _combine(h, y, dest, gates)
    return h.reshape(batch, seq_len, D_MODEL)
```

```python
import functools

import numpy as np
import jax
import jax.numpy as jnp
from jax import lax
from jax.experimental import pallas as pl
from jax.experimental.pallas import tpu as pltpu

F32 = jnp.float32
BF16 = jnp.bfloat16

D_MODEL = 1024
HEAD_DIM = 64
H_NA, H_DIL, H_MEM = 6, 6, 4
W_NA, W_DIL, W_MEM = H_NA * HEAD_DIM, H_DIL * HEAD_DIM, H_MEM * HEAD_DIM
IN_WIDTH = 3 * W_NA + 3 * W_DIL + W_MEM
GRID_W = 64
NA_KH, NA_KW = 8, 16
DIL_CFG = ((128, 1), (512, 4), (2048, 16))
ROPE_THETA = 10000.0
N_EXPERTS = 8
TOP_K = 2
RMS_EPS = 1e-6
NEG_INF = -1e30

LANES = 128
TM_PROJ = 512
PROJ_CHAINS = 2
NA_QROWS = 4
NA_QB = NA_QROWS * GRID_W
NA_KROWS = 12
NA_KB = NA_KROWS * GRID_W
NA_STEP_BLOCKS = 2
DIL_QB = 128
DIL_HALF = 64
DIL_KB = DIL_QB + 2 * DIL_HALF
DIL_STEP = 1024
TM_FFN = 512
TF_DENSE = 256
TM_MOE = 2048
TF_MOE = 512
MOE_CHUNK = 256
TC_COMBINE = 512
ZERO_ROWS = 256
ISSUE_UNROLL = 8
VMEM_LIMIT = 56 * 1024 * 1024


def _cparams(sem):
    return pltpu.CompilerParams(dimension_semantics=sem, vmem_limit_bytes=VMEM_LIMIT)


def _rms(x, g):
    return x * lax.rsqrt(jnp.mean(x * x, axis=-1, keepdims=True) + RMS_EPS) * g


_CHUNKS = (
    [(0, c, True, False) for c in range(3)] + [(1, c, True, False) for c in range(3)]
    + [(2, c, False, False) for c in range(3)]
    + [(3, c, True, True) for c in range(3)] + [(4, c, True, True) for c in range(3)]
    + [(5, c, False, False) for c in range(3)]
    + [(6, c, True, False) for c in range(2)]
)


DILATIONS = tuple(d for _, d in DIL_CFG)
_RESIDUE_DILS = tuple(d for d in DILATIONS if d > 1)
assert len(_RESIDUE_DILS) == 2 and _RESIDUE_DILS[1] == _RESIDUE_DILS[0] ** 2
_N_FLAT_OUT = 7


def _inproj_kernel(h_ref, ga_ref, w_ref, gq_ref, cos_ref, sin_ref, seg_ref, *rest):
    out_refs, stage_ref, stage2_ref = rest[:-2], rest[-2], rest[-1]
    tm = h_ref.shape[0]
    seg = seg_ref[...]
    lane = lax.broadcasted_iota(jnp.int32, (1, LANES), 1)
    first_half = (lane % HEAD_DIM) < (HEAD_DIM // 2)
    f = _RESIDUE_DILS[0]

    def chain(r0, th):
        rows = slice(r0, r0 + th)
        u = _rms(h_ref[rows, :], ga_ref[...]).astype(BF16)
        p = jnp.dot(u, w_ref[...], preferred_element_type=F32)
        normed = {}
        c = 0
        while c < len(_CHUNKS):
            if _CHUNKS[c][2]:
                assert _CHUNKS[c + 1][2]
                cols = slice(c * LANES, (c + 2) * LANES)
                x2 = p[:, cols]
                ss = jnp.dot((x2 * x2).astype(BF16), seg, preferred_element_type=F32)
                x2 = x2 * lax.rsqrt(ss * (1.0 / HEAD_DIM) + RMS_EPS) * gq_ref[:, cols]
                normed[c], normed[c + 1] = x2[:, :LANES], x2[:, LANES:]
                c += 2
            else:
                normed[c] = p[:, c * LANES:(c + 1) * LANES]
                c += 1
        for c, (oi, oc, norm, rope) in enumerate(_CHUNKS):
            x = normed[c]
            cols = slice(oc * LANES, (oc + 1) * LANES)
            if rope:
                swapped = jnp.where(first_half, pltpu.roll(x, LANES - HEAD_DIM // 2, 1),
                                    pltpu.roll(x, HEAD_DIM // 2, 1))
                x = x * cos_ref[rows, :] + swapped * sin_ref[rows, :]
            out_refs[oi][rows, cols] = x.astype(BF16)
            if 3 <= oi <= 5:
                st = (oi - 3) * (W_DIL // LANES) + oc
                dst1 = out_refs[_N_FLAT_OUT + (oi - 3)]
                dst2 = out_refs[_N_FLAT_OUT + 3 + (oi - 3)]
                stage_ref[st, rows, :] = x
                for r in range(f):
                    xr = stage_ref[st, pl.ds(r0 + r, th // f, stride=f), :]
                    dst1[r, r0 // f:(r0 + th) // f, cols] = xr.astype(BF16)
                    stage2_ref[st * f + r, r0 // f:(r0 + th) // f, :] = xr
                    for r2 in range(f):
                        dst2[r + f * r2, r0 // (f * f):(r0 + th) // (f * f), cols] = stage2_ref[
                            st * f + r, pl.ds(r0 // f + r2, th // (f * f), stride=f), :].astype(BF16)

    th = tm // PROJ_CHAINS
    for ci in range(PROJ_CHAINS):
        chain(ci * th, th)


def _inproj(h, g_attn, w_in_bf16, gq_row, cos_t, sin_t, seg, batch, seq_len):
    n = h.shape[0]
    tm = TM_PROJ
    tblocks = seq_len // tm
    widths = (W_NA, W_NA, W_NA, W_DIL, W_DIL, W_DIL, W_MEM)
    nst, f = 3 * W_DIL // LANES, _RESIDUE_DILS[0]
    out_specs = [pl.BlockSpec((tm, w), lambda i: (i, 0)) for w in widths]
    out_shape = [jax.ShapeDtypeStruct((n, w), BF16) for w in widths]
    for d in _RESIDUE_DILS:
        out_specs += [pl.BlockSpec((None, d, tm // d, W_DIL),
                                   lambda i: (i // tblocks, 0, i % tblocks, 0))] * 3
        out_shape += [jax.ShapeDtypeStruct((batch, d, seq_len // d, W_DIL), BF16)] * 3
    return pl.pallas_call(
        _inproj_kernel,
        grid=(n // tm,),
        in_specs=[
            pl.BlockSpec((tm, D_MODEL), lambda i: (i, 0)),
            pl.BlockSpec((1, D_MODEL), lambda i: (0, 0)),
            pl.BlockSpec((D_MODEL, IN_WIDTH), lambda i: (0, 0)),
            pl.BlockSpec((1, IN_WIDTH), lambda i: (0, 0)),
            pl.BlockSpec((tm, LANES), lambda i: (i % tblocks, 0)),
            pl.BlockSpec((tm, LANES), lambda i: (i % tblocks, 0)),
            pl.BlockSpec((2 * LANES, 2 * LANES), lambda i: (0, 0)),
        ],
        out_specs=out_specs,
        out_shape=out_shape,
        scratch_shapes=[pltpu.VMEM((nst, tm, LANES), F32), pltpu.VMEM((nst * f, tm // f, LANES), F32)],
        compiler_params=_cparams(("parallel",)),
        name="inproj",
    )(h, g_attn, w_in_bf16, gq_row, cos_t, sin_t, seg)


def _pair_attention(q, k, v, bias, want_lse):
    nq = q.shape[0]
    first = lax.broadcasted_iota(jnp.int32, (1, LANES), 1) < HEAD_DIM
    zero = jnp.zeros_like(q)
    q2 = jnp.concatenate([jnp.where(first, q, zero), jnp.where(first, zero, q)], axis=0)
    v_ones = jnp.concatenate([v, jnp.ones_like(v)], axis=1)
    o, lse = _softmax_pv(q2, k, v_ones, bias, want_lse)
    out = jnp.where(first, o[:nq], o[nq:])
    lse_out = jnp.where(first, lse[:nq], lse[nq:]) if want_lse else None
    return out, lse_out


def _softmax_pv(q, k, v_ones, bias, want_lse):
    s = lax.dot_general(q, k, (((1,), (1,)), ((), ())), preferred_element_type=F32)
    if bias is not None:
        s = s + bias
    m = jnp.max(s, axis=-1, keepdims=True)
    p = jnp.exp(s - m).astype(BF16)
    ol = jnp.dot(p, v_ones, preferred_element_type=F32)
    l = ol[:, LANES:]
    return ol[:, :LANES] / l, (m + jnp.log(l)) if want_lse else None


def _na_mem_kernel(q_ref, k_ref, v_ref, *rest, rows):
    bias_refs = rest[:NA_STEP_BLOCKS]
    qm_ref, km_ref, vm_ref, o_ref, om_ref = rest[NA_STEP_BLOCKS:]
    i = pl.program_id(1)
    first = lax.broadcasted_iota(jnp.int32, (1, LANES), 1) < HEAD_DIM
    for sb, bias_ref in enumerate(bias_refs):
        qrows = slice(sb * NA_QB, (sb + 1) * NA_QB)
        srow = jnp.clip((i * NA_STEP_BLOCKS + sb) * NA_QROWS - NA_KH // 2, 0, rows - NA_KROWS)
        start = pl.multiple_of(srow * GRID_W, GRID_W)
        for p in range(W_NA // LANES):
            sl = slice(p * LANES, (p + 1) * LANES)
            k = k_ref[pl.ds(start, NA_KB), sl]
            v = v_ref[pl.ds(start, NA_KB), sl]
            q = q_ref[qrows, sl]
            zero = jnp.zeros_like(q)
            v_ones = jnp.concatenate([v, jnp.ones_like(v)], axis=1)
            o0, _ = _softmax_pv(jnp.where(first, q, zero), k, v_ones, bias_ref[0, 2 * p], False)
            o1, _ = _softmax_pv(jnp.where(first, zero, q), k, v_ones, bias_ref[0, 2 * p + 1], False)
            o_ref[qrows, sl] = jnp.where(first, o0, o1)
        for c in range(W_MEM // LANES):
            sl = slice(c * LANES, (c + 1) * LANES)
            om_ref[qrows, sl], _ = _pair_attention(qm_ref[qrows, sl], km_ref[:, sl], vm_ref[:, sl], None, False)


def _na_mem_attention(q, k, v, bias, qm, km, vm, batch, seq_len, mem_len):
    rows = seq_len // GRID_W
    nblk = seq_len // NA_QB
    step = NA_STEP_BLOCKS * NA_QB
    q3, k3, v3 = (a.reshape(batch, seq_len, W_NA) for a in (q, k, v))

    def bias_spec(sb):
        def index(b, i):
            blk = i * NA_STEP_BLOCKS + sb
            return ((blk > 0).astype(jnp.int32) + (blk == nblk - 1).astype(jnp.int32), 0, 0, 0)
        return pl.BlockSpec((1, H_NA, NA_QB, NA_KB), index)

    o, om = pl.pallas_call(
        functools.partial(_na_mem_kernel, rows=rows),
        grid=(batch, nblk // NA_STEP_BLOCKS),
        in_specs=[
            pl.BlockSpec((None, step, W_NA), lambda b, i: (b, i, 0)),
            pl.BlockSpec((None, seq_len, W_NA), lambda b, i: (b, 0, 0)),
            pl.BlockSpec((None, seq_len, W_NA), lambda b, i: (b, 0, 0)),
            *[bias_spec(sb) for sb in range(NA_STEP_BLOCKS)],
            pl.BlockSpec((None, step, W_MEM), lambda b, i: (b, i, 0)),
            pl.BlockSpec((None, mem_len, W_MEM), lambda b, i: (b, 0, 0)),
            pl.BlockSpec((None, mem_len, W_MEM), lambda b, i: (b, 0, 0)),
        ],
        out_specs=[pl.BlockSpec((None, step, W_NA), lambda b, i: (b, i, 0)),
                   pl.BlockSpec((None, step, W_MEM), lambda b, i: (b, i, 0))],
        out_shape=[jax.ShapeDtypeStruct((batch, seq_len, W_NA), F32),
                   jax.ShapeDtypeStruct((batch, seq_len, W_MEM), F32)],
        compiler_params=_cparams(("parallel", "arbitrary")),
        name="na_mem_attention",
    )(q3, k3, v3, *([bias] * NA_STEP_BLOCKS), qm.reshape(batch, seq_len, W_MEM),
      km.reshape(batch, mem_len, W_MEM), vm.reshape(batch, mem_len, W_MEM))
    return o.reshape(batch * seq_len, W_NA), om.reshape(batch * seq_len, W_MEM)


def _na_bias_tables(rpb, rows):
    nblk = rows // NA_QROWS
    qc = np.arange(GRID_W)[:, None]
    kc = np.arange(GRID_W)[None, :]
    cs = np.clip(qc - NA_KW // 2, 0, GRID_W - NA_KW)
    col_valid = (kc >= cs) & (kc < cs + NA_KW)
    col_rel = kc - qc + (NA_KW - 1)
    col_sel = (col_rel[None] == np.arange(2 * NA_KW - 1)[:, None, None]).astype(np.float32)
    t1 = jnp.einsum('hab,bqk->haqk', rpb, jnp.asarray(col_sel), precision=lax.Precision.HIGHEST)
    t1 = jnp.where(jnp.asarray(col_valid)[None, None], t1, NEG_INF)
    masked = jnp.full((H_NA, GRID_W, GRID_W), NEG_INF, F32)
    tables = []
    for blk in (0, 1, nblk - 1):
        r0 = blk * NA_QROWS
        srow = int(np.clip(r0 - NA_KH // 2, 0, rows - NA_KROWS))
        slabs = []
        for qr in range(r0, r0 + NA_QROWS):
            rs = int(np.clip(qr - NA_KH // 2, 0, rows - NA_KH))
            slabs.append(jnp.concatenate(
                [t1[:, kr - qr + NA_KH - 1] if rs <= kr < rs + NA_KH else masked
                 for kr in range(srow, srow + NA_KROWS)], axis=-1))
        tables.append(jnp.stack(slabs, axis=1))
    return jnp.stack(tables).reshape(3, H_NA, NA_QB, NA_KB)


def _band_kernel(q_ref, k_ref, v_ref, mask_ref, o_ref, lse_ref, *, length):
    j = pl.program_id(1)
    step = q_ref.shape[0]
    for sb in range(step // DIL_QB):
        rows = slice(sb * DIL_QB, (sb + 1) * DIL_QB)
        q0 = j * step + sb * DIL_QB
        start = pl.multiple_of(jnp.clip(q0 - DIL_HALF, 0, length - DIL_KB), DIL_HALF)
        variant = (q0 > 0).astype(jnp.int32) + (q0 == length - DIL_QB).astype(jnp.int32)
        mask = mask_ref[variant]
        mask2 = jnp.concatenate([mask, mask], axis=0)
        for p in range(W_DIL // LANES):
            sl = slice(p * LANES, (p + 1) * LANES)
            k = k_ref[pl.ds(start, DIL_KB), sl]
            v = v_ref[pl.ds(start, DIL_KB), sl]
            o, lse = _pair_attention(q_ref[rows, sl], k, v, mask2, True)
            o_ref[rows, sl] = o
            lse_ref[rows, sl] = lse


def _band_attention(q, k, v, masks):
    nseq, length, _ = q.shape
    step = min(DIL_STEP, length)
    nblk = length // step
    blk = pl.BlockSpec((None, step, W_DIL), lambda s, j: (s, j, 0))
    seq = pl.BlockSpec((None, length, W_DIL), lambda s, j: (s, 0, 0))
    return pl.pallas_call(
        functools.partial(_band_kernel, length=length),
        grid=(nseq, nblk),
        in_specs=[blk, seq, seq, pl.BlockSpec((3, DIL_QB, DIL_KB), lambda s, j: (0, 0, 0))],
        out_specs=[blk, blk],
        out_shape=[jax.ShapeDtypeStruct((nseq, length, W_DIL), F32)] * 2,
        compiler_params=_cparams(("parallel", "arbitrary")),
        name="band_attention",
    )(q, k, v, masks)


def _band_masks():
    qq = np.arange(DIL_QB)[:, None]
    kk = np.arange(DIL_KB)[None, :]
    out = []
    for off in (0, -DIL_HALF, DIL_QB - DIL_KB):
        rel = kk + off - qq
        out.append(np.where(np.abs(rel) <= DIL_HALF, 0.0, NEG_INF))
    return jnp.asarray(np.stack(out), F32)


def _memkv_kernel(mem_ref, g_ref, w_ref, gk_ref, seg_ref, km_ref, vm_ref):
    mn = _rms(mem_ref[...], g_ref[...]).astype(BF16)
    kv = jnp.dot(mn, w_ref[...], preferred_element_type=F32)
    for c in range(W_MEM // LANES):
        x = kv[:, c * LANES:(c + 1) * LANES]
        ss = jnp.dot((x * x).astype(BF16), seg_ref[:LANES, :LANES], preferred_element_type=F32)
        x = x * lax.rsqrt(ss * (1.0 / HEAD_DIM) + RMS_EPS) * gk_ref[...]
        km_ref[:, c * LANES:(c + 1) * LANES] = x.astype(BF16)
    vm_ref[...] = kv[:, W_MEM:].astype(BF16)


def _memkv(mem2, g_mem, w_kv_bf16, gk_row, seg):
    rows = mem2.shape[0]
    full = lambda shape: pl.BlockSpec(shape, lambda i: (0,) * len(shape))
    return pl.pallas_call(
        _memkv_kernel,
        grid=(1,),
        in_specs=[full((rows, D_MODEL)), full((1, D_MODEL)), full((D_MODEL, 2 * W_MEM)),
                  full((1, LANES)), full((2 * LANES, 2 * LANES))],
        out_specs=[full((rows, W_MEM)), full((rows, W_MEM))],
        out_shape=[jax.ShapeDtypeStruct((rows, W_MEM), BF16)] * 2,
        compiler_params=_cparams(("arbitrary",)),
        name="mem_kv",
    )(mem2, g_mem, w_kv_bf16, gk_row, seg)


def _outproj_kernel(ona_ref, *rest, with_router):
    nd = 2 * len(DILATIONS)
    dil_refs, rest = rest[:nd], rest[nd:]
    om_ref, h_ref, go_ref, w_ref, gf_ref = rest[:5]
    rest = rest[5:]
    if with_router:
        wr_ref, hn_ref, u_ref, lg_ref = rest[:4]
        stage_refs = rest[4:]
    else:
        hn_ref, u_ref = rest[:2]
        stage_refs = rest[2:]
    stage_refs, tmp_ref = stage_refs[:-1], stage_refs[-1]
    tm = h_ref.shape[0]
    f = _RESIDUE_DILS[0]
    nc = W_DIL // LANES
    go = go_ref[...]

    def chain(r0, th):
        rows = slice(r0, r0 + th)
        vals = []
        si = 0
        for di, d in enumerate(DILATIONS):
            for ref in dil_refs[2 * di:2 * di + 2]:
                if d == 1:
                    vals.append(ref[rows, :])
                    continue
                st = stage_refs[si]
                si += 1
                for c in range(nc):
                    cols = slice(c * LANES, (c + 1) * LANES)
                    for r in range(f):
                        if d == f:
                            st[c, pl.ds(r0 + r, th // f, stride=f), :] = ref[r, r0 // f:(r0 + th) // f, cols]
                        else:
                            for r2 in range(f):
                                tmp_ref[c * f + r, pl.ds(r0 // f + r2, th // d, stride=f), :] = ref[
                                    r + f * r2, r0 // d:(r0 + th) // d, cols]
                            st[c, pl.ds(r0 + r, th // f, stride=f), :] = tmp_ref[
                                c * f + r, r0 // f:(r0 + th) // f, :]
                vals.append(jnp.concatenate([st[c, rows, :] for c in range(nc)], axis=-1))
        os_, ls_ = vals[0::2], vals[1::2]
        mx = functools.reduce(jnp.maximum, ls_)
        es = [jnp.exp(l - mx) for l in ls_]
        odil = sum(e * o for e, o in zip(es, os_)) / sum(es)
        mixed = jnp.concatenate([
            _rms(ona_ref[rows, :], go[:, :W_NA]),
            _rms(odil, go[:, W_NA:W_NA + W_DIL]),
            _rms(om_ref[rows, :], go[:, W_NA + W_DIL:]),
        ], axis=-1).astype(BF16)
        hn = h_ref[rows, :] + jnp.dot(mixed, w_ref[...], preferred_element_type=F32)
        hn_ref[rows, :] = hn
        u = _rms(hn, gf_ref[...])
        if with_router:
            u_ref[rows, :] = u
            u_hi = u.astype(BF16)
            u_lo = (u - u_hi.astype(F32)).astype(BF16)
            lg_ref[rows, :] = (jnp.dot(u_hi, wr_ref[...], preferred_element_type=F32)
                               + jnp.dot(u_lo, wr_ref[...], preferred_element_type=F32))
        else:
            u_ref[rows, :] = u.astype(BF16)

    th = tm // PROJ_CHAINS
    for ci in range(PROJ_CHAINS):
        chain(ci * th, th)


def _outproj(ona, dil, om, h, g_out, w_out_bf16, g_ffn, w_router_pad, batch, seq_len):
    n = h.shape[0]
    tm = TM_PROJ
    tblocks = seq_len // tm
    with_router = w_router_pad is not None
    row = lambda w: pl.BlockSpec((tm, w), lambda i: (i, 0))
    full = lambda shape: pl.BlockSpec(shape, lambda i: (0,) * len(shape))
    in_specs = [row(W_NA)]
    args = [ona]
    scratch = []
    for di, d in enumerate(DILATIONS):
        for a in dil[2 * di:2 * di + 2]:
            if d == 1:
                in_specs.append(row(W_DIL))
                args.append(a.reshape(n, W_DIL))
            else:
                in_specs.append(pl.BlockSpec((None, d, tm // d, W_DIL),
                                             lambda i: (i // tblocks, 0, i % tblocks, 0)))
                args.append(a.reshape(batch, d, seq_len // d, W_DIL))
                scratch.append(pltpu.VMEM((W_DIL // LANES, tm, LANES), F32))
    f = _RESIDUE_DILS[0]
    scratch.append(pltpu.VMEM((W_DIL // LANES * f, tm // f, LANES), F32))
    in_specs += [row(W_MEM), row(D_MODEL), full((1, D_MODEL)), full((D_MODEL, D_MODEL)),
                 full((1, D_MODEL))]
    args += [om, h, g_out, w_out_bf16, g_ffn]
    out_specs = [row(D_MODEL), row(D_MODEL)]
    out_shape = [jax.ShapeDtypeStruct((n, D_MODEL), F32),
                 jax.ShapeDtypeStruct((n, D_MODEL), F32 if with_router else BF16)]
    if with_router:
        in_specs.append(full((D_MODEL, LANES)))
        args.append(w_router_pad)
        out_specs.append(row(LANES))
        out_shape.append(jax.ShapeDtypeStruct((n, LANES), F32))
    return pl.pallas_call(
        functools.partial(_outproj_kernel, with_router=with_router),
        grid=(n // tm,),
        in_specs=in_specs,
        out_specs=out_specs,
        out_shape=out_shape,
        scratch_shapes=scratch,
        compiler_params=_cparams(("parallel",)),
        name="outproj",
    )(*args)


def _swiglu(x, wg, wu, wd):
    g = jnp.dot(x, wg, preferred_element_type=F32)
    u = jnp.dot(x, wu, preferred_element_type=F32)
    hm = (g * jax.nn.sigmoid(g) * u).astype(BF16)
    return jnp.dot(hm, wd, preferred_element_type=F32)


def _dense_ffn_kernel(u_ref, h_ref, wg_ref, wu_ref, wd_ref, o_ref, hm_ref):
    x = u_ref[...]
    for c in range(hm_ref.shape[1] // TF_DENSE):
        cols = slice(c * TF_DENSE, (c + 1) * TF_DENSE)
        g = jnp.dot(x, wg_ref[:, cols], preferred_element_type=F32)
        u = jnp.dot(x, wu_ref[:, cols], preferred_element_type=F32)
        hm_ref[:, cols] = (g * jax.nn.sigmoid(g) * u).astype(BF16)
    o_ref[...] = h_ref[...] + jnp.dot(hm_ref[...], wd_ref[...], preferred_element_type=F32)


def _dense_ffn(u, h, wg_bf16, wu_bf16, wd_bf16):
    n = h.shape[0]
    d_ff = wg_bf16.shape[1]
    tm = TM_FFN
    full = lambda shape: pl.BlockSpec(shape, lambda i: (0,) * len(shape))
    return pl.pallas_call(
        _dense_ffn_kernel,
        grid=(n // tm,),
        in_specs=[
            pl.BlockSpec((tm, D_MODEL), lambda i: (i, 0)),
            pl.BlockSpec((tm, D_MODEL), lambda i: (i, 0)),
            full((D_MODEL, d_ff)), full((D_MODEL, d_ff)), full((d_ff, D_MODEL)),
        ],
        out_specs=pl.BlockSpec((tm, D_MODEL), lambda i: (i, 0)),
        out_shape=jax.ShapeDtypeStruct((n, D_MODEL), F32),
        scratch_shapes=[pltpu.VMEM((tm, d_ff), BF16)],
        compiler_params=_cparams(("parallel",)),
        name="dense_ffn",
    )(u, h, wg_bf16, wu_bf16, wd_bf16)


def _dispatch_kernel(vend_ref, pend_ref, d0_ref, d1_ref, u_ref, x_hbm, zbuf, sem, zsem):
    i = pl.program_id(0)
    tc = u_ref.shape[0]
    zr = zbuf.shape[0]
    n_rows = x_hbm.shape[0]

    def zero_pieces(lo, hi):
        def start(p, carry):
            pltpu.make_async_copy(zbuf, x_hbm.at[pl.ds(pl.multiple_of(p * zr, zr), zr), :], zsem).start()
            return carry

        def wait(p, carry):
            pltpu.make_async_copy(zbuf, x_hbm.at[pl.ds(0, zr), :], zsem).wait()
            return carry

        lax.fori_loop(lo, hi, start, 0)
        lax.fori_loop(lo, hi, wait, 0)

    @pl.when(i == 0)
    def _():
        zbuf[...] = jnp.zeros_like(zbuf)
        for e in range(N_EXPERTS):
            zero_pieces(vend_ref[e] // zr, pend_ref[e] // zr)
        zero_pieces(pend_ref[N_EXPERTS - 1] // zr, n_rows // zr)

    def issue(r, carry):
        src = u_ref.at[pl.ds(r, 1), :]
        pltpu.make_async_copy(src, x_hbm.at[pl.ds(d0_ref[0, 0, r], 1), :], sem).start(priority=0)
        pltpu.make_async_copy(src, x_hbm.at[pl.ds(d1_ref[0, 0, r], 1), :], sem).start(priority=1)
        return carry

    lax.fori_loop(0, tc, issue, 0, unroll=ISSUE_UNROLL)
    for _ in range(TOP_K):
        pltpu.make_async_copy(u_ref, x_hbm.at[pl.ds(0, tc), :], sem).wait()


def _dispatch(u, dest, vend, pend, n_rows):
    n = u.shape[0]
    tc = TC_COMBINE
    nblk = n // tc
    d0 = dest[:, 0].reshape(nblk, 1, tc)
    d1 = dest[:, 1].reshape(nblk, 1, tc)
    smem = pl.BlockSpec((1, 1, tc), lambda i, ve, pe: (i, 0, 0), memory_space=pltpu.SMEM)
    grid_spec = pltpu.PrefetchScalarGridSpec(
        num_scalar_prefetch=2,
        grid=(nblk,),
        in_specs=[smem, smem, pl.BlockSpec((tc, D_MODEL), lambda i, ve, pe: (i, 0))],
        out_specs=pl.BlockSpec(memory_space=pl.ANY),
        scratch_shapes=[pltpu.VMEM((ZERO_ROWS, D_MODEL), F32), pltpu.SemaphoreType.DMA(()),
                        pltpu.SemaphoreType.DMA(())],
    )
    return pl.pallas_call(
        _dispatch_kernel,
        grid_spec=grid_spec,
        out_shape=jax.ShapeDtypeStruct((n_rows, D_MODEL), F32),
        compiler_params=_cparams(("arbitrary",)),
        name="moe_dispatch",
    )(vend, pend, d0, d1, u)


def _moe_ffn_kernel(be_ref, nv_ref, nu_ref, x_ref, wg_ref, wu_ref, wd_ref, y_ref, xb):
    i = pl.program_id(0)
    j = pl.program_id(1)
    active = i < nu_ref[0]
    tm = x_ref.shape[0]
    nchunk = tm // MOE_CHUNK

    def chunk(c, wg, wu, wd):
        rows = slice(c * MOE_CHUNK, (c + 1) * MOE_CHUNK)
        y_ref[rows, :] += _swiglu(xb[rows, :], wg, wu, wd)

    @pl.when(j == 0)
    def _():
        y_ref[...] = jnp.zeros_like(y_ref)

    @pl.when(active)
    def _():
        @pl.when(j == 0)
        def _():
            xb[...] = x_ref[...].astype(BF16)

        nvalid = nv_ref[i]

        @pl.when(nvalid > tm - MOE_CHUNK)
        def _():
            wg, wu, wd = (r[...].astype(BF16) for r in (wg_ref, wu_ref, wd_ref))
            for c in range(nchunk):
                chunk(c, wg, wu, wd)

        @pl.when(nvalid <= tm - MOE_CHUNK)
        def _():
            wg, wu, wd = (r[...].astype(BF16) for r in (wg_ref, wu_ref, wd_ref))
            for c in range(nchunk):
                @pl.when(c * MOE_CHUNK < nvalid)
                def _():
                    chunk(c, wg, wu, wd)


def _moe_ffn(x_sorted, blk_expert, blk_valid, n_used, wg, wu, wd):
    n_rows = x_sorted.shape[0]
    tm, tf = TM_MOE, TF_MOE
    nblk = n_rows // tm
    d_ff = wg.shape[2]
    nj = d_ff // tf

    def jeff(i, j, nu):
        return jnp.where(i < nu[0], j, nj - 1)

    grid_spec = pltpu.PrefetchScalarGridSpec(
        num_scalar_prefetch=3,
        grid=(nblk, nj),
        in_specs=[
            pl.BlockSpec((tm, D_MODEL), lambda i, j, be, nv, nu: (jnp.minimum(i, nu[0] - 1), 0)),
            pl.BlockSpec((None, D_MODEL, tf), lambda i, j, be, nv, nu: (be[i], 0, jeff(i, j, nu))),
            pl.BlockSpec((None, D_MODEL, tf), lambda i, j, be, nv, nu: (be[i], 0, jeff(i, j, nu))),
            pl.BlockSpec((None, tf, D_MODEL), lambda i, j, be, nv, nu: (be[i], jeff(i, j, nu), 0)),
        ],
        out_specs=pl.BlockSpec((tm, D_MODEL), lambda i, j, be, nv, nu: (i, 0)),
        scratch_shapes=[pltpu.VMEM((tm, D_MODEL), BF16)],
    )
    return pl.pallas_call(
        _moe_ffn_kernel,
        grid_spec=grid_spec,
        out_shape=jax.ShapeDtypeStruct((n_rows, D_MODEL), F32),
        compiler_params=_cparams(("arbitrary", "arbitrary")),
        name="moe_ffn",
    )(blk_expert, blk_valid, n_used, x_sorted, wg, wu, wd)


def _combine_kernel(p0_ref, p1_ref, h_ref, g_ref, y_hbm, o_ref, buf0, buf1, sem):
    tc = buf0.shape[0]

    def issue(r, carry):
        pltpu.make_async_copy(y_hbm.at[pl.ds(p0_ref[0, 0, r], 1), :], buf0.at[pl.ds(r, 1), :],
                              sem.at[0]).start(priority=0)
        pltpu.make_async_copy(y_hbm.at[pl.ds(p1_ref[0, 0, r], 1), :], buf1.at[pl.ds(r, 1), :],
                              sem.at[1]).start(priority=1)
        return carry

    lax.fori_loop(0, tc, issue, 0, unroll=ISSUE_UNROLL)
    pltpu.make_async_copy(y_hbm.at[pl.ds(0, tc), :], buf0, sem.at[0]).wait()
    pltpu.make_async_copy(y_hbm.at[pl.ds(0, tc), :], buf1, sem.at[1]).wait()
    g = g_ref[...]
    o_ref[...] = h_ref[...] + g[:, 0:1] * buf0[...] + g[:, 1:2] * buf1[...]


def _combine(h, y, dest, gates):
    n = h.shape[0]
    tc = TC_COMBINE
    nblk = n // tc
    p0 = dest[:, 0].reshape(nblk, 1, tc)
    p1 = dest[:, 1].reshape(nblk, 1, tc)
    smem = pl.BlockSpec((1, 1, tc), lambda i: (i, 0, 0), memory_space=pltpu.SMEM)
    return pl.pallas_call(
        _combine_kernel,
        grid=(nblk,),
        in_specs=[smem, smem, pl.BlockSpec((tc, D_MODEL), lambda i: (i, 0)),
                  pl.BlockSpec((tc, TOP_K), lambda i: (i, 0)), pl.BlockSpec(memory_space=pl.ANY)],
        out_specs=pl.BlockSpec((tc, D_MODEL), lambda i: (i, 0)),
        out_shape=jax.ShapeDtypeStruct((n, D_MODEL), F32),
        scratch_shapes=[pltpu.VMEM((tc, D_MODEL), F32), pltpu.VMEM((tc, D_MODEL), F32),
                        pltpu.SemaphoreType.DMA((2,))],
        compiler_params=_cparams(("arbitrary",)),
        name="moe_combine",
    )(p0, p1, h, gates, y)


def _route(logits):
    n = logits.shape[0]
    top_logit, top_idx = lax.top_k(logits, TOP_K)
    gates = jax.nn.softmax(top_logit, axis=-1)
    e_flat = top_idx.reshape(n * TOP_K).astype(jnp.int32)
    onehot = (e_flat[:, None] == jnp.arange(N_EXPERTS, dtype=jnp.int32)[None, :]).astype(jnp.int32)
    csum = jnp.cumsum(onehot, axis=0)
    counts = csum[-1]
    padded = (counts + TM_MOE - 1) // TM_MOE * TM_MOE
    pend = jnp.cumsum(padded)
    pstart = pend - padded
    dest = jnp.sum(onehot * (csum - 1 + pstart[None, :]), axis=1).reshape(n, TOP_K)
    n_rows = n * TOP_K + N_EXPERTS * TM_MOE
    nblk = n_rows // TM_MOE
    n_used = pend[-1] // TM_MOE
    blk_start = jnp.minimum(jnp.arange(nblk, dtype=jnp.int32) * TM_MOE, (n_used - 1) * TM_MOE)
    blk_expert = jnp.minimum(
        jnp.sum((blk_start[:, None] >= pend[None, :]).astype(jnp.int32), axis=1), N_EXPERTS - 1)
    blk_valid = jnp.clip(counts[blk_expert] - (blk_start - pstart[blk_expert]), 0, TM_MOE)
    i32 = lambda a: a.astype(jnp.int32)
    return (i32(dest), gates, i32(pstart + counts), i32(pend), i32(blk_expert), i32(blk_valid),
            i32(n_used).reshape(1), n_rows)


def _rope_tables(seq_len):
    half = HEAD_DIM // 2
    inv_freq = jnp.power(ROPE_THETA, -(2.0 / HEAD_DIM) * jnp.arange(half, dtype=F32))
    ang = jnp.arange(seq_len, dtype=F32)[:, None] * inv_freq[None, :]
    cos, sin = jnp.cos(ang), jnp.sin(ang)
    cos_t = jnp.tile(cos, (1, LANES // half))
    sin_t = jnp.tile(jnp.concatenate([-sin, sin], axis=1), (1, LANES // HEAD_DIM))
    return cos_t, sin_t


def kernel(x, mem, g_attn, w_in, g_qk_na, rpb_na, g_qk_dil, g_mem, w_mem_kv, g_qk_mem, g_out, w_out,
           g_ffn, w_gate_dense, w_up_dense, w_down_dense, w_router, w_gate_moe, w_up_moe, w_down_moe):
    batch, seq_len, _ = x.shape
    mem_len = mem.shape[1]
    depth = g_attn.shape[0]
    n = batch * seq_len
    scale = HEAD_DIM ** -0.5

    cos_t, sin_t = _rope_tables(seq_len)
    seg = jnp.asarray(np.kron(np.eye(2 * LANES // HEAD_DIM), np.ones((HEAD_DIM, HEAD_DIM))), BF16)
    band_masks = _band_masks()
    ones = lambda w: jnp.ones((w,), F32)

    h = x.reshape(n, D_MODEL)
    mem2 = mem.reshape(batch * mem_len, D_MODEL)
    for layer in range(depth):
        gq_row = jnp.concatenate([
            jnp.tile(g_qk_na[layer, 0] * scale, H_NA), jnp.tile(g_qk_na[layer, 1], H_NA), ones(W_NA),
            jnp.tile(g_qk_dil[layer, 0] * scale, H_DIL), jnp.tile(g_qk_dil[layer, 1], H_DIL), ones(W_DIL),
            jnp.tile(g_qk_mem[layer, 0] * scale, H_MEM)]).reshape(1, IN_WIDTH)
        proj = _inproj(h, g_attn[layer].reshape(1, D_MODEL), w_in[layer].astype(BF16), gq_row, cos_t, sin_t,
                       seg, batch, seq_len)
        qa, ka, va, qb, kb, vb, qm = proj[:_N_FLAT_OUT]

        dil = []
        for d in DILATIONS:
            if d == 1:
                qkv = [a.reshape(batch, seq_len, W_DIL) for a in (qb, kb, vb)]
            else:
                base = _N_FLAT_OUT + 3 * _RESIDUE_DILS.index(d)
                qkv = [a.reshape(batch * d, seq_len // d, W_DIL) for a in proj[base:base + 3]]
            dil += _band_attention(*qkv, band_masks)

        km, vm = _memkv(mem2, g_mem[layer].reshape(1, D_MODEL), w_mem_kv[layer].astype(BF16),
                        jnp.tile(g_qk_mem[layer, 1], LANES // HEAD_DIM).reshape(1, LANES), seg)
        o_na, o_mem = _na_mem_attention(qa, ka, va, _na_bias_tables(rpb_na[layer], seq_len // GRID_W),
                                        qm, km, vm, batch, seq_len, mem_len)

        i = layer // 2
        moe = layer % 2 == 1
        w_router_pad = None
        if moe:
            wr_hi = w_router[i].astype(BF16)
            wr_lo = (w_router[i] - wr_hi.astype(F32)).astype(BF16)
            w_router_pad = jnp.pad(jnp.concatenate([wr_hi, wr_lo], axis=1),
                                   ((0, 0), (0, LANES - 2 * N_EXPERTS)))
        outs = _outproj(o_na, dil, o_mem, h, g_out[layer].reshape(1, D_MODEL), w_out[layer].astype(BF16),
                        g_ffn[layer].reshape(1, D_MODEL), w_router_pad, batch, seq_len)
        if not moe:
            h, u = outs
            h = _dense_ffn(u, h, w_gate_dense[i].astype(BF16), w_up_dense[i].astype(BF16),
                           w_down_dense[i].astype(BF16))
        else:
            h, u, logits = outs
            dest, gates, vend, pend, blk_expert, blk_valid, n_used, n_rows = _route(
                logits[:, :N_EXPERTS] + logits[:, N_EXPERTS:2 * N_EXPERTS])
            x_sorted = _dispatch(u, dest, vend, pend, n_rows)
            y = _moe_ffn(x_sorted, blk_expert, blk_valid, n_used, w_gate_moe[i], w_up_moe[i], w_down_moe[i])
            h = _combine(h, y, dest, gates)
    return h.reshape(batch, seq_len, D_MODEL)
```

```python
import functools

import numpy as np
import jax
import jax.numpy as jnp
from jax import lax
from jax.experimental import pallas as pl
from jax.experimental.pallas import tpu as pltpu

F32 = jnp.float32
BF16 = jnp.bfloat16

D_MODEL = 1024
HEAD_DIM = 64
H_NA, H_DIL, H_MEM = 6, 6, 4
W_NA, W_DIL, W_MEM = H_NA * HEAD_DIM, H_DIL * HEAD_DIM, H_MEM * HEAD_DIM
IN_WIDTH = 3 * W_NA + 3 * W_DIL + W_MEM
GRID_W = 64
NA_KH, NA_KW = 8, 16
DIL_CFG = ((128, 1), (512, 4), (2048, 16))
ROPE_THETA = 10000.0
N_EXPERTS = 8
TOP_K = 2
RMS_EPS = 1e-6
NEG_INF = -1e30

LANES = 128
TM_PROJ = 512
PROJ_CHAINS = 2
NA_QROWS = 4
NA_QB = NA_QROWS * GRID_W
NA_KROWS = 12
NA_KB = NA_KROWS * GRID_W
NA_STEP_BLOCKS = 2
DIL_QB = 128
DIL_HALF = 64
DIL_KB = DIL_QB + 2 * DIL_HALF
DIL_STEP = 1024
TM_FFN = 512
TF_DENSE = 256
TM_MOE = 2048
TF_MOE = 512
MOE_CHUNK = 256
TC_COMBINE = 512
ZERO_ROWS = 256
ISSUE_UNROLL = 8
VMEM_LIMIT = 56 * 1024 * 1024


def _cparams(sem):
    return pltpu.CompilerParams(dimension_semantics=sem, vmem_limit_bytes=VMEM_LIMIT)


def _rms(x, g):
    return x * lax.rsqrt(jnp.mean(x * x, axis=-1, keepdims=True) + RMS_EPS) * g


_CHUNKS = (
    [(0, c, True, False) for c in range(3)] + [(1, c, True, False) for c in range(3)]
    + [(2, c, False, False) for c in range(3)]
    + [(3, c, True, True) for c in range(3)] + [(4, c, True, True) for c in range(3)]
    + [(5, c, False, False) for c in range(3)]
    + [(6, c, True, False) for c in range(2)]
)


DILATIONS = tuple(d for _, d in DIL_CFG)
_RESIDUE_DILS = tuple(d for d in DILATIONS if d > 1)
assert len(_RESIDUE_DILS) == 2 and _RESIDUE_DILS[1] == _RESIDUE_DILS[0] ** 2
_N_FLAT_OUT = 7


def _inproj_kernel(h_ref, ga_ref, w_ref, gq_ref, cos_ref, sin_ref, seg_ref, *rest):
    out_refs, stage_ref, stage2_ref = rest[:-2], rest[-2], rest[-1]
    tm = h_ref.shape[0]
    seg = seg_ref[...]
    lane = lax.broadcasted_iota(jnp.int32, (1, LANES), 1)
    first_half = (lane % HEAD_DIM) < (HEAD_DIM // 2)
    f = _RESIDUE_DILS[0]

    def chain(r0, th):
        rows = slice(r0, r0 + th)
        u = _rms(h_ref[rows, :], ga_ref[...]).astype(BF16)
        p = jnp.dot(u, w_ref[...], preferred_element_type=F32)
        normed = {}
        c = 0
        while c < len(_CHUNKS):
            if _CHUNKS[c][2]:
                assert _CHUNKS[c + 1][2]
                cols = slice(c * LANES, (c + 2) * LANES)
                x2 = p[:, cols]
                ss = jnp.dot((x2 * x2).astype(BF16), seg, preferred_element_type=F32)
                x2 = x2 * lax.rsqrt(ss * (1.0 / HEAD_DIM) + RMS_EPS) * gq_ref[:, cols]
                normed[c], normed[c + 1] = x2[:, :LANES], x2[:, LANES:]
                c += 2
            else:
                normed[c] = p[:, c * LANES:(c + 1) * LANES]
                c += 1
        for c, (oi, oc, norm, rope) in enumerate(_CHUNKS):
            x = normed[c]
            cols = slice(oc * LANES, (oc + 1) * LANES)
            if rope:
                swapped = jnp.where(first_half, pltpu.roll(x, LANES - HEAD_DIM // 2, 1),
                                    pltpu.roll(x, HEAD_DIM // 2, 1))
                x = x * cos_ref[rows, :] + swapped * sin_ref[rows, :]
            out_refs[oi][rows, cols] = x.astype(BF16)
            if 3 <= oi <= 5:
                st = (oi - 3) * (W_DIL // LANES) + oc
                dst1 = out_refs[_N_FLAT_OUT + (oi - 3)]
                dst2 = out_refs[_N_FLAT_OUT + 3 + (oi - 3)]
                stage_ref[st, rows, :] = x
                for r in range(f):
                    xr = stage_ref[st, pl.ds(r0 + r, th // f, stride=f), :]
                    dst1[r, r0 // f:(r0 + th) // f, cols] = xr.astype(BF16)
                    stage2_ref[st * f + r, r0 // f:(r0 + th) // f, :] = xr
                    for r2 in range(f):
                        dst2[r + f * r2, r0 // (f * f):(r0 + th) // (f * f), cols] = stage2_ref[
                            st * f + r, pl.ds(r0 // f + r2, th // (f * f), stride=f), :].astype(BF16)

    th = tm // PROJ_CHAINS
    for ci in range(PROJ_CHAINS):
        chain(ci * th, th)


def _inproj(h, g_attn, w_in_bf16, gq_row, cos_t, sin_t, seg, batch, seq_len):
    n = h.shape[0]
    tm = TM_PROJ
    tblocks = seq_len // tm
    widths = (W_NA, W_NA, W_NA, W_DIL, W_DIL, W_DIL, W_MEM)
    nst, f = 3 * W_DIL // LANES, _RESIDUE_DILS[0]
    out_specs = [pl.BlockSpec((tm, w), lambda i: (i, 0)) for w in widths]
    out_shape = [jax.ShapeDtypeStruct((n, w), BF16) for w in widths]
    for d in _RESIDUE_DILS:
        out_specs += [pl.BlockSpec((None, d, tm // d, W_DIL),
                                   lambda i: (i // tblocks, 0, i % tblocks, 0))] * 3
        out_shape += [jax.ShapeDtypeStruct((batch, d, seq_len // d, W_DIL), BF16)] * 3
    return pl.pallas_call(
        _inproj_kernel,
        grid=(n // tm,),
        in_specs=[
            pl.BlockSpec((tm, D_MODEL), lambda i: (i, 0)),
            pl.BlockSpec((1, D_MODEL), lambda i: (0, 0)),
            pl.BlockSpec((D_MODEL, IN_WIDTH), lambda i: (0, 0)),
            pl.BlockSpec((1, IN_WIDTH), lambda i: (0, 0)),
            pl.BlockSpec((tm, LANES), lambda i: (i % tblocks, 0)),
            pl.BlockSpec((tm, LANES), lambda i: (i % tblocks, 0)),
            pl.BlockSpec((2 * LANES, 2 * LANES), lambda i: (0, 0)),
        ],
        out_specs=out_specs,
        out_shape=out_shape,
        scratch_shapes=[pltpu.VMEM((nst, tm, LANES), F32), pltpu.VMEM((nst * f, tm // f, LANES), F32)],
        compiler_params=_cparams(("parallel",)),
        name="inproj",
    )(h, g_attn, w_in_bf16, gq_row, cos_t, sin_t, seg)


def _pair_attention(q, k, v, bias, want_lse):
    nq = q.shape[0]
    first = lax.broadcasted_iota(jnp.int32, (1, LANES), 1) < HEAD_DIM
    zero = jnp.zeros_like(q)
    q2 = jnp.concatenate([jnp.where(first, q, zero), jnp.where(first, zero, q)], axis=0)
    v_ones = jnp.concatenate([v, jnp.ones_like(v)], axis=1)
    o, lse = _softmax_pv(q2, k, v_ones, bias, want_lse)
    out = jnp.where(first, o[:nq], o[nq:])
    return out, ((lse[:nq], lse[nq:]) if want_lse else None)


def _softmax_pv(q, k, v_ones, bias, want_lse):
    s = lax.dot_general(q, k, (((1,), (1,)), ((), ())), preferred_element_type=F32)
    if bias is not None:
        s = s + bias
    m = jnp.max(s, axis=-1, keepdims=True)
    p = jnp.exp(s - m).astype(BF16)
    ol = jnp.dot(p, v_ones, preferred_element_type=F32)
    l = ol[:, LANES:]
    return ol[:, :LANES] / l, (m + jnp.log(l)) if want_lse else None


def _na_mem_kernel(q_ref, k_ref, v_ref, *rest, rows):
    bias_refs = rest[:NA_STEP_BLOCKS]
    qm_ref, km_ref, vm_ref, o_ref, om_ref = rest[NA_STEP_BLOCKS:]
    i = pl.program_id(1)
    first = lax.broadcasted_iota(jnp.int32, (1, LANES), 1) < HEAD_DIM
    for sb, bias_ref in enumerate(bias_refs):
        qrows = slice(sb * NA_QB, (sb + 1) * NA_QB)
        srow = jnp.clip((i * NA_STEP_BLOCKS + sb) * NA_QROWS - NA_KH // 2, 0, rows - NA_KROWS)
        start = pl.multiple_of(srow * GRID_W, GRID_W)
        for p in range(W_NA // LANES):
            sl = slice(p * LANES, (p + 1) * LANES)
            k = k_ref[pl.ds(start, NA_KB), sl]
            v = v_ref[pl.ds(start, NA_KB), sl]
            q = q_ref[qrows, sl]
            zero = jnp.zeros_like(q)
            v_ones = jnp.concatenate([v, jnp.ones_like(v)], axis=1)
            o0, _ = _softmax_pv(jnp.where(first, q, zero), k, v_ones, bias_ref[0, 2 * p], False)
            o1, _ = _softmax_pv(jnp.where(first, zero, q), k, v_ones, bias_ref[0, 2 * p + 1], False)
            o_ref[qrows, sl] = jnp.where(first, o0, o1).astype(BF16)
        for c in range(W_MEM // LANES):
            sl = slice(c * LANES, (c + 1) * LANES)
            om, _ = _pair_attention(qm_ref[qrows, sl], km_ref[:, sl], vm_ref[:, sl], None, False)
            om_ref[qrows, sl] = om.astype(BF16)


def _na_mem_attention(q, k, v, bias, qm, km, vm, batch, seq_len, mem_len):
    rows = seq_len // GRID_W
    nblk = seq_len // NA_QB
    step = NA_STEP_BLOCKS * NA_QB
    q3, k3, v3 = (a.reshape(batch, seq_len, W_NA) for a in (q, k, v))

    def bias_spec(sb):
        def index(b, i):
            blk = i * NA_STEP_BLOCKS + sb
            return ((blk > 0).astype(jnp.int32) + (blk == nblk - 1).astype(jnp.int32), 0, 0, 0)
        return pl.BlockSpec((1, H_NA, NA_QB, NA_KB), index)

    o, om = pl.pallas_call(
        functools.partial(_na_mem_kernel, rows=rows),
        grid=(batch, nblk // NA_STEP_BLOCKS),
        in_specs=[
            pl.BlockSpec((None, step, W_NA), lambda b, i: (b, i, 0)),
            pl.BlockSpec((None, seq_len, W_NA), lambda b, i: (b, 0, 0)),
            pl.BlockSpec((None, seq_len, W_NA), lambda b, i: (b, 0, 0)),
            *[bias_spec(sb) for sb in range(NA_STEP_BLOCKS)],
            pl.BlockSpec((None, step, W_MEM), lambda b, i: (b, i, 0)),
            pl.BlockSpec((None, mem_len, W_MEM), lambda b, i: (b, 0, 0)),
            pl.BlockSpec((None, mem_len, W_MEM), lambda b, i: (b, 0, 0)),
        ],
        out_specs=[pl.BlockSpec((None, step, W_NA), lambda b, i: (b, i, 0)),
                   pl.BlockSpec((None, step, W_MEM), lambda b, i: (b, i, 0))],
        out_shape=[jax.ShapeDtypeStruct((batch, seq_len, W_NA), BF16),
                   jax.ShapeDtypeStruct((batch, seq_len, W_MEM), BF16)],
        compiler_params=_cparams(("parallel", "arbitrary")),
        name="na_mem_attention",
    )(q3, k3, v3, *([bias] * NA_STEP_BLOCKS), qm.reshape(batch, seq_len, W_MEM),
      km.reshape(batch, mem_len, W_MEM), vm.reshape(batch, mem_len, W_MEM))
    return o.reshape(batch * seq_len, W_NA), om.reshape(batch * seq_len, W_MEM)


def _na_bias_tables(rpb, rows):
    nblk = rows // NA_QROWS
    qc = np.arange(GRID_W)[:, None]
    kc = np.arange(GRID_W)[None, :]
    cs = np.clip(qc - NA_KW // 2, 0, GRID_W - NA_KW)
    col_valid = (kc >= cs) & (kc < cs + NA_KW)
    col_rel = kc - qc + (NA_KW - 1)
    col_sel = (col_rel[None] == np.arange(2 * NA_KW - 1)[:, None, None]).astype(np.float32)
    t1 = jnp.einsum('hab,bqk->haqk', rpb, jnp.asarray(col_sel), precision=lax.Precision.HIGHEST)
    t1 = jnp.where(jnp.asarray(col_valid)[None, None], t1, NEG_INF)
    masked = jnp.full((H_NA, GRID_W, GRID_W), NEG_INF, F32)
    tables = []
    for blk in (0, 1, nblk - 1):
        r0 = blk * NA_QROWS
        srow = int(np.clip(r0 - NA_KH // 2, 0, rows - NA_KROWS))
        slabs = []
        for qr in range(r0, r0 + NA_QROWS):
            rs = int(np.clip(qr - NA_KH // 2, 0, rows - NA_KH))
            slabs.append(jnp.concatenate(
                [t1[:, kr - qr + NA_KH - 1] if rs <= kr < rs + NA_KH else masked
                 for kr in range(srow, srow + NA_KROWS)], axis=-1))
        tables.append(jnp.stack(slabs, axis=1))
    return jnp.stack(tables).reshape(3, H_NA, NA_QB, NA_KB)


def _band_kernel(q_ref, k_ref, v_ref, mask_ref, o_ref, lse_ref, *, length):
    j = pl.program_id(1)
    step = q_ref.shape[0]
    for sb in range(step // DIL_QB):
        rows = slice(sb * DIL_QB, (sb + 1) * DIL_QB)
        q0 = j * step + sb * DIL_QB
        start = pl.multiple_of(jnp.clip(q0 - DIL_HALF, 0, length - DIL_KB), DIL_HALF)
        variant = (q0 > 0).astype(jnp.int32) + (q0 == length - DIL_QB).astype(jnp.int32)
        mask = mask_ref[variant]
        mask2 = jnp.concatenate([mask, mask], axis=0)
        lane = lax.broadcasted_iota(jnp.int32, (1, LANES), 1)
        lse_heads = jnp.zeros((DIL_QB, LANES), F32)
        for p in range(W_DIL // LANES):
            sl = slice(p * LANES, (p + 1) * LANES)
            k = k_ref[pl.ds(start, DIL_KB), sl]
            v = v_ref[pl.ds(start, DIL_KB), sl]
            o, (lse0, lse1) = _pair_attention(q_ref[rows, sl], k, v, mask2, True)
            o_ref[rows, sl] = o.astype(BF16)
            lse_heads = jnp.where(lane == 2 * p, lse0, jnp.where(lane == 2 * p + 1, lse1, lse_heads))
        lse_ref[rows, :] = lse_heads


def _band_attention(q, k, v, masks):
    nseq, length, _ = q.shape
    step = min(DIL_STEP, length)
    nblk = length // step
    blk = pl.BlockSpec((None, step, W_DIL), lambda s, j: (s, j, 0))
    seq = pl.BlockSpec((None, length, W_DIL), lambda s, j: (s, 0, 0))
    return pl.pallas_call(
        functools.partial(_band_kernel, length=length),
        grid=(nseq, nblk),
        in_specs=[blk, seq, seq, pl.BlockSpec((3, DIL_QB, DIL_KB), lambda s, j: (0, 0, 0))],
        out_specs=[blk, pl.BlockSpec((None, step, LANES), lambda s, j: (s, j, 0))],
        out_shape=[jax.ShapeDtypeStruct((nseq, length, W_DIL), BF16),
                   jax.ShapeDtypeStruct((nseq, length, LANES), F32)],
        compiler_params=_cparams(("parallel", "arbitrary")),
        name="band_attention",
    )(q, k, v, masks)


def _band_masks():
    qq = np.arange(DIL_QB)[:, None]
    kk = np.arange(DIL_KB)[None, :]
    out = []
    for off in (0, -DIL_HALF, DIL_QB - DIL_KB):
        rel = kk + off - qq
        out.append(np.where(np.abs(rel) <= DIL_HALF, 0.0, NEG_INF))
    return jnp.asarray(np.stack(out), F32)


def _memkv_kernel(mem_ref, g_ref, w_ref, gk_ref, seg_ref, km_ref, vm_ref):
    mn = _rms(mem_ref[...], g_ref[...]).astype(BF16)
    kv = jnp.dot(mn, w_ref[...], preferred_element_type=F32)
    for c in range(W_MEM // LANES):
        x = kv[:, c * LANES:(c + 1) * LANES]
        ss = jnp.dot((x * x).astype(BF16), seg_ref[:LANES, :LANES], preferred_element_type=F32)
        x = x * lax.rsqrt(ss * (1.0 / HEAD_DIM) + RMS_EPS) * gk_ref[...]
        km_ref[:, c * LANES:(c + 1) * LANES] = x.astype(BF16)
    vm_ref[...] = kv[:, W_MEM:].astype(BF16)


def _memkv(mem2, g_mem, w_kv_bf16, gk_row, seg):
    rows = mem2.shape[0]
    full = lambda shape: pl.BlockSpec(shape, lambda i: (0,) * len(shape))
    return pl.pallas_call(
        _memkv_kernel,
        grid=(1,),
        in_specs=[full((rows, D_MODEL)), full((1, D_MODEL)), full((D_MODEL, 2 * W_MEM)),
                  full((1, LANES)), full((2 * LANES, 2 * LANES))],
        out_specs=[full((rows, W_MEM)), full((rows, W_MEM))],
        out_shape=[jax.ShapeDtypeStruct((rows, W_MEM), BF16)] * 2,
        compiler_params=_cparams(("arbitrary",)),
        name="mem_kv",
    )(mem2, g_mem, w_kv_bf16, gk_row, seg)


def _outproj_kernel(ona_ref, *rest, with_router):
    nd = 2 * len(DILATIONS)
    dil_refs, rest = rest[:nd], rest[nd:]
    om_ref, h_ref, go_ref, w_ref, gf_ref = rest[:5]
    rest = rest[5:]
    if with_router:
        wr_ref, hn_ref, u_ref, lg_ref = rest[:4]
        stage_refs = rest[4:]
    else:
        hn_ref, u_ref = rest[:2]
        stage_refs = rest[2:]
    stage_refs, tmp_ref = stage_refs[:-1], stage_refs[-1]
    tm = h_ref.shape[0]
    f = _RESIDUE_DILS[0]
    nc = W_DIL // LANES
    go = go_ref[...]
    spread = (lax.broadcasted_iota(jnp.int32, (LANES, W_DIL), 0)
              == lax.broadcasted_iota(jnp.int32, (LANES, W_DIL), 1) // HEAD_DIM).astype(BF16)

    def chain(r0, th):
        rows = slice(r0, r0 + th)

        def token_order(ref, d, st, ncols):
            if d == 1:
                return ref[rows, :].astype(F32)
            for c in range(ncols):
                cols = slice(c * LANES, (c + 1) * LANES)
                for r in range(f):
                    if d == f:
                        st[c, pl.ds(r0 + r, th // f, stride=f), :] = ref[
                            r, r0 // f:(r0 + th) // f, cols].astype(F32)
                    else:
                        for r2 in range(f):
                            tmp_ref[c * f + r, pl.ds(r0 // f + r2, th // d, stride=f), :] = ref[
                                r + f * r2, r0 // d:(r0 + th) // d, cols].astype(F32)
                        st[c, pl.ds(r0 + r, th // f, stride=f), :] = tmp_ref[
                            c * f + r, r0 // f:(r0 + th) // f, :]
            return jnp.concatenate([st[c, rows, :] for c in range(ncols)], axis=-1)

        def per_head_to_lanes(w):
            w_hi = w.astype(BF16)
            w_lo = (w - w_hi.astype(F32)).astype(BF16)
            return (jnp.dot(w_hi, spread, preferred_element_type=F32)
                    + jnp.dot(w_lo, spread, preferred_element_type=F32))

        os_, ls_ = [], []
        for di, d in enumerate(DILATIONS):
            o_ref_d, l_ref_d = dil_refs[2 * di:2 * di + 2]
            o_st, l_st = (stage_refs[2 * (di - 1)], stage_refs[2 * (di - 1) + 1]) if d > 1 else (None, None)
            os_.append(token_order(o_ref_d, d, o_st, nc))
            ls_.append(token_order(l_ref_d, d, l_st, 1))
        mx = functools.reduce(jnp.maximum, ls_)
        es = [jnp.exp(l - mx) for l in ls_]
        inv = 1.0 / sum(es)
        odil = sum(per_head_to_lanes(e * inv) * o for e, o in zip(es, os_))
        mixed = jnp.concatenate([
            _rms(ona_ref[rows, :].astype(F32), go[:, :W_NA]),
            _rms(odil, go[:, W_NA:W_NA + W_DIL]),
            _rms(om_ref[rows, :].astype(F32), go[:, W_NA + W_DIL:]),
        ], axis=-1).astype(BF16)
        hn = h_ref[rows, :] + jnp.dot(mixed, w_ref[...], preferred_element_type=F32)
        hn_ref[rows, :] = hn
        u = _rms(hn, gf_ref[...])
        if with_router:
            u_ref[rows, :] = u
            u_hi = u.astype(BF16)
            u_lo = (u - u_hi.astype(F32)).astype(BF16)
            lg_ref[rows, :] = (jnp.dot(u_hi, wr_ref[...], preferred_element_type=F32)
                               + jnp.dot(u_lo, wr_ref[...], preferred_element_type=F32))
        else:
            u_ref[rows, :] = u.astype(BF16)

    th = tm // PROJ_CHAINS
    for ci in range(PROJ_CHAINS):
        chain(ci * th, th)


def _outproj(ona, dil, om, h, g_out, w_out_bf16, g_ffn, w_router_pad, batch, seq_len):
    n = h.shape[0]
    tm = TM_PROJ
    tblocks = seq_len // tm
    with_router = w_router_pad is not None
    row = lambda w: pl.BlockSpec((tm, w), lambda i: (i, 0))
    full = lambda shape: pl.BlockSpec(shape, lambda i: (0,) * len(shape))
    in_specs = [row(W_NA)]
    args = [ona]
    scratch = []
    for di, d in enumerate(DILATIONS):
        for a, w in zip(dil[2 * di:2 * di + 2], (W_DIL, LANES)):
            if d == 1:
                in_specs.append(row(w))
                args.append(a.reshape(n, w))
            else:
                in_specs.append(pl.BlockSpec((None, d, tm // d, w),
                                             lambda i: (i // tblocks, 0, i % tblocks, 0)))
                args.append(a.reshape(batch, d, seq_len // d, w))
                scratch.append(pltpu.VMEM((w // LANES, tm, LANES), F32))
    f = _RESIDUE_DILS[0]
    scratch.append(pltpu.VMEM((W_DIL // LANES * f, tm // f, LANES), F32))
    in_specs += [row(W_MEM), row(D_MODEL), full((1, D_MODEL)), full((D_MODEL, D_MODEL)),
                 full((1, D_MODEL))]
    args += [om, h, g_out, w_out_bf16, g_ffn]
    out_specs = [row(D_MODEL), row(D_MODEL)]
    out_shape = [jax.ShapeDtypeStruct((n, D_MODEL), F32),
                 jax.ShapeDtypeStruct((n, D_MODEL), F32 if with_router else BF16)]
    if with_router:
        in_specs.append(full((D_MODEL, LANES)))
        args.append(w_router_pad)
        out_specs.append(row(LANES))
        out_shape.append(jax.ShapeDtypeStruct((n, LANES), F32))
    return pl.pallas_call(
        functools.partial(_outproj_kernel, with_router=with_router),
        grid=(n // tm,),
        in_specs=in_specs,
        out_specs=out_specs,
        out_shape=out_shape,
        scratch_shapes=scratch,
        compiler_params=_cparams(("parallel",)),
        name="outproj",
    )(*args)


def _swiglu(x, wg, wu, wd):
    g = jnp.dot(x, wg, preferred_element_type=F32)
    u = jnp.dot(x, wu, preferred_element_type=F32)
    hm = (g * jax.nn.sigmoid(g) * u).astype(BF16)
    return jnp.dot(hm, wd, preferred_element_type=F32)


def _dense_ffn_kernel(u_ref, h_ref, wg_ref, wu_ref, wd_ref, o_ref, hm_ref):
    x = u_ref[...]
    for c in range(hm_ref.shape[1] // TF_DENSE):
        cols = slice(c * TF_DENSE, (c + 1) * TF_DENSE)
        g = jnp.dot(x, wg_ref[:, cols], preferred_element_type=F32)
        u = jnp.dot(x, wu_ref[:, cols], preferred_element_type=F32)
        hm_ref[:, cols] = (g * jax.nn.sigmoid(g) * u).astype(BF16)
    o_ref[...] = h_ref[...] + jnp.dot(hm_ref[...], wd_ref[...], preferred_element_type=F32)


def _dense_ffn(u, h, wg_bf16, wu_bf16, wd_bf16):
    n = h.shape[0]
    d_ff = wg_bf16.shape[1]
    tm = TM_FFN
    full = lambda shape: pl.BlockSpec(shape, lambda i: (0,) * len(shape))
    return pl.pallas_call(
        _dense_ffn_kernel,
        grid=(n // tm,),
        in_specs=[
            pl.BlockSpec((tm, D_MODEL), lambda i: (i, 0)),
            pl.BlockSpec((tm, D_MODEL), lambda i: (i, 0)),
            full((D_MODEL, d_ff)), full((D_MODEL, d_ff)), full((d_ff, D_MODEL)),
        ],
        out_specs=pl.BlockSpec((tm, D_MODEL), lambda i: (i, 0)),
        out_shape=jax.ShapeDtypeStruct((n, D_MODEL), F32),
        scratch_shapes=[pltpu.VMEM((tm, d_ff), BF16)],
        compiler_params=_cparams(("parallel",)),
        name="dense_ffn",
    )(u, h, wg_bf16, wu_bf16, wd_bf16)


def _dispatch_kernel(vend_ref, pend_ref, d0_ref, d1_ref, u_ref, x_hbm, zbuf, sem, zsem):
    i = pl.program_id(0)
    tc = u_ref.shape[0]
    zr = zbuf.shape[0]
    n_rows = x_hbm.shape[0]

    def zero_pieces(lo, hi):
        def start(p, carry):
            pltpu.make_async_copy(zbuf, x_hbm.at[pl.ds(pl.multiple_of(p * zr, zr), zr), :], zsem).start()
            return carry

        def wait(p, carry):
            pltpu.make_async_copy(zbuf, x_hbm.at[pl.ds(0, zr), :], zsem).wait()
            return carry

        lax.fori_loop(lo, hi, start, 0)
        lax.fori_loop(lo, hi, wait, 0)

    @pl.when(i == 0)
    def _():
        zbuf[...] = jnp.zeros_like(zbuf)
        for e in range(N_EXPERTS):
            zero_pieces(vend_ref[e] // zr, pend_ref[e] // zr)
        zero_pieces(pend_ref[N_EXPERTS - 1] // zr, n_rows // zr)

    def issue(r, carry):
        src = u_ref.at[pl.ds(r, 1), :]
        pltpu.make_async_copy(src, x_hbm.at[pl.ds(d0_ref[0, 0, r], 1), :], sem).start(priority=0)
        pltpu.make_async_copy(src, x_hbm.at[pl.ds(d1_ref[0, 0, r], 1), :], sem).start(priority=1)
        return carry

    lax.fori_loop(0, tc, issue, 0, unroll=ISSUE_UNROLL)
    for _ in range(TOP_K):
        pltpu.make_async_copy(u_ref, x_hbm.at[pl.ds(0, tc), :], sem).wait()


def _dispatch(u, dest, vend, pend, n_rows):
    n = u.shape[0]
    tc = TC_COMBINE
    nblk = n // tc
    d0 = dest[:, 0].reshape(nblk, 1, tc)
    d1 = dest[:, 1].reshape(nblk, 1, tc)
    smem = pl.BlockSpec((1, 1, tc), lambda i, ve, pe: (i, 0, 0), memory_space=pltpu.SMEM)
    grid_spec = pltpu.PrefetchScalarGridSpec(
        num_scalar_prefetch=2,
        grid=(nblk,),
        in_specs=[smem, smem, pl.BlockSpec((tc, D_MODEL), lambda i, ve, pe: (i, 0))],
        out_specs=pl.BlockSpec(memory_space=pl.ANY),
        scratch_shapes=[pltpu.VMEM((ZERO_ROWS, D_MODEL), F32), pltpu.SemaphoreType.DMA(()),
                        pltpu.SemaphoreType.DMA(())],
    )
    return pl.pallas_call(
        _dispatch_kernel,
        grid_spec=grid_spec,
        out_shape=jax.ShapeDtypeStruct((n_rows, D_MODEL), F32),
        compiler_params=_cparams(("arbitrary",)),
        name="moe_dispatch",
    )(vend, pend, d0, d1, u)


def _moe_ffn_kernel(be_ref, nv_ref, nu_ref, x_ref, wg_ref, wu_ref, wd_ref, y_ref, xb):
    i = pl.program_id(0)
    j = pl.program_id(1)
    active = i < nu_ref[0]
    tm = x_ref.shape[0]
    nchunk = tm // MOE_CHUNK

    def chunk(c, wg, wu, wd):
        rows = slice(c * MOE_CHUNK, (c + 1) * MOE_CHUNK)
        y_ref[rows, :] += _swiglu(xb[rows, :], wg, wu, wd)

    @pl.when(j == 0)
    def _():
        y_ref[...] = jnp.zeros_like(y_ref)

    @pl.when(active)
    def _():
        @pl.when(j == 0)
        def _():
            xb[...] = x_ref[...].astype(BF16)

        nvalid = nv_ref[i]

        @pl.when(nvalid > tm - MOE_CHUNK)
        def _():
            wg, wu, wd = (r[...].astype(BF16) for r in (wg_ref, wu_ref, wd_ref))
            for c in range(nchunk):
                chunk(c, wg, wu, wd)

        @pl.when(nvalid <= tm - MOE_CHUNK)
        def _():
            wg, wu, wd = (r[...].astype(BF16) for r in (wg_ref, wu_ref, wd_ref))
            for c in range(nchunk):
                @pl.when(c * MOE_CHUNK < nvalid)
                def _():
                    chunk(c, wg, wu, wd)


def _moe_ffn(x_sorted, blk_expert, blk_valid, n_used, wg, wu, wd):
    n_rows = x_sorted.shape[0]
    tm, tf = TM_MOE, TF_MOE
    nblk = n_rows // tm
    d_ff = wg.shape[2]
    nj = d_ff // tf

    def jeff(i, j, nu):
        return jnp.where(i < nu[0], j, nj - 1)

    grid_spec = pltpu.PrefetchScalarGridSpec(
        num_scalar_prefetch=3,
        grid=(nblk, nj),
        in_specs=[
            pl.BlockSpec((tm, D_MODEL), lambda i, j, be, nv, nu: (jnp.minimum(i, nu[0] - 1), 0)),
            pl.BlockSpec((None, D_MODEL, tf), lambda i, j, be, nv, nu: (be[i], 0, jeff(i, j, nu))),
            pl.BlockSpec((None, D_MODEL, tf), lambda i, j, be, nv, nu: (be[i], 0, jeff(i, j, nu))),
            pl.BlockSpec((None, tf, D_MODEL), lambda i, j, be, nv, nu: (be[i], jeff(i, j, nu), 0)),
        ],
        out_specs=pl.BlockSpec((tm, D_MODEL), lambda i, j, be, nv, nu: (i, 0)),
        scratch_shapes=[pltpu.VMEM((tm, D_MODEL), BF16)],
    )
    return pl.pallas_call(
        _moe_ffn_kernel,
        grid_spec=grid_spec,
        out_shape=jax.ShapeDtypeStruct((n_rows, D_MODEL), F32),
        compiler_params=_cparams(("arbitrary", "arbitrary")),
        name="moe_ffn",
    )(blk_expert, blk_valid, n_used, x_sorted, wg, wu, wd)


def _combine_kernel(p0_ref, p1_ref, h_ref, g_ref, y_hbm, o_ref, buf0, buf1, sem):
    tc = buf0.shape[0]

    def issue(r, carry):
        pltpu.make_async_copy(y_hbm.at[pl.ds(p0_ref[0, 0, r], 1), :], buf0.at[pl.ds(r, 1), :],
                              sem.at[0]).start(priority=0)
        pltpu.make_async_copy(y_hbm.at[pl.ds(p1_ref[0, 0, r], 1), :], buf1.at[pl.ds(r, 1), :],
                              sem.at[1]).start(priority=1)
        return carry

    lax.fori_loop(0, tc, issue, 0, unroll=ISSUE_UNROLL)
    pltpu.make_async_copy(y_hbm.at[pl.ds(0, tc), :], buf0, sem.at[0]).wait()
    pltpu.make_async_copy(y_hbm.at[pl.ds(0, tc), :], buf1, sem.at[1]).wait()
    g = g_ref[...]
    o_ref[...] = h_ref[...] + g[:, 0:1] * buf0[...] + g[:, 1:2] * buf1[...]


def _combine(h, y, dest, gates):
    n = h.shape[0]
    tc = TC_COMBINE
    nblk = n // tc
    p0 = dest[:, 0].reshape(nblk, 1, tc)
    p1 = dest[:, 1].reshape(nblk, 1, tc)
    smem = pl.BlockSpec((1, 1, tc), lambda i: (i, 0, 0), memory_space=pltpu.SMEM)
    return pl.pallas_call(
        _combine_kernel,
        grid=(nblk,),
        in_specs=[smem, smem, pl.BlockSpec((tc, D_MODEL), lambda i: (i, 0)),
                  pl.BlockSpec((tc, TOP_K), lambda i: (i, 0)), pl.BlockSpec(memory_space=pl.ANY)],
        out_specs=pl.BlockSpec((tc, D_MODEL), lambda i: (i, 0)),
        out_shape=jax.ShapeDtypeStruct((n, D_MODEL), F32),
        scratch_shapes=[pltpu.VMEM((tc, D_MODEL), F32), pltpu.VMEM((tc, D_MODEL), F32),
                        pltpu.SemaphoreType.DMA((2,))],
        compiler_params=_cparams(("arbitrary",)),
        name="moe_combine",
    )(p0, p1, h, gates, y)


def _route(logits):
    n = logits.shape[0]
    top_logit, top_idx = lax.top_k(logits, TOP_K)
    gates = jax.nn.softmax(top_logit, axis=-1)
    e_flat = top_idx.reshape(n * TOP_K).astype(jnp.int32)
    onehot = (e_flat[:, None] == jnp.arange(N_EXPERTS, dtype=jnp.int32)[None, :]).astype(jnp.int32)
    csum = jnp.cumsum(onehot, axis=0)
    counts = csum[-1]
    padded = (counts + TM_MOE - 1) // TM_MOE * TM_MOE
    pend = jnp.cumsum(padded)
    pstart = pend - padded
    dest = jnp.sum(onehot * (csum - 1 + pstart[None, :]), axis=1).reshape(n, TOP_K)
    n_rows = n * TOP_K + N_EXPERTS * TM_MOE
    nblk = n_rows // TM_MOE
    n_used = pend[-1] // TM_MOE
    blk_start = jnp.minimum(jnp.arange(nblk, dtype=jnp.int32) * TM_MOE, (n_used - 1) * TM_MOE)
    blk_expert = jnp.minimum(
        jnp.sum((blk_start[:, None] >= pend[None, :]).astype(jnp.int32), axis=1), N_EXPERTS - 1)
    blk_valid = jnp.clip(counts[blk_expert] - (blk_start - pstart[blk_expert]), 0, TM_MOE)
    i32 = lambda a: a.astype(jnp.int32)
    return (i32(dest), gates, i32(pstart + counts), i32(pend), i32(blk_expert), i32(blk_valid),
            i32(n_used).reshape(1), n_rows)


def _rope_tables(seq_len):
    half = HEAD_DIM // 2
    inv_freq = jnp.power(ROPE_THETA, -(2.0 / HEAD_DIM) * jnp.arange(half, dtype=F32))
    ang = jnp.arange(seq_len, dtype=F32)[:, None] * inv_freq[None, :]
    cos, sin = jnp.cos(ang), jnp.sin(ang)
    cos_t = jnp.tile(cos, (1, LANES // half))
    sin_t = jnp.tile(jnp.concatenate([-sin, sin], axis=1), (1, LANES // HEAD_DIM))
    return cos_t, sin_t


def kernel(x, mem, g_attn, w_in, g_qk_na, rpb_na, g_qk_dil, g_mem, w_mem_kv, g_qk_mem, g_out, w_out,
           g_ffn, w_gate_dense, w_up_dense, w_down_dense, w_router, w_gate_moe, w_up_moe, w_down_moe):
    batch, seq_len, _ = x.shape
    mem_len = mem.shape[1]
    depth = g_attn.shape[0]
    n = batch * seq_len
    scale = HEAD_DIM ** -0.5

    cos_t, sin_t = _rope_tables(seq_len)
    seg = jnp.asarray(np.kron(np.eye(2 * LANES // HEAD_DIM), np.ones((HEAD_DIM, HEAD_DIM))), BF16)
    band_masks = _band_masks()
    ones = lambda w: jnp.ones((w,), F32)

    h = x.reshape(n, D_MODEL)
    mem2 = mem.reshape(batch * mem_len, D_MODEL)
    for layer in range(depth):
        gq_row = jnp.concatenate([
            jnp.tile(g_qk_na[layer, 0] * scale, H_NA), jnp.tile(g_qk_na[layer, 1], H_NA), ones(W_NA),
            jnp.tile(g_qk_dil[layer, 0] * scale, H_DIL), jnp.tile(g_qk_dil[layer, 1], H_DIL), ones(W_DIL),
            jnp.tile(g_qk_mem[layer, 0] * scale, H_MEM)]).reshape(1, IN_WIDTH)
        proj = _inproj(h, g_attn[layer].reshape(1, D_MODEL), w_in[layer].astype(BF16), gq_row, cos_t, sin_t,
                       seg, batch, seq_len)
        qa, ka, va, qb, kb, vb, qm = proj[:_N_FLAT_OUT]

        dil = []
        for d in DILATIONS:
            if d == 1:
                qkv = [a.reshape(batch, seq_len, W_DIL) for a in (qb, kb, vb)]
            else:
                base = _N_FLAT_OUT + 3 * _RESIDUE_DILS.index(d)
                qkv = [a.reshape(batch * d, seq_len // d, W_DIL) for a in proj[base:base + 3]]
            dil += _band_attention(*qkv, band_masks)

        km, vm = _memkv(mem2, g_mem[layer].reshape(1, D_MODEL), w_mem_kv[layer].astype(BF16),
                        jnp.tile(g_qk_mem[layer, 1], LANES // HEAD_DIM).reshape(1, LANES), seg)
        o_na, o_mem = _na_mem_attention(qa, ka, va, _na_bias_tables(rpb_na[layer], seq_len // GRID_W),
                                        qm, km, vm, batch, seq_len, mem_len)

        i = layer // 2
        moe = layer % 2 == 1
        w_router_pad = None
        if moe:
            wr_hi = w_router[i].astype(BF16)
            wr_lo = (w_router[i] - wr_hi.astype(F32)).astype(BF16)
            w_router_pad = jnp.pad(jnp.concatenate([wr_hi, wr_lo], axis=1),
                                   ((0, 0), (0, LANES - 2 * N_EXPERTS)))
        outs = _outproj(o_na, dil, o_mem, h, g_out[layer].reshape(1, D_MODEL), w_out[layer].astype(BF16),
                        g_ffn[layer].reshape(1, D_MODEL), w_router_pad, batch, seq_len)
        if not moe:
            h, u = outs
            h = _dense_ffn(u, h, w_gate_dense[i].astype(BF16), w_up_dense[i].astype(BF16),
                           w_down_dense[i].astype(BF16))
        else:
            h, u, logits = outs
            dest, gates, vend, pend, blk_expert, blk_valid, n_used, n_rows = _route(
                logits[:, :N_EXPERTS] + logits[:, N_EXPERTS:2 * N_EXPERTS])
            x_sorted = _dispatch(u, dest, vend, pend, n_rows)
            y = _moe_ffn(x_sorted, blk_expert, blk_valid, n_used, w_gate_moe[i], w_up_moe[i], w_down_moe[i])
            h = _combine(h, y, dest, gates)
    return h.reshape(batch, seq_len, D_MODEL)
```

```python
import functools

import numpy as np
import jax
import jax.numpy as jnp
from jax import lax
from jax.experimental import pallas as pl
from jax.experimental.pallas import tpu as pltpu

F32 = jnp.float32
BF16 = jnp.bfloat16

D_MODEL = 1024
HEAD_DIM = 64
H_NA, H_DIL, H_MEM = 6, 6, 4
W_NA, W_DIL, W_MEM = H_NA * HEAD_DIM, H_DIL * HEAD_DIM, H_MEM * HEAD_DIM
IN_WIDTH = 3 * W_NA + 3 * W_DIL + W_MEM
GRID_W = 64
NA_KH, NA_KW = 8, 16
DIL_CFG = ((128, 1), (512, 4), (2048, 16))
ROPE_THETA = 10000.0
N_EXPERTS = 8
TOP_K = 2
RMS_EPS = 1e-6
NEG_INF = -1e30

LANES = 128
TM_PROJ = 512
PROJ_CHAINS = 2
NA_QROWS = 4
NA_QB = NA_QROWS * GRID_W
NA_KROWS = 12
NA_KB = NA_KROWS * GRID_W
NA_STEP_BLOCKS = 2
DIL_QB = 128
DIL_HALF = 64
DIL_KB = DIL_QB + 2 * DIL_HALF
DIL_STEP = 1024
TM_FFN = 512
TF_DENSE = 256
TM_MOE = 2048
TF_MOE = 512
MOE_CHUNK = 256
TC_COMBINE = 512
ZERO_ROWS = 256
ISSUE_UNROLL = 8
VMEM_LIMIT = 56 * 1024 * 1024


def _cparams(sem):
    return pltpu.CompilerParams(dimension_semantics=sem, vmem_limit_bytes=VMEM_LIMIT)


def _rms(x, g):
    return x * lax.rsqrt(jnp.mean(x * x, axis=-1, keepdims=True) + RMS_EPS) * g


_CHUNKS = (
    [(0, c, True, False) for c in range(3)] + [(1, c, True, False) for c in range(3)]
    + [(2, c, False, False) for c in range(3)]
    + [(3, c, True, True) for c in range(3)] + [(4, c, True, True) for c in range(3)]
    + [(5, c, False, False) for c in range(3)]
    + [(6, c, True, False) for c in range(2)]
)


DILATIONS = tuple(d for _, d in DIL_CFG)
_RESIDUE_DILS = tuple(d for d in DILATIONS if d > 1)
assert len(_RESIDUE_DILS) == 2 and _RESIDUE_DILS[1] == _RESIDUE_DILS[0] ** 2
_N_FLAT_OUT = 7


def _inproj_kernel(h_ref, ga_ref, w_ref, gq_ref, cos_ref, sin_ref, seg_ref, *rest):
    out_refs, stage_ref, stage2_ref, wb_ref = rest[:-3], rest[-3], rest[-2], rest[-1]
    tm = h_ref.shape[0]

    @pl.when(pl.program_id(0) == 0)
    def _():
        wb_ref[...] = w_ref[...].astype(BF16)

    seg = seg_ref[...]
    lane = lax.broadcasted_iota(jnp.int32, (1, LANES), 1)
    first_half = (lane % HEAD_DIM) < (HEAD_DIM // 2)
    f = _RESIDUE_DILS[0]

    def chain(r0, th):
        rows = slice(r0, r0 + th)
        u = _rms(h_ref[rows, :], ga_ref[...]).astype(BF16)
        p = jnp.dot(u, wb_ref[...], preferred_element_type=F32)
        normed = {}
        c = 0
        while c < len(_CHUNKS):
            if _CHUNKS[c][2]:
                assert _CHUNKS[c + 1][2]
                cols = slice(c * LANES, (c + 2) * LANES)
                x2 = p[:, cols]
                ss = jnp.dot((x2 * x2).astype(BF16), seg, preferred_element_type=F32)
                x2 = x2 * lax.rsqrt(ss * (1.0 / HEAD_DIM) + RMS_EPS) * gq_ref[:, cols]
                normed[c], normed[c + 1] = x2[:, :LANES], x2[:, LANES:]
                c += 2
            else:
                normed[c] = p[:, c * LANES:(c + 1) * LANES]
                c += 1
        for c, (oi, oc, norm, rope) in enumerate(_CHUNKS):
            x = normed[c]
            cols = slice(oc * LANES, (oc + 1) * LANES)
            if rope:
                swapped = jnp.where(first_half, pltpu.roll(x, LANES - HEAD_DIM // 2, 1),
                                    pltpu.roll(x, HEAD_DIM // 2, 1))
                x = x * cos_ref[rows, :] + swapped * sin_ref[rows, :]
            out_refs[oi][rows, cols] = x.astype(BF16)
            if 3 <= oi <= 5:
                st = (oi - 3) * (W_DIL // LANES) + oc
                dst1 = out_refs[_N_FLAT_OUT + (oi - 3)]
                dst2 = out_refs[_N_FLAT_OUT + 3 + (oi - 3)]
                stage_ref[st, rows, :] = x
                for r in range(f):
                    xr = stage_ref[st, pl.ds(r0 + r, th // f, stride=f), :]
                    dst1[r, r0 // f:(r0 + th) // f, cols] = xr.astype(BF16)
                    stage2_ref[st * f + r, r0 // f:(r0 + th) // f, :] = xr
                    for r2 in range(f):
                        dst2[r + f * r2, r0 // (f * f):(r0 + th) // (f * f), cols] = stage2_ref[
                            st * f + r, pl.ds(r0 // f + r2, th // (f * f), stride=f), :].astype(BF16)

    th = tm // PROJ_CHAINS
    for ci in range(PROJ_CHAINS):
        chain(ci * th, th)


def _inproj(h, g_attn, w_in, layer, gq_row, cos_t, sin_t, seg, batch, seq_len):
    n = h.shape[0]
    tm = TM_PROJ
    tblocks = seq_len // tm
    widths = (W_NA, W_NA, W_NA, W_DIL, W_DIL, W_DIL, W_MEM)
    nst, f = 3 * W_DIL // LANES, _RESIDUE_DILS[0]
    out_specs = [pl.BlockSpec((tm, w), lambda i: (i, 0)) for w in widths]
    out_shape = [jax.ShapeDtypeStruct((n, w), BF16) for w in widths]
    for d in _RESIDUE_DILS:
        out_specs += [pl.BlockSpec((None, d, tm // d, W_DIL),
                                   lambda i: (i // tblocks, 0, i % tblocks, 0))] * 3
        out_shape += [jax.ShapeDtypeStruct((batch, d, seq_len // d, W_DIL), BF16)] * 3
    return pl.pallas_call(
        _inproj_kernel,
        grid=(n // tm,),
        in_specs=[
            pl.BlockSpec((tm, D_MODEL), lambda i: (i, 0)),
            pl.BlockSpec((1, D_MODEL), lambda i: (0, 0)),
            pl.BlockSpec((None, D_MODEL, IN_WIDTH), lambda i: (layer, 0, 0)),
            pl.BlockSpec((1, IN_WIDTH), lambda i: (0, 0)),
            pl.BlockSpec((tm, LANES), lambda i: (i % tblocks, 0)),
            pl.BlockSpec((tm, LANES), lambda i: (i % tblocks, 0)),
            pl.BlockSpec((2 * LANES, 2 * LANES), lambda i: (0, 0)),
        ],
        out_specs=out_specs,
        out_shape=out_shape,
        scratch_shapes=[pltpu.VMEM((nst, tm, LANES), F32), pltpu.VMEM((nst * f, tm // f, LANES), F32),
                        pltpu.VMEM((D_MODEL, IN_WIDTH), BF16)],
        compiler_params=_cparams(("arbitrary",)),
        name="inproj",
    )(h, g_attn, w_in, gq_row, cos_t, sin_t, seg)


def _pair_attention(q, k, v, bias, want_lse):
    nq = q.shape[0]
    first = lax.broadcasted_iota(jnp.int32, (1, LANES), 1) < HEAD_DIM
    zero = jnp.zeros_like(q)
    q2 = jnp.concatenate([jnp.where(first, q, zero), jnp.where(first, zero, q)], axis=0)
    v_ones = jnp.concatenate([v, jnp.ones_like(v)], axis=1)
    o, lse = _softmax_pv(q2, k, v_ones, bias, want_lse)
    out = jnp.where(first, o[:nq], o[nq:])
    return out, ((lse[:nq], lse[nq:]) if want_lse else None)


def _softmax_pv(q, k, v_ones, bias, want_lse):
    s = lax.dot_general(q, k, (((1,), (1,)), ((), ())), preferred_element_type=F32)
    if bias is not None:
        s = s + bias
    m = jnp.max(s, axis=-1, keepdims=True)
    p = jnp.exp(s - m).astype(BF16)
    ol = jnp.dot(p, v_ones, preferred_element_type=F32)
    l = ol[:, LANES:]
    return ol[:, :LANES] / l, (m + jnp.log(l)) if want_lse else None


def _na_mem_kernel(q_ref, k_ref, v_ref, *rest, rows):
    bias_refs = rest[:NA_STEP_BLOCKS]
    qm_ref, km_ref, vm_ref, o_ref, om_ref = rest[NA_STEP_BLOCKS:]
    i = pl.program_id(1)
    first = lax.broadcasted_iota(jnp.int32, (1, LANES), 1) < HEAD_DIM
    for sb, bias_ref in enumerate(bias_refs):
        qrows = slice(sb * NA_QB, (sb + 1) * NA_QB)
        srow = jnp.clip((i * NA_STEP_BLOCKS + sb) * NA_QROWS - NA_KH // 2, 0, rows - NA_KROWS)
        start = pl.multiple_of(srow * GRID_W, GRID_W)
        for p in range(W_NA // LANES):
            sl = slice(p * LANES, (p + 1) * LANES)
            k = k_ref[pl.ds(start, NA_KB), sl]
            v = v_ref[pl.ds(start, NA_KB), sl]
            q = q_ref[qrows, sl]
            zero = jnp.zeros_like(q)
            v_ones = jnp.concatenate([v, jnp.ones_like(v)], axis=1)
            o0, _ = _softmax_pv(jnp.where(first, q, zero), k, v_ones, bias_ref[0, 2 * p], False)
            o1, _ = _softmax_pv(jnp.where(first, zero, q), k, v_ones, bias_ref[0, 2 * p + 1], False)
            o_ref[qrows, sl] = jnp.where(first, o0, o1).astype(BF16)
        for c in range(W_MEM // LANES):
            sl = slice(c * LANES, (c + 1) * LANES)
            om, _ = _pair_attention(qm_ref[qrows, sl], km_ref[:, sl], vm_ref[:, sl], None, False)
            om_ref[qrows, sl] = om.astype(BF16)


def _na_mem_attention(q, k, v, bias, layer, qm, km, vm, batch, seq_len, mem_len):
    rows = seq_len // GRID_W
    nblk = seq_len // NA_QB
    step = NA_STEP_BLOCKS * NA_QB
    q3, k3, v3 = (a.reshape(batch, seq_len, W_NA) for a in (q, k, v))

    def bias_spec(sb):
        def index(b, i):
            blk = i * NA_STEP_BLOCKS + sb
            return (layer, (blk > 0).astype(jnp.int32) + (blk == nblk - 1).astype(jnp.int32), 0, 0, 0)
        return pl.BlockSpec((None, 1, H_NA, NA_QB, NA_KB), index)

    o, om = pl.pallas_call(
        functools.partial(_na_mem_kernel, rows=rows),
        grid=(batch, nblk // NA_STEP_BLOCKS),
        in_specs=[
            pl.BlockSpec((None, step, W_NA), lambda b, i: (b, i, 0)),
            pl.BlockSpec((None, seq_len, W_NA), lambda b, i: (b, 0, 0)),
            pl.BlockSpec((None, seq_len, W_NA), lambda b, i: (b, 0, 0)),
            *[bias_spec(sb) for sb in range(NA_STEP_BLOCKS)],
            pl.BlockSpec((None, step, W_MEM), lambda b, i: (b, i, 0)),
            pl.BlockSpec((None, mem_len, W_MEM), lambda b, i: (b, 0, 0)),
            pl.BlockSpec((None, mem_len, W_MEM), lambda b, i: (b, 0, 0)),
        ],
        out_specs=[pl.BlockSpec((None, step, W_NA), lambda b, i: (b, i, 0)),
                   pl.BlockSpec((None, step, W_MEM), lambda b, i: (b, i, 0))],
        out_shape=[jax.ShapeDtypeStruct((batch, seq_len, W_NA), BF16),
                   jax.ShapeDtypeStruct((batch, seq_len, W_MEM), BF16)],
        compiler_params=_cparams(("parallel", "arbitrary")),
        name="na_mem_attention",
    )(q3, k3, v3, *([bias] * NA_STEP_BLOCKS), qm.reshape(batch, seq_len, W_MEM),
      km.reshape(batch, mem_len, W_MEM), vm.reshape(batch, mem_len, W_MEM))
    return o.reshape(batch * seq_len, W_NA), om.reshape(batch * seq_len, W_MEM)


def _na_bias_tables(rpb, rows):
    depth = rpb.shape[0]
    nblk = rows // NA_QROWS
    qc = np.arange(GRID_W)[:, None]
    kc = np.arange(GRID_W)[None, :]
    cs = np.clip(qc - NA_KW // 2, 0, GRID_W - NA_KW)
    col_valid = (kc >= cs) & (kc < cs + NA_KW)
    col_rel = kc - qc + (NA_KW - 1)
    col_sel = (col_rel[None] == np.arange(2 * NA_KW - 1)[:, None, None]).astype(np.float32)
    t1 = jnp.einsum('lhab,bqk->lhaqk', rpb, jnp.asarray(col_sel), precision=lax.Precision.HIGHEST)
    t1 = jnp.where(jnp.asarray(col_valid), t1, NEG_INF)
    masked = jnp.full((depth, H_NA, GRID_W, GRID_W), NEG_INF, F32)
    tables = []
    for blk in (0, 1, nblk - 1):
        r0 = blk * NA_QROWS
        srow = int(np.clip(r0 - NA_KH // 2, 0, rows - NA_KROWS))
        slabs = []
        for qr in range(r0, r0 + NA_QROWS):
            rs = int(np.clip(qr - NA_KH // 2, 0, rows - NA_KH))
            slabs.append(jnp.concatenate(
                [t1[:, :, kr - qr + NA_KH - 1] if rs <= kr < rs + NA_KH else masked
                 for kr in range(srow, srow + NA_KROWS)], axis=-1))
        tables.append(jnp.stack(slabs, axis=2))
    return jnp.stack(tables, axis=1).reshape(depth, 3, H_NA, NA_QB, NA_KB)


def _band_kernel(q_ref, k_ref, v_ref, mask_ref, o_ref, lse_ref, *, length):
    j = pl.program_id(1)
    step = q_ref.shape[0]
    for sb in range(step // DIL_QB):
        rows = slice(sb * DIL_QB, (sb + 1) * DIL_QB)
        q0 = j * step + sb * DIL_QB
        start = pl.multiple_of(jnp.clip(q0 - DIL_HALF, 0, length - DIL_KB), DIL_HALF)
        variant = (q0 > 0).astype(jnp.int32) + (q0 == length - DIL_QB).astype(jnp.int32)
        mask = mask_ref[variant]
        mask2 = jnp.concatenate([mask, mask], axis=0)
        lane = lax.broadcasted_iota(jnp.int32, (1, LANES), 1)
        lse_heads = jnp.zeros((DIL_QB, LANES), F32)
        for p in range(W_DIL // LANES):
            sl = slice(p * LANES, (p + 1) * LANES)
            k = k_ref[pl.ds(start, DIL_KB), sl]
            v = v_ref[pl.ds(start, DIL_KB), sl]
            o, (lse0, lse1) = _pair_attention(q_ref[rows, sl], k, v, mask2, True)
            o_ref[rows, sl] = o.astype(BF16)
            lse_heads = jnp.where(lane == 2 * p, lse0, jnp.where(lane == 2 * p + 1, lse1, lse_heads))
        lse_ref[rows, :] = lse_heads


def _band_attention(q, k, v, masks):
    nseq, length, _ = q.shape
    step = min(DIL_STEP, length)
    nblk = length // step
    blk = pl.BlockSpec((None, step, W_DIL), lambda s, j: (s, j, 0))
    seq = pl.BlockSpec((None, length, W_DIL), lambda s, j: (s, 0, 0))
    return pl.pallas_call(
        functools.partial(_band_kernel, length=length),
        grid=(nseq, nblk),
        in_specs=[blk, seq, seq, pl.BlockSpec((3, DIL_QB, DIL_KB), lambda s, j: (0, 0, 0))],
        out_specs=[blk, pl.BlockSpec((None, step, LANES), lambda s, j: (s, j, 0))],
        out_shape=[jax.ShapeDtypeStruct((nseq, length, W_DIL), BF16),
                   jax.ShapeDtypeStruct((nseq, length, LANES), F32)],
        compiler_params=_cparams(("parallel", "arbitrary")),
        name="band_attention",
    )(q, k, v, masks)


def _band_masks():
    qq = np.arange(DIL_QB)[:, None]
    kk = np.arange(DIL_KB)[None, :]
    out = []
    for off in (0, -DIL_HALF, DIL_QB - DIL_KB):
        rel = kk + off - qq
        out.append(np.where(np.abs(rel) <= DIL_HALF, 0.0, NEG_INF))
    return jnp.asarray(np.stack(out), F32)


def _memkv_kernel(mem_ref, g_ref, w_ref, gk_ref, seg_ref, km_ref, vm_ref):
    mn = _rms(mem_ref[...], g_ref[...]).astype(BF16)
    kv = jnp.dot(mn, w_ref[...], preferred_element_type=F32)
    for c in range(W_MEM // LANES):
        x = kv[:, c * LANES:(c + 1) * LANES]
        ss = jnp.dot((x * x).astype(BF16), seg_ref[:LANES, :LANES], preferred_element_type=F32)
        x = x * lax.rsqrt(ss * (1.0 / HEAD_DIM) + RMS_EPS) * gk_ref[...]
        km_ref[:, c * LANES:(c + 1) * LANES] = x.astype(BF16)
    vm_ref[...] = kv[:, W_MEM:].astype(BF16)


def _memkv(mem2, g_mem, w_kv_bf16, gk_row, seg):
    rows = mem2.shape[0]
    full = lambda shape: pl.BlockSpec(shape, lambda i: (0,) * len(shape))
    return pl.pallas_call(
        _memkv_kernel,
        grid=(1,),
        in_specs=[full((rows, D_MODEL)), full((1, D_MODEL)), full((D_MODEL, 2 * W_MEM)),
                  full((1, LANES)), full((2 * LANES, 2 * LANES))],
        out_specs=[full((rows, W_MEM)), full((rows, W_MEM))],
        out_shape=[jax.ShapeDtypeStruct((rows, W_MEM), BF16)] * 2,
        compiler_params=_cparams(("arbitrary",)),
        name="mem_kv",
    )(mem2, g_mem, w_kv_bf16, gk_row, seg)


def _outproj_kernel(ona_ref, *rest, with_router):
    nd = 2 * len(DILATIONS)
    dil_refs, rest = rest[:nd], rest[nd:]
    om_ref, h_ref, go_ref, w_ref, gf_ref = rest[:5]
    rest = rest[5:]
    if with_router:
        wr_ref, hn_ref, u_ref, lg_ref = rest[:4]
        stage_refs = rest[4:]
    else:
        hn_ref, u_ref = rest[:2]
        stage_refs = rest[2:]
    stage_refs, tmp_ref, wb_ref = stage_refs[:-2], stage_refs[-2], stage_refs[-1]
    tm = h_ref.shape[0]
    f = _RESIDUE_DILS[0]
    nc = W_DIL // LANES
    go = go_ref[...]

    @pl.when(pl.program_id(0) == 0)
    def _():
        wb_ref[...] = w_ref[...].astype(BF16)

    spread = (lax.broadcasted_iota(jnp.int32, (LANES, W_DIL), 0)
              == lax.broadcasted_iota(jnp.int32, (LANES, W_DIL), 1) // HEAD_DIM).astype(BF16)

    def chain(r0, th):
        rows = slice(r0, r0 + th)

        def token_order(ref, d, st, ncols):
            if d == 1:
                return ref[rows, :].astype(F32)
            for c in range(ncols):
                cols = slice(c * LANES, (c + 1) * LANES)
                for r in range(f):
                    if d == f:
                        st[c, pl.ds(r0 + r, th // f, stride=f), :] = ref[
                            r, r0 // f:(r0 + th) // f, cols].astype(F32)
                    else:
                        for r2 in range(f):
                            tmp_ref[c * f + r, pl.ds(r0 // f + r2, th // d, stride=f), :] = ref[
                                r + f * r2, r0 // d:(r0 + th) // d, cols].astype(F32)
                        st[c, pl.ds(r0 + r, th // f, stride=f), :] = tmp_ref[
                            c * f + r, r0 // f:(r0 + th) // f, :]
            return jnp.concatenate([st[c, rows, :] for c in range(ncols)], axis=-1)

        def per_head_to_lanes(w):
            w_hi = w.astype(BF16)
            w_lo = (w - w_hi.astype(F32)).astype(BF16)
            return (jnp.dot(w_hi, spread, preferred_element_type=F32)
                    + jnp.dot(w_lo, spread, preferred_element_type=F32))

        os_, ls_ = [], []
        for di, d in enumerate(DILATIONS):
            o_ref_d, l_ref_d = dil_refs[2 * di:2 * di + 2]
            o_st, l_st = (stage_refs[2 * (di - 1)], stage_refs[2 * (di - 1) + 1]) if d > 1 else (None, None)
            os_.append(token_order(o_ref_d, d, o_st, nc))
            ls_.append(token_order(l_ref_d, d, l_st, 1))
        mx = functools.reduce(jnp.maximum, ls_)
        es = [jnp.exp(l - mx) for l in ls_]
        inv = 1.0 / sum(es)
        odil = sum(per_head_to_lanes(e * inv) * o for e, o in zip(es, os_))
        mixed = jnp.concatenate([
            _rms(ona_ref[rows, :].astype(F32), go[:, :W_NA]),
            _rms(odil, go[:, W_NA:W_NA + W_DIL]),
            _rms(om_ref[rows, :].astype(F32), go[:, W_NA + W_DIL:]),
        ], axis=-1).astype(BF16)
        hn = h_ref[rows, :] + jnp.dot(mixed, wb_ref[...], preferred_element_type=F32)
        hn_ref[rows, :] = hn
        u = _rms(hn, gf_ref[...])
        if with_router:
            u_ref[rows, :] = u
            u_hi = u.astype(BF16)
            u_lo = (u - u_hi.astype(F32)).astype(BF16)
            lg_ref[rows, :] = (jnp.dot(u_hi, wr_ref[...], preferred_element_type=F32)
                               + jnp.dot(u_lo, wr_ref[...], preferred_element_type=F32))
        else:
            u_ref[rows, :] = u.astype(BF16)

    th = tm // PROJ_CHAINS
    for ci in range(PROJ_CHAINS):
        chain(ci * th, th)


def _outproj(ona, dil, om, h, g_out, w_out, layer, g_ffn, w_router_pad, batch, seq_len):
    n = h.shape[0]
    tm = TM_PROJ
    tblocks = seq_len // tm
    with_router = w_router_pad is not None
    row = lambda w: pl.BlockSpec((tm, w), lambda i: (i, 0))
    full = lambda shape: pl.BlockSpec(shape, lambda i: (0,) * len(shape))
    in_specs = [row(W_NA)]
    args = [ona]
    scratch = []
    for di, d in enumerate(DILATIONS):
        for a, w in zip(dil[2 * di:2 * di + 2], (W_DIL, LANES)):
            if d == 1:
                in_specs.append(row(w))
                args.append(a.reshape(n, w))
            else:
                in_specs.append(pl.BlockSpec((None, d, tm // d, w),
                                             lambda i: (i // tblocks, 0, i % tblocks, 0)))
                args.append(a.reshape(batch, d, seq_len // d, w))
                scratch.append(pltpu.VMEM((w // LANES, tm, LANES), F32))
    f = _RESIDUE_DILS[0]
    scratch.append(pltpu.VMEM((W_DIL // LANES * f, tm // f, LANES), F32))
    scratch.append(pltpu.VMEM((D_MODEL, D_MODEL), BF16))
    in_specs += [row(W_MEM), row(D_MODEL), full((1, D_MODEL)),
                 pl.BlockSpec((None, D_MODEL, D_MODEL), lambda i: (layer, 0, 0)), full((1, D_MODEL))]
    args += [om, h, g_out, w_out, g_ffn]
    out_specs = [row(D_MODEL), row(D_MODEL)]
    out_shape = [jax.ShapeDtypeStruct((n, D_MODEL), F32),
                 jax.ShapeDtypeStruct((n, D_MODEL), F32 if with_router else BF16)]
    if with_router:
        in_specs.append(full((D_MODEL, LANES)))
        args.append(w_router_pad)
        out_specs.append(row(LANES))
        out_shape.append(jax.ShapeDtypeStruct((n, LANES), F32))
    return pl.pallas_call(
        functools.partial(_outproj_kernel, with_router=with_router),
        grid=(n // tm,),
        in_specs=in_specs,
        out_specs=out_specs,
        out_shape=out_shape,
        scratch_shapes=scratch,
        compiler_params=_cparams(("arbitrary",)),
        name="outproj",
    )(*args)


def _swiglu(x, wg, wu, wd):
    g = jnp.dot(x, wg, preferred_element_type=F32)
    u = jnp.dot(x, wu, preferred_element_type=F32)
    hm = (g * jax.nn.sigmoid(g) * u).astype(BF16)
    return jnp.dot(hm, wd, preferred_element_type=F32)


def _dense_ffn_kernel(u_ref, h_ref, wg_ref, wu_ref, wd_ref, o_ref, hm_ref):
    x = u_ref[...]
    for c in range(hm_ref.shape[1] // TF_DENSE):
        cols = slice(c * TF_DENSE, (c + 1) * TF_DENSE)
        g = jnp.dot(x, wg_ref[:, cols], preferred_element_type=F32)
        u = jnp.dot(x, wu_ref[:, cols], preferred_element_type=F32)
        hm_ref[:, cols] = (g * jax.nn.sigmoid(g) * u).astype(BF16)
    o_ref[...] = h_ref[...] + jnp.dot(hm_ref[...], wd_ref[...], preferred_element_type=F32)


def _dense_ffn(u, h, wg_bf16, wu_bf16, wd_bf16):
    n = h.shape[0]
    d_ff = wg_bf16.shape[1]
    tm = TM_FFN
    full = lambda shape: pl.BlockSpec(shape, lambda i: (0,) * len(shape))
    return pl.pallas_call(
        _dense_ffn_kernel,
        grid=(n // tm,),
        in_specs=[
            pl.BlockSpec((tm, D_MODEL), lambda i: (i, 0)),
            pl.BlockSpec((tm, D_MODEL), lambda i: (i, 0)),
            full((D_MODEL, d_ff)), full((D_MODEL, d_ff)), full((d_ff, D_MODEL)),
        ],
        out_specs=pl.BlockSpec((tm, D_MODEL), lambda i: (i, 0)),
        out_shape=jax.ShapeDtypeStruct((n, D_MODEL), F32),
        scratch_shapes=[pltpu.VMEM((tm, d_ff), BF16)],
        compiler_params=_cparams(("parallel",)),
        name="dense_ffn",
    )(u, h, wg_bf16, wu_bf16, wd_bf16)


def _dispatch_kernel(vend_ref, pend_ref, d0_ref, d1_ref, u_ref, x_hbm, zbuf, sem, zsem):
    i = pl.program_id(0)
    tc = u_ref.shape[0]
    zr = zbuf.shape[0]
    n_rows = x_hbm.shape[0]

    def zero_pieces(lo, hi):
        def start(p, carry):
            pltpu.make_async_copy(zbuf, x_hbm.at[pl.ds(pl.multiple_of(p * zr, zr), zr), :], zsem).start()
            return carry

        def wait(p, carry):
            pltpu.make_async_copy(zbuf, x_hbm.at[pl.ds(0, zr), :], zsem).wait()
            return carry

        lax.fori_loop(lo, hi, start, 0)
        lax.fori_loop(lo, hi, wait, 0)

    @pl.when(i == 0)
    def _():
        zbuf[...] = jnp.zeros_like(zbuf)
        for e in range(N_EXPERTS):
            zero_pieces(vend_ref[e] // zr, pend_ref[e] // zr)
        zero_pieces(pend_ref[N_EXPERTS - 1] // zr, n_rows // zr)

    def issue(r, carry):
        src = u_ref.at[pl.ds(r, 1), :]
        pltpu.make_async_copy(src, x_hbm.at[pl.ds(d0_ref[0, 0, r], 1), :], sem).start(priority=0)
        pltpu.make_async_copy(src, x_hbm.at[pl.ds(d1_ref[0, 0, r], 1), :], sem).start(priority=1)
        return carry

    lax.fori_loop(0, tc, issue, 0, unroll=ISSUE_UNROLL)
    for _ in range(TOP_K):
        pltpu.make_async_copy(u_ref, x_hbm.at[pl.ds(0, tc), :], sem).wait()


def _dispatch(u, dest, vend, pend, n_rows):
    n = u.shape[0]
    tc = TC_COMBINE
    nblk = n // tc
    d0 = dest[:, 0].reshape(nblk, 1, tc)
    d1 = dest[:, 1].reshape(nblk, 1, tc)
    smem = pl.BlockSpec((1, 1, tc), lambda i, ve, pe: (i, 0, 0), memory_space=pltpu.SMEM)
    grid_spec = pltpu.PrefetchScalarGridSpec(
        num_scalar_prefetch=2,
        grid=(nblk,),
        in_specs=[smem, smem, pl.BlockSpec((tc, D_MODEL), lambda i, ve, pe: (i, 0))],
        out_specs=pl.BlockSpec(memory_space=pl.ANY),
        scratch_shapes=[pltpu.VMEM((ZERO_ROWS, D_MODEL), F32), pltpu.SemaphoreType.DMA(()),
                        pltpu.SemaphoreType.DMA(())],
    )
    return pl.pallas_call(
        _dispatch_kernel,
        grid_spec=grid_spec,
        out_shape=jax.ShapeDtypeStruct((n_rows, D_MODEL), F32),
        compiler_params=_cparams(("arbitrary",)),
        name="moe_dispatch",
    )(vend, pend, d0, d1, u)


def _moe_ffn_kernel(be_ref, nv_ref, nu_ref, x_ref, wg_ref, wu_ref, wd_ref, y_ref, xb):
    i = pl.program_id(0)
    j = pl.program_id(1)
    active = i < nu_ref[0]
    tm = x_ref.shape[0]
    nchunk = tm // MOE_CHUNK

    def chunk(c, wg, wu, wd):
        rows = slice(c * MOE_CHUNK, (c + 1) * MOE_CHUNK)
        y_ref[rows, :] += _swiglu(xb[rows, :], wg, wu, wd)

    @pl.when(j == 0)
    def _():
        y_ref[...] = jnp.zeros_like(y_ref)

    @pl.when(active)
    def _():
        @pl.when(j == 0)
        def _():
            xb[...] = x_ref[...].astype(BF16)

        nvalid = nv_ref[i]

        @pl.when(nvalid > tm - MOE_CHUNK)
        def _():
            wg, wu, wd = (r[...].astype(BF16) for r in (wg_ref, wu_ref, wd_ref))
            for c in range(nchunk):
                chunk(c, wg, wu, wd)

        @pl.when(nvalid <= tm - MOE_CHUNK)
        def _():
            wg, wu, wd = (r[...].astype(BF16) for r in (wg_ref, wu_ref, wd_ref))
            for c in range(nchunk):
                @pl.when(c * MOE_CHUNK < nvalid)
                def _():
                    chunk(c, wg, wu, wd)


def _moe_ffn(x_sorted, blk_expert, blk_valid, n_used, wg, wu, wd):
    n_rows = x_sorted.shape[0]
    tm, tf = TM_MOE, TF_MOE
    nblk = n_rows // tm
    d_ff = wg.shape[2]
    nj = d_ff // tf

    def jeff(i, j, nu):
        return jnp.where(i < nu[0], j, nj - 1)

    grid_spec = pltpu.PrefetchScalarGridSpec(
        num_scalar_prefetch=3,
        grid=(nblk, nj),
        in_specs=[
            pl.BlockSpec((tm, D_MODEL), lambda i, j, be, nv, nu: (jnp.minimum(i, nu[0] - 1), 0)),
            pl.BlockSpec((None, D_MODEL, tf), lambda i, j, be, nv, nu: (be[i], 0, jeff(i, j, nu))),
            pl.BlockSpec((None, D_MODEL, tf), lambda i, j, be, nv, nu: (be[i], 0, jeff(i, j, nu))),
            pl.BlockSpec((None, tf, D_MODEL), lambda i, j, be, nv, nu: (be[i], jeff(i, j, nu), 0)),
        ],
        out_specs=pl.BlockSpec((tm, D_MODEL), lambda i, j, be, nv, nu: (i, 0)),
        scratch_shapes=[pltpu.VMEM((tm, D_MODEL), BF16)],
    )
    return pl.pallas_call(
        _moe_ffn_kernel,
        grid_spec=grid_spec,
        out_shape=jax.ShapeDtypeStruct((n_rows, D_MODEL), F32),
        compiler_params=_cparams(("arbitrary", "arbitrary")),
        name="moe_ffn",
    )(blk_expert, blk_valid, n_used, x_sorted, wg, wu, wd)


def _combine_kernel(p0_ref, p1_ref, h_ref, g_ref, y_hbm, o_ref, buf0, buf1, sem):
    tc = buf0.shape[0]

    def issue(r, carry):
        pltpu.make_async_copy(y_hbm.at[pl.ds(p0_ref[0, 0, r], 1), :], buf0.at[pl.ds(r, 1), :],
                              sem.at[0]).start(priority=0)
        pltpu.make_async_copy(y_hbm.at[pl.ds(p1_ref[0, 0, r], 1), :], buf1.at[pl.ds(r, 1), :],
                              sem.at[1]).start(priority=1)
        return carry

    lax.fori_loop(0, tc, issue, 0, unroll=ISSUE_UNROLL)
    pltpu.make_async_copy(y_hbm.at[pl.ds(0, tc), :], buf0, sem.at[0]).wait()
    pltpu.make_async_copy(y_hbm.at[pl.ds(0, tc), :], buf1, sem.at[1]).wait()
    g = g_ref[...]
    o_ref[...] = h_ref[...] + g[:, 0:1] * buf0[...] + g[:, 1:2] * buf1[...]


def _combine(h, y, dest, gates):
    n = h.shape[0]
    tc = TC_COMBINE
    nblk = n // tc
    p0 = dest[:, 0].reshape(nblk, 1, tc)
    p1 = dest[:, 1].reshape(nblk, 1, tc)
    smem = pl.BlockSpec((1, 1, tc), lambda i: (i, 0, 0), memory_space=pltpu.SMEM)
    return pl.pallas_call(
        _combine_kernel,
        grid=(nblk,),
        in_specs=[smem, smem, pl.BlockSpec((tc, D_MODEL), lambda i: (i, 0)),
                  pl.BlockSpec((tc, TOP_K), lambda i: (i, 0)), pl.BlockSpec(memory_space=pl.ANY)],
        out_specs=pl.BlockSpec((tc, D_MODEL), lambda i: (i, 0)),
        out_shape=jax.ShapeDtypeStruct((n, D_MODEL), F32),
        scratch_shapes=[pltpu.VMEM((tc, D_MODEL), F32), pltpu.VMEM((tc, D_MODEL), F32),
                        pltpu.SemaphoreType.DMA((2,))],
        compiler_params=_cparams(("arbitrary",)),
        name="moe_combine",
    )(p0, p1, h, gates, y)


def _route(logits):
    n = logits.shape[0]
    top_logit, top_idx = lax.top_k(logits, TOP_K)
    gates = jax.nn.softmax(top_logit, axis=-1)
    e_flat = top_idx.reshape(n * TOP_K).astype(jnp.int32)
    onehot = (e_flat[:, None] == jnp.arange(N_EXPERTS, dtype=jnp.int32)[None, :]).astype(jnp.int32)
    csum = jnp.cumsum(onehot, axis=0)
    counts = csum[-1]
    padded = (counts + TM_MOE - 1) // TM_MOE * TM_MOE
    pend = jnp.cumsum(padded)
    pstart = pend - padded
    dest = jnp.sum(onehot * (csum - 1 + pstart[None, :]), axis=1).reshape(n, TOP_K)
    n_rows = n * TOP_K + N_EXPERTS * TM_MOE
    nblk = n_rows // TM_MOE
    n_used = pend[-1] // TM_MOE
    blk_start = jnp.minimum(jnp.arange(nblk, dtype=jnp.int32) * TM_MOE, (n_used - 1) * TM_MOE)
    blk_expert = jnp.minimum(
        jnp.sum((blk_start[:, None] >= pend[None, :]).astype(jnp.int32), axis=1), N_EXPERTS - 1)
    blk_valid = jnp.clip(counts[blk_expert] - (blk_start - pstart[blk_expert]), 0, TM_MOE)
    i32 = lambda a: a.astype(jnp.int32)
    return (i32(dest), gates, i32(pstart + counts), i32(pend), i32(blk_expert), i32(blk_valid),
            i32(n_used).reshape(1), n_rows)


def _rope_tables(seq_len):
    half = HEAD_DIM // 2
    inv_freq = jnp.power(ROPE_THETA, -(2.0 / HEAD_DIM) * jnp.arange(half, dtype=F32))
    ang = jnp.arange(seq_len, dtype=F32)[:, None] * inv_freq[None, :]
    cos, sin = jnp.cos(ang), jnp.sin(ang)
    cos_t = jnp.tile(cos, (1, LANES // half))
    sin_t = jnp.tile(jnp.concatenate([-sin, sin], axis=1), (1, LANES // HEAD_DIM))
    return cos_t, sin_t


def kernel(x, mem, g_attn, w_in, g_qk_na, rpb_na, g_qk_dil, g_mem, w_mem_kv, g_qk_mem, g_out, w_out,
           g_ffn, w_gate_dense, w_up_dense, w_down_dense, w_router, w_gate_moe, w_up_moe, w_down_moe):
    batch, seq_len, _ = x.shape
    mem_len = mem.shape[1]
    depth = g_attn.shape[0]
    n = batch * seq_len
    scale = HEAD_DIM ** -0.5

    cos_t, sin_t = _rope_tables(seq_len)
    seg = jnp.asarray(np.kron(np.eye(2 * LANES // HEAD_DIM), np.ones((HEAD_DIM, HEAD_DIM))), BF16)
    band_masks = _band_masks()
    na_bias = _na_bias_tables(rpb_na, seq_len // GRID_W)
    ones = lambda w: jnp.ones((w,), F32)

    h = x.reshape(n, D_MODEL)
    mem2 = mem.reshape(batch * mem_len, D_MODEL)
    for layer in range(depth):
        gq_row = jnp.concatenate([
            jnp.tile(g_qk_na[layer, 0] * scale, H_NA), jnp.tile(g_qk_na[layer, 1], H_NA), ones(W_NA),
            jnp.tile(g_qk_dil[layer, 0] * scale, H_DIL), jnp.tile(g_qk_dil[layer, 1], H_DIL), ones(W_DIL),
            jnp.tile(g_qk_mem[layer, 0] * scale, H_MEM)]).reshape(1, IN_WIDTH)
        proj = _inproj(h, g_attn[layer].reshape(1, D_MODEL), w_in, layer, gq_row, cos_t, sin_t,
                       seg, batch, seq_len)
        qa, ka, va, qb, kb, vb, qm = proj[:_N_FLAT_OUT]

        dil = []
        for d in DILATIONS:
            if d == 1:
                qkv = [a.reshape(batch, seq_len, W_DIL) for a in (qb, kb, vb)]
            else:
                base = _N_FLAT_OUT + 3 * _RESIDUE_DILS.index(d)
                qkv = [a.reshape(batch * d, seq_len // d, W_DIL) for a in proj[base:base + 3]]
            dil += _band_attention(*qkv, band_masks)

        km, vm = _memkv(mem2, g_mem[layer].reshape(1, D_MODEL), w_mem_kv[layer].astype(BF16),
                        jnp.tile(g_qk_mem[layer, 1], LANES // HEAD_DIM).reshape(1, LANES), seg)
        o_na, o_mem = _na_mem_attention(qa, ka, va, na_bias, layer, qm, km, vm, batch, seq_len, mem_len)

        i = layer // 2
        moe = layer % 2 == 1
        w_router_pad = None
        if moe:
            wr_hi = w_router[i].astype(BF16)
            wr_lo = (w_router[i] - wr_hi.astype(F32)).astype(BF16)
            w_router_pad = jnp.pad(jnp.concatenate([wr_hi, wr_lo], axis=1),
                                   ((0, 0), (0, LANES - 2 * N_EXPERTS)))
        outs = _outproj(o_na, dil, o_mem, h, g_out[layer].reshape(1, D_MODEL), w_out, layer,
                        g_ffn[layer].reshape(1, D_MODEL), w_router_pad, batch, seq_len)
        if not moe:
            h, u = outs
            h = _dense_ffn(u, h, w_gate_dense[i].astype(BF16), w_up_dense[i].astype(BF16),
                           w_down_dense[i].astype(BF16))
        else:
            h, u, logits = outs
            dest, gates, vend, pend, blk_expert, blk_valid, n_used, n_rows = _route(
                logits[:, :N_EXPERTS] + logits[:, N_EXPERTS:2 * N_EXPERTS])
            x_sorted = _dispatch(u, dest, vend, pend, n_rows)
            y = _moe_ffn(x_sorted, blk_expert, blk_valid, n_used, w_gate_moe[i], w_up_moe[i], w_down_moe[i])
            h = _combine(h, y, dest, gates)
    return h.reshape(batch, seq_len, D_MODEL)
```

```python
import functools

import numpy as np
import jax
import jax.numpy as jnp
from jax import lax
from jax.experimental import pallas as pl
from jax.experimental.pallas import tpu as pltpu

F32 = jnp.float32
BF16 = jnp.bfloat16

D_MODEL = 1024
HEAD_DIM = 64
H_NA, H_DIL, H_MEM = 6, 6, 4
W_NA, W_DIL, W_MEM = H_NA * HEAD_DIM, H_DIL * HEAD_DIM, H_MEM * HEAD_DIM
IN_WIDTH = 3 * W_NA + 3 * W_DIL + W_MEM
GRID_W = 64
NA_KH, NA_KW = 8, 16
DIL_CFG = ((128, 1), (512, 4), (2048, 16))
ROPE_THETA = 10000.0
N_EXPERTS = 8
TOP_K = 2
RMS_EPS = 1e-6
NEG_INF = -1e30

LANES = 128
TM_PROJ = 512
PROJ_CHAINS = 2
NA_QROWS = 4
NA_QB = NA_QROWS * GRID_W
NA_KROWS = 12
NA_KB = NA_KROWS * GRID_W
NA_STEP_BLOCKS = 2
DIL_QB = 128
DIL_HALF = 64
DIL_KB = DIL_QB + 2 * DIL_HALF
DIL_STEP = 1024
TM_FFN = 512
TF_DENSE = 256
TM_MOE = 2048
TF_MOE = 512
MOE_CHUNK = 256
TC_COMBINE = 512
ZERO_ROWS = 256
ISSUE_UNROLL = 8
VMEM_LIMIT = 56 * 1024 * 1024


def _cparams(sem):
    return pltpu.CompilerParams(dimension_semantics=sem, vmem_limit_bytes=VMEM_LIMIT)


def _rms(x, g):
    return x * lax.rsqrt(jnp.mean(x * x, axis=-1, keepdims=True) + RMS_EPS) * g


_CHUNKS = (
    [(0, c, True, False) for c in range(3)] + [(1, c, True, False) for c in range(3)]
    + [(2, c, False, False) for c in range(3)]
    + [(3, c, True, True) for c in range(3)] + [(4, c, True, True) for c in range(3)]
    + [(5, c, False, False) for c in range(3)]
    + [(6, c, True, False) for c in range(2)]
)


DILATIONS = tuple(d for _, d in DIL_CFG)
_RESIDUE_DILS = tuple(d for d in DILATIONS if d > 1)
assert len(_RESIDUE_DILS) == 2 and _RESIDUE_DILS[1] == _RESIDUE_DILS[0] ** 2
_N_FLAT_OUT = 7


def _inproj_kernel(h_ref, ga_ref, w_ref, gq_ref, cos_ref, sin_ref, seg_ref, *rest):
    out_refs, stage_ref, stage2_ref, pa_ref, pb_ref = rest[:-4], rest[-4], rest[-3], rest[-2], rest[-1]
    i = pl.program_id(0)
    tm = h_ref.shape[0]
    seg = seg_ref[...]
    lane = lax.broadcasted_iota(jnp.int32, (1, LANES), 1)
    first_half = (lane % HEAD_DIM) < (HEAD_DIM // 2)
    f = _RESIDUE_DILS[0]

    @pl.when(i == 0)
    def _():
        pb_ref[...] = jnp.zeros_like(pb_ref)

    def step(p_new, p):
        u = _rms(h_ref[...], ga_ref[...]).astype(BF16)

        def project(k):
            cols = slice(k * 2 * LANES, (k + 1) * 2 * LANES)
            p_new[:, cols] = jnp.dot(u, w_ref[:, cols], preferred_element_type=F32)

        def finish(c, x):
            oi, oc, _, rope = _CHUNKS[c]
            cols = slice(oc * LANES, (oc + 1) * LANES)
            if rope:
                swapped = jnp.where(first_half, pltpu.roll(x, LANES - HEAD_DIM // 2, 1),
                                    pltpu.roll(x, HEAD_DIM // 2, 1))
                x = x * cos_ref[...] + swapped * sin_ref[...]
            out_refs[oi][:, cols] = x.astype(BF16)
            if 3 <= oi <= 5:
                st = (oi - 3) * (W_DIL // LANES) + oc
                dst1 = out_refs[_N_FLAT_OUT + (oi - 3)]
                dst2 = out_refs[_N_FLAT_OUT + 3 + (oi - 3)]
                stage_ref[st] = x
                for r in range(f):
                    xr = stage_ref[st, pl.ds(r, tm // f, stride=f), :]
                    dst1[r, :, cols] = xr.astype(BF16)
                    stage2_ref[st * f + r] = xr
                    for r2 in range(f):
                        dst2[r + f * r2, :, cols] = stage2_ref[
                            st * f + r, pl.ds(r2, tm // (f * f), stride=f), :].astype(BF16)

        def normed_pair(c):
            cols = slice(c * LANES, (c + 2) * LANES)
            x2 = p[:, cols]
            ss = jnp.dot((x2 * x2).astype(BF16), seg, preferred_element_type=F32)
            x2 = x2 * lax.rsqrt(ss * (1.0 / HEAD_DIM) + RMS_EPS) * gq_ref[:, cols]
            finish(c, x2[:, :LANES])
            finish(c + 1, x2[:, LANES:])

        b_units = []
        c = 0
        while c < len(_CHUNKS):
            if _CHUNKS[c][2]:
                assert _CHUNKS[c + 1][2]
                b_units.append(functools.partial(normed_pair, c))
                c += 2
            else:
                b_units.append(functools.partial(lambda c: finish(c, p[:, c * LANES:(c + 1) * LANES]), c))
                c += 1
        a_units = [functools.partial(project, k) for k in range(IN_WIDTH // (2 * LANES))]
        for k in range(max(len(a_units), len(b_units))):
            if k < len(a_units):
                a_units[k]()
            if k < len(b_units):
                b_units[k]()

    @pl.when(i % 2 == 0)
    def _():
        step(pa_ref, pb_ref)

    @pl.when(i % 2 == 1)
    def _():
        step(pb_ref, pa_ref)


def _inproj(h, g_attn, w_in_bf16, gq_row, cos_t, sin_t, seg, batch, seq_len):
    n = h.shape[0]
    tm = TM_PROJ
    nsteps = n // tm
    tblocks = seq_len // tm
    widths = (W_NA, W_NA, W_NA, W_DIL, W_DIL, W_DIL, W_MEM)
    nst, f = 3 * W_DIL // LANES, _RESIDUE_DILS[0]
    prev = lambda i: jnp.maximum(i - 1, 0)
    out_specs = [pl.BlockSpec((tm, w), lambda i: (prev(i), 0)) for w in widths]
    out_shape = [jax.ShapeDtypeStruct((n, w), BF16) for w in widths]
    for d in _RESIDUE_DILS:
        out_specs += [pl.BlockSpec((None, d, tm // d, W_DIL),
                                   lambda i: (prev(i) // tblocks, 0, prev(i) % tblocks, 0))] * 3
        out_shape += [jax.ShapeDtypeStruct((batch, d, seq_len // d, W_DIL), BF16)] * 3
    return pl.pallas_call(
        _inproj_kernel,
        grid=(nsteps + 1,),
        in_specs=[
            pl.BlockSpec((tm, D_MODEL), lambda i: (jnp.minimum(i, nsteps - 1), 0)),
            pl.BlockSpec((1, D_MODEL), lambda i: (0, 0)),
            pl.BlockSpec((D_MODEL, IN_WIDTH), lambda i: (0, 0)),
            pl.BlockSpec((1, IN_WIDTH), lambda i: (0, 0)),
            pl.BlockSpec((tm, LANES), lambda i: (prev(i) % tblocks, 0)),
            pl.BlockSpec((tm, LANES), lambda i: (prev(i) % tblocks, 0)),
            pl.BlockSpec((2 * LANES, 2 * LANES), lambda i: (0, 0)),
        ],
        out_specs=out_specs,
        out_shape=out_shape,
        scratch_shapes=[pltpu.VMEM((nst, tm, LANES), F32), pltpu.VMEM((nst * f, tm // f, LANES), F32),
                        pltpu.VMEM((tm, IN_WIDTH), F32), pltpu.VMEM((tm, IN_WIDTH), F32)],
        compiler_params=_cparams(("arbitrary",)),
        name="inproj",
    )(h, g_attn, w_in_bf16, gq_row, cos_t, sin_t, seg)


def _stack_heads(q):
    first = lax.broadcasted_iota(jnp.int32, (1, LANES), 1) < HEAD_DIM
    zero = jnp.zeros_like(q)
    return jnp.concatenate([jnp.where(first, q, zero), jnp.where(first, zero, q)], axis=0)


def _with_ones(v):
    return jnp.concatenate([v, jnp.ones_like(v)], axis=1)


def _attend_all(problems, consume, want_lse, skew):
    def logits(n):
        q, k, bias, v_ones = problems[n]()
        s = lax.dot_general(q, k, (((1,), (1,)), ((), ())), preferred_element_type=F32)
        return (s if bias is None else s + bias), v_ones

    ahead = logits(0) if skew else None
    for n in range(len(problems)):
        s, v_ones = ahead if skew else logits(n)
        if skew and n + 1 < len(problems):
            ahead = logits(n + 1)
        m = jnp.max(s, axis=-1, keepdims=True)
        p = jnp.exp(s - m).astype(BF16)
        ol = jnp.dot(p, v_ones, preferred_element_type=F32)
        l = ol[:, LANES:]
        consume(n, ol[:, :LANES] / l, (m + jnp.log(l)) if want_lse else None)


def _na_mem_kernel(q_ref, k_ref, v_ref, *rest, rows):
    bias_refs = rest[:NA_STEP_BLOCKS]
    qm_ref, km_ref, vm_ref, o_ref, om_ref = rest[NA_STEP_BLOCKS:]
    i = pl.program_id(1)
    first = lax.broadcasted_iota(jnp.int32, (1, LANES), 1) < HEAD_DIM
    n_tiles = W_NA // LANES
    n_mem = W_MEM // LANES

    def na_problem(sb, p, hh):
        srow = jnp.clip((i * NA_STEP_BLOCKS + sb) * NA_QROWS - NA_KH // 2, 0, rows - NA_KROWS)
        start = pl.multiple_of(srow * GRID_W, GRID_W)
        sl = slice(p * LANES, (p + 1) * LANES)
        q = q_ref[sb * NA_QB:(sb + 1) * NA_QB, sl]
        zero = jnp.zeros_like(q)
        q = jnp.where(first, q, zero) if hh == 0 else jnp.where(first, zero, q)
        return (q, k_ref[pl.ds(start, NA_KB), sl], bias_refs[sb][0, 2 * p + hh],
                _with_ones(v_ref[pl.ds(start, NA_KB), sl]))

    def mem_problem(sb, c):
        sl = slice(c * LANES, (c + 1) * LANES)
        return (_stack_heads(qm_ref[sb * NA_QB:(sb + 1) * NA_QB, sl]), km_ref[:, sl], None,
                _with_ones(vm_ref[:, sl]))

    problems, targets = [], []
    for sb in range(NA_STEP_BLOCKS):
        for p in range(n_tiles):
            for hh in range(2):
                problems.append(functools.partial(na_problem, sb, p, hh))
                targets.append(("na", sb, p, hh))
        for c in range(n_mem):
            problems.append(functools.partial(mem_problem, sb, c))
            targets.append(("mem", sb, c, None))
    held = {}

    def consume(n, o, _):
        kind, sb, t, hh = targets[n]
        qrows = slice(sb * NA_QB, (sb + 1) * NA_QB)
        sl = slice(t * LANES, (t + 1) * LANES)
        if kind == "mem":
            om_ref[qrows, sl] = jnp.where(first, o[:NA_QB], o[NA_QB:]).astype(BF16)
        elif hh == 0:
            held[(sb, t)] = o
        else:
            o_ref[qrows, sl] = jnp.where(first, held.pop((sb, t)), o).astype(BF16)

    _attend_all(problems, consume, False, skew=True)


def _na_mem_attention(q, k, v, bias, layer, qm, km, vm, batch, seq_len, mem_len):
    rows = seq_len // GRID_W
    nblk = seq_len // NA_QB
    step = NA_STEP_BLOCKS * NA_QB
    q3, k3, v3 = (a.reshape(batch, seq_len, W_NA) for a in (q, k, v))

    def bias_spec(sb):
        def index(b, i):
            blk = i * NA_STEP_BLOCKS + sb
            return (layer, (blk > 0).astype(jnp.int32) + (blk == nblk - 1).astype(jnp.int32), 0, 0, 0)
        return pl.BlockSpec((None, 1, H_NA, NA_QB, NA_KB), index)

    o, om = pl.pallas_call(
        functools.partial(_na_mem_kernel, rows=rows),
        grid=(batch, nblk // NA_STEP_BLOCKS),
        in_specs=[
            pl.BlockSpec((None, step, W_NA), lambda b, i: (b, i, 0)),
            pl.BlockSpec((None, seq_len, W_NA), lambda b, i: (b, 0, 0)),
            pl.BlockSpec((None, seq_len, W_NA), lambda b, i: (b, 0, 0)),
            *[bias_spec(sb) for sb in range(NA_STEP_BLOCKS)],
            pl.BlockSpec((None, step, W_MEM), lambda b, i: (b, i, 0)),
            pl.BlockSpec((None, mem_len, W_MEM), lambda b, i: (b, 0, 0)),
            pl.BlockSpec((None, mem_len, W_MEM), lambda b, i: (b, 0, 0)),
        ],
        out_specs=[pl.BlockSpec((None, step, W_NA), lambda b, i: (b, i, 0)),
                   pl.BlockSpec((None, step, W_MEM), lambda b, i: (b, i, 0))],
        out_shape=[jax.ShapeDtypeStruct((batch, seq_len, W_NA), BF16),
                   jax.ShapeDtypeStruct((batch, seq_len, W_MEM), BF16)],
        compiler_params=_cparams(("parallel", "arbitrary")),
        name="na_mem_attention",
    )(q3, k3, v3, *([bias] * NA_STEP_BLOCKS), qm.reshape(batch, seq_len, W_MEM),
      km.reshape(batch, mem_len, W_MEM), vm.reshape(batch, mem_len, W_MEM))
    return o.reshape(batch * seq_len, W_NA), om.reshape(batch * seq_len, W_MEM)


def _na_bias_tables(rpb, rows):
    depth = rpb.shape[0]
    nblk = rows // NA_QROWS
    qc = np.arange(GRID_W)[:, None]
    kc = np.arange(GRID_W)[None, :]
    cs = np.clip(qc - NA_KW // 2, 0, GRID_W - NA_KW)
    col_valid = (kc >= cs) & (kc < cs + NA_KW)
    col_rel = kc - qc + (NA_KW - 1)
    col_sel = (col_rel[None] == np.arange(2 * NA_KW - 1)[:, None, None]).astype(np.float32)
    t1 = jnp.einsum('lhab,bqk->lhaqk', rpb, jnp.asarray(col_sel), precision=lax.Precision.HIGHEST)
    t1 = jnp.where(jnp.asarray(col_valid), t1, NEG_INF)
    masked = jnp.full((depth, H_NA, GRID_W, GRID_W), NEG_INF, F32)
    tables = []
    for blk in (0, 1, nblk - 1):
        r0 = blk * NA_QROWS
        srow = int(np.clip(r0 - NA_KH // 2, 0, rows - NA_KROWS))
        slabs = []
        for qr in range(r0, r0 + NA_QROWS):
            rs = int(np.clip(qr - NA_KH // 2, 0, rows - NA_KH))
            slabs.append(jnp.concatenate(
                [t1[:, :, kr - qr + NA_KH - 1] if rs <= kr < rs + NA_KH else masked
                 for kr in range(srow, srow + NA_KROWS)], axis=-1))
        tables.append(jnp.stack(slabs, axis=2))
    return jnp.stack(tables, axis=1).reshape(depth, 3, H_NA, NA_QB, NA_KB)


def _band_kernel(q_ref, k_ref, v_ref, mask_ref, o_ref, lse_ref, *, length):
    j = pl.program_id(1)
    step = q_ref.shape[0]
    n_tiles = W_DIL // LANES
    first = lax.broadcasted_iota(jnp.int32, (1, LANES), 1) < HEAD_DIM
    lane = lax.broadcasted_iota(jnp.int32, (1, LANES), 1)

    def problem(sb, p):
        q0 = j * step + sb * DIL_QB
        start = pl.multiple_of(jnp.clip(q0 - DIL_HALF, 0, length - DIL_KB), DIL_HALF)
        variant = (q0 > 0).astype(jnp.int32) + (q0 == length - DIL_QB).astype(jnp.int32)
        mask = mask_ref[variant]
        sl = slice(p * LANES, (p + 1) * LANES)
        return (_stack_heads(q_ref[sb * DIL_QB:(sb + 1) * DIL_QB, sl]), k_ref[pl.ds(start, DIL_KB), sl],
                jnp.concatenate([mask, mask], axis=0), _with_ones(v_ref[pl.ds(start, DIL_KB), sl]))

    problems = [functools.partial(problem, sb, p) for sb in range(step // DIL_QB) for p in range(n_tiles)]
    lse_heads = {}

    def consume(n, o, lse):
        sb, p = divmod(n, n_tiles)
        rows = slice(sb * DIL_QB, (sb + 1) * DIL_QB)
        o_ref[rows, p * LANES:(p + 1) * LANES] = jnp.where(first, o[:DIL_QB], o[DIL_QB:]).astype(BF16)
        acc = lse_heads.pop(sb, jnp.zeros((DIL_QB, LANES), F32))
        acc = jnp.where(lane == 2 * p, lse[:DIL_QB], jnp.where(lane == 2 * p + 1, lse[DIL_QB:], acc))
        if p == n_tiles - 1:
            lse_ref[rows, :] = acc
        else:
            lse_heads[sb] = acc

    _attend_all(problems, consume, True, skew=False)


def _band_attention(q, k, v, masks):
    nseq, length, _ = q.shape
    step = min(DIL_STEP, length)
    nblk = length // step
    blk = pl.BlockSpec((None, step, W_DIL), lambda s, j: (s, j, 0))
    seq = pl.BlockSpec((None, length, W_DIL), lambda s, j: (s, 0, 0))
    return pl.pallas_call(
        functools.partial(_band_kernel, length=length),
        grid=(nseq, nblk),
        in_specs=[blk, seq, seq, pl.BlockSpec((3, DIL_QB, DIL_KB), lambda s, j: (0, 0, 0))],
        out_specs=[blk, pl.BlockSpec((None, step, LANES), lambda s, j: (s, j, 0))],
        out_shape=[jax.ShapeDtypeStruct((nseq, length, W_DIL), BF16),
                   jax.ShapeDtypeStruct((nseq, length, LANES), F32)],
        compiler_params=_cparams(("parallel", "arbitrary")),
        name="band_attention",
    )(q, k, v, masks)


def _band_masks():
    qq = np.arange(DIL_QB)[:, None]
    kk = np.arange(DIL_KB)[None, :]
    out = []
    for off in (0, -DIL_HALF, DIL_QB - DIL_KB):
        rel = kk + off - qq
        out.append(np.where(np.abs(rel) <= DIL_HALF, 0.0, NEG_INF))
    return jnp.asarray(np.stack(out), F32)


def _memkv_kernel(mem_ref, g_ref, w_ref, gk_ref, seg_ref, km_ref, vm_ref):
    mn = _rms(mem_ref[...], g_ref[...]).astype(BF16)
    kv = jnp.dot(mn, w_ref[...], preferred_element_type=F32)
    for c in range(W_MEM // LANES):
        x = kv[:, c * LANES:(c + 1) * LANES]
        ss = jnp.dot((x * x).astype(BF16), seg_ref[:LANES, :LANES], preferred_element_type=F32)
        x = x * lax.rsqrt(ss * (1.0 / HEAD_DIM) + RMS_EPS) * gk_ref[...]
        km_ref[:, c * LANES:(c + 1) * LANES] = x.astype(BF16)
    vm_ref[...] = kv[:, W_MEM:].astype(BF16)


def _memkv(mem2, g_mem, w_kv_bf16, gk_row, seg):
    rows = mem2.shape[0]
    full = lambda shape: pl.BlockSpec(shape, lambda i: (0,) * len(shape))
    return pl.pallas_call(
        _memkv_kernel,
        grid=(1,),
        in_specs=[full((rows, D_MODEL)), full((1, D_MODEL)), full((D_MODEL, 2 * W_MEM)),
                  full((1, LANES)), full((2 * LANES, 2 * LANES))],
        out_specs=[full((rows, W_MEM)), full((rows, W_MEM))],
        out_shape=[jax.ShapeDtypeStruct((rows, W_MEM), BF16)] * 2,
        compiler_params=_cparams(("arbitrary",)),
        name="mem_kv",
    )(mem2, g_mem, w_kv_bf16, gk_row, seg)


def _outproj_kernel(ona_ref, *rest, with_router):
    nd = 2 * len(DILATIONS)
    dil_refs, rest = rest[:nd], rest[nd:]
    om_ref, h_ref, go_ref, w_ref, gf_ref = rest[:5]
    rest = rest[5:]
    if with_router:
        wr_ref, hn_ref, u_ref, lg_ref = rest[:4]
        stage_refs = rest[4:]
    else:
        hn_ref, u_ref = rest[:2]
        stage_refs = rest[2:]
    stage_refs, tmp_ref = stage_refs[:-1], stage_refs[-1]
    tm = h_ref.shape[0]
    f = _RESIDUE_DILS[0]
    nc = W_DIL // LANES
    go = go_ref[...]
    spread = (lax.broadcasted_iota(jnp.int32, (LANES, W_DIL), 0)
              == lax.broadcasted_iota(jnp.int32, (LANES, W_DIL), 1) // HEAD_DIM).astype(BF16)

    def chain(r0, th):
        rows = slice(r0, r0 + th)

        def token_order(ref, d, st, ncols):
            if d == 1:
                return ref[rows, :].astype(F32)
            for c in range(ncols):
                cols = slice(c * LANES, (c + 1) * LANES)
                for r in range(f):
                    if d == f:
                        st[c, pl.ds(r0 + r, th // f, stride=f), :] = ref[
                            r, r0 // f:(r0 + th) // f, cols].astype(F32)
                    else:
                        for r2 in range(f):
                            tmp_ref[c * f + r, pl.ds(r0 // f + r2, th // d, stride=f), :] = ref[
                                r + f * r2, r0 // d:(r0 + th) // d, cols].astype(F32)
                        st[c, pl.ds(r0 + r, th // f, stride=f), :] = tmp_ref[
                            c * f + r, r0 // f:(r0 + th) // f, :]
            return jnp.concatenate([st[c, rows, :] for c in range(ncols)], axis=-1)

        def per_head_to_lanes(w):
            w_hi = w.astype(BF16)
            w_lo = (w - w_hi.astype(F32)).astype(BF16)
            return (jnp.dot(w_hi, spread, preferred_element_type=F32)
                    + jnp.dot(w_lo, spread, preferred_element_type=F32))

        os_, ls_ = [], []
        for di, d in enumerate(DILATIONS):
            o_ref_d, l_ref_d = dil_refs[2 * di:2 * di + 2]
            o_st, l_st = (stage_refs[2 * (di - 1)], stage_refs[2 * (di - 1) + 1]) if d > 1 else (None, None)
            os_.append(token_order(o_ref_d, d, o_st, nc))
            ls_.append(token_order(l_ref_d, d, l_st, 1))
        mx = functools.reduce(jnp.maximum, ls_)
        es = [jnp.exp(l - mx) for l in ls_]
        inv = 1.0 / sum(es)
        odil = sum(per_head_to_lanes(e * inv) * o for e, o in zip(es, os_))
        mixed = jnp.concatenate([
            _rms(ona_ref[rows, :].astype(F32), go[:, :W_NA]),
            _rms(odil, go[:, W_NA:W_NA + W_DIL]),
            _rms(om_ref[rows, :].astype(F32), go[:, W_NA + W_DIL:]),
        ], axis=-1).astype(BF16)
        hn = h_ref[rows, :] + jnp.dot(mixed, w_ref[...], preferred_element_type=F32)
        hn_ref[rows, :] = hn
        u = _rms(hn, gf_ref[...])
        if with_router:
            u_ref[rows, :] = u
            u_hi = u.astype(BF16)
            u_lo = (u - u_hi.astype(F32)).astype(BF16)
            lg_ref[rows, :] = (jnp.dot(u_hi, wr_ref[...], preferred_element_type=F32)
                               + jnp.dot(u_lo, wr_ref[...], preferred_element_type=F32))
        else:
            u_ref[rows, :] = u.astype(BF16)

    th = tm // PROJ_CHAINS
    for ci in range(PROJ_CHAINS):
        chain(ci * th, th)


def _outproj(ona, dil, om, h, g_out, w_out_bf16, g_ffn, w_router_pad, batch, seq_len):
    n = h.shape[0]
    tm = TM_PROJ
    tblocks = seq_len // tm
    with_router = w_router_pad is not None
    row = lambda w: pl.BlockSpec((tm, w), lambda i: (i, 0))
    full = lambda shape: pl.BlockSpec(shape, lambda i: (0,) * len(shape))
    in_specs = [row(W_NA)]
    args = [ona]
    scratch = []
    for di, d in enumerate(DILATIONS):
        for a, w in zip(dil[2 * di:2 * di + 2], (W_DIL, LANES)):
            if d == 1:
                in_specs.append(row(w))
                args.append(a.reshape(n, w))
            else:
                in_specs.append(pl.BlockSpec((None, d, tm // d, w),
                                             lambda i: (i // tblocks, 0, i % tblocks, 0)))
                args.append(a.reshape(batch, d, seq_len // d, w))
                scratch.append(pltpu.VMEM((w // LANES, tm, LANES), F32))
    f = _RESIDUE_DILS[0]
    scratch.append(pltpu.VMEM((W_DIL // LANES * f, tm // f, LANES), F32))
    in_specs += [row(W_MEM), row(D_MODEL), full((1, D_MODEL)), full((D_MODEL, D_MODEL)),
                 full((1, D_MODEL))]
    args += [om, h, g_out, w_out_bf16, g_ffn]
    out_specs = [row(D_MODEL), row(D_MODEL)]
    out_shape = [jax.ShapeDtypeStruct((n, D_MODEL), F32),
                 jax.ShapeDtypeStruct((n, D_MODEL), F32 if with_router else BF16)]
    if with_router:
        in_specs.append(full((D_MODEL, LANES)))
        args.append(w_router_pad)
        out_specs.append(row(LANES))
        out_shape.append(jax.ShapeDtypeStruct((n, LANES), F32))
    return pl.pallas_call(
        functools.partial(_outproj_kernel, with_router=with_router),
        grid=(n // tm,),
        in_specs=in_specs,
        out_specs=out_specs,
        out_shape=out_shape,
        scratch_shapes=scratch,
        compiler_params=_cparams(("parallel",)),
        name="outproj",
    )(*args)


def _swiglu(x, wg, wu, wd):
    g = jnp.dot(x, wg, preferred_element_type=F32)
    u = jnp.dot(x, wu, preferred_element_type=F32)
    hm = (g * jax.nn.sigmoid(g) * u).astype(BF16)
    return jnp.dot(hm, wd, preferred_element_type=F32)


def _dense_ffn_kernel(u_ref, h_ref, wg_ref, wu_ref, wd_ref, o_ref, hm_ref):
    x = u_ref[...]
    for c in range(hm_ref.shape[1] // TF_DENSE):
        cols = slice(c * TF_DENSE, (c + 1) * TF_DENSE)
        g = jnp.dot(x, wg_ref[:, cols], preferred_element_type=F32)
        u = jnp.dot(x, wu_ref[:, cols], preferred_element_type=F32)
        hm_ref[:, cols] = (g * jax.nn.sigmoid(g) * u).astype(BF16)
    o_ref[...] = h_ref[...] + jnp.dot(hm_ref[...], wd_ref[...], preferred_element_type=F32)


def _dense_ffn(u, h, wg_bf16, wu_bf16, wd_bf16):
    n = h.shape[0]
    d_ff = wg_bf16.shape[1]
    tm = TM_FFN
    full = lambda shape: pl.BlockSpec(shape, lambda i: (0,) * len(shape))
    return pl.pallas_call(
        _dense_ffn_kernel,
        grid=(n // tm,),
        in_specs=[
            pl.BlockSpec((tm, D_MODEL), lambda i: (i, 0)),
            pl.BlockSpec((tm, D_MODEL), lambda i: (i, 0)),
            full((D_MODEL, d_ff)), full((D_MODEL, d_ff)), full((d_ff, D_MODEL)),
        ],
        out_specs=pl.BlockSpec((tm, D_MODEL), lambda i: (i, 0)),
        out_shape=jax.ShapeDtypeStruct((n, D_MODEL), F32),
        scratch_shapes=[pltpu.VMEM((tm, d_ff), BF16)],
        compiler_params=_cparams(("parallel",)),
        name="dense_ffn",
    )(u, h, wg_bf16, wu_bf16, wd_bf16)


def _dispatch_kernel(vend_ref, pend_ref, d0_ref, d1_ref, u_ref, x_hbm, zbuf, sem, zsem):
    i = pl.program_id(0)
    tc = u_ref.shape[0]
    zr = zbuf.shape[0]
    n_rows = x_hbm.shape[0]

    def zero_pieces(lo, hi):
        def start(p, carry):
            pltpu.make_async_copy(zbuf, x_hbm.at[pl.ds(pl.multiple_of(p * zr, zr), zr), :], zsem).start()
            return carry

        def wait(p, carry):
            pltpu.make_async_copy(zbuf, x_hbm.at[pl.ds(0, zr), :], zsem).wait()
            return carry

        lax.fori_loop(lo, hi, start, 0)
        lax.fori_loop(lo, hi, wait, 0)

    @pl.when(i == 0)
    def _():
        zbuf[...] = jnp.zeros_like(zbuf)
        for e in range(N_EXPERTS):
            zero_pieces(vend_ref[e] // zr, pend_ref[e] // zr)
        zero_pieces(pend_ref[N_EXPERTS - 1] // zr, n_rows // zr)

    def issue(r, carry):
        src = u_ref.at[pl.ds(r, 1), :]
        pltpu.make_async_copy(src, x_hbm.at[pl.ds(d0_ref[0, 0, r], 1), :], sem).start(priority=0)
        pltpu.make_async_copy(src, x_hbm.at[pl.ds(d1_ref[0, 0, r], 1), :], sem).start(priority=1)
        return carry

    lax.fori_loop(0, tc, issue, 0, unroll=ISSUE_UNROLL)
    for _ in range(TOP_K):
        pltpu.make_async_copy(u_ref, x_hbm.at[pl.ds(0, tc), :], sem).wait()


def _dispatch(u, dest, vend, pend, n_rows):
    n = u.shape[0]
    tc = TC_COMBINE
    nblk = n // tc
    d0 = dest[:, 0].reshape(nblk, 1, tc)
    d1 = dest[:, 1].reshape(nblk, 1, tc)
    smem = pl.BlockSpec((1, 1, tc), lambda i, ve, pe: (i, 0, 0), memory_space=pltpu.SMEM)
    grid_spec = pltpu.PrefetchScalarGridSpec(
        num_scalar_prefetch=2,
        grid=(nblk,),
        in_specs=[smem, smem, pl.BlockSpec((tc, D_MODEL), lambda i, ve, pe: (i, 0))],
        out_specs=pl.BlockSpec(memory_space=pl.ANY),
        scratch_shapes=[pltpu.VMEM((ZERO_ROWS, D_MODEL), F32), pltpu.SemaphoreType.DMA(()),
                        pltpu.SemaphoreType.DMA(())],
    )
    return pl.pallas_call(
        _dispatch_kernel,
        grid_spec=grid_spec,
        out_shape=jax.ShapeDtypeStruct((n_rows, D_MODEL), F32),
        compiler_params=_cparams(("arbitrary",)),
        name="moe_dispatch",
    )(vend, pend, d0, d1, u)


def _moe_ffn_kernel(be_ref, nv_ref, nu_ref, x_ref, wg_ref, wu_ref, wd_ref, y_ref, xb):
    i = pl.program_id(0)
    j = pl.program_id(1)
    active = i < nu_ref[0]
    tm = x_ref.shape[0]
    nchunk = tm // MOE_CHUNK

    def chunk(c, wg, wu, wd):
        rows = slice(c * MOE_CHUNK, (c + 1) * MOE_CHUNK)
        y_ref[rows, :] += _swiglu(xb[rows, :], wg, wu, wd)

    @pl.when(j == 0)
    def _():
        y_ref[...] = jnp.zeros_like(y_ref)

    @pl.when(active)
    def _():
        @pl.when(j == 0)
        def _():
            xb[...] = x_ref[...].astype(BF16)

        nvalid = nv_ref[i]

        @pl.when(nvalid > tm - MOE_CHUNK)
        def _():
            wg, wu, wd = (r[...].astype(BF16) for r in (wg_ref, wu_ref, wd_ref))
            for c in range(nchunk):
                chunk(c, wg, wu, wd)

        @pl.when(nvalid <= tm - MOE_CHUNK)
        def _():
            wg, wu, wd = (r[...].astype(BF16) for r in (wg_ref, wu_ref, wd_ref))
            for c in range(nchunk):
                @pl.when(c * MOE_CHUNK < nvalid)
                def _():
                    chunk(c, wg, wu, wd)


def _moe_ffn(x_sorted, blk_expert, blk_valid, n_used, wg, wu, wd):
    n_rows = x_sorted.shape[0]
    tm, tf = TM_MOE, TF_MOE
    nblk = n_rows // tm
    d_ff = wg.shape[2]
    nj = d_ff // tf

    def jeff(i, j, nu):
        return jnp.where(i < nu[0], j, nj - 1)

    grid_spec = pltpu.PrefetchScalarGridSpec(
        num_scalar_prefetch=3,
        grid=(nblk, nj),
        in_specs=[
            pl.BlockSpec((tm, D_MODEL), lambda i, j, be, nv, nu: (jnp.minimum(i, nu[0] - 1), 0)),
            pl.BlockSpec((None, D_MODEL, tf), lambda i, j, be, nv, nu: (be[i], 0, jeff(i, j, nu))),
            pl.BlockSpec((None, D_MODEL, tf), lambda i, j, be, nv, nu: (be[i], 0, jeff(i, j, nu))),
            pl.BlockSpec((None, tf, D_MODEL), lambda i, j, be, nv, nu: (be[i], jeff(i, j, nu), 0)),
        ],
        out_specs=pl.BlockSpec((tm, D_MODEL), lambda i, j, be, nv, nu: (i, 0)),
        scratch_shapes=[pltpu.VMEM((tm, D_MODEL), BF16)],
    )
    return pl.pallas_call(
        _moe_ffn_kernel,
        grid_spec=grid_spec,
        out_shape=jax.ShapeDtypeStruct((n_rows, D_MODEL), F32),
        compiler_params=_cparams(("arbitrary", "arbitrary")),
        name="moe_ffn",
    )(blk_expert, blk_valid, n_used, x_sorted, wg, wu, wd)


def _combine_kernel(p0_ref, p1_ref, h_ref, g_ref, y_hbm, o_ref, buf0, buf1, sem):
    tc = buf0.shape[0]

    def issue(r, carry):
        pltpu.make_async_copy(y_hbm.at[pl.ds(p0_ref[0, 0, r], 1), :], buf0.at[pl.ds(r, 1), :],
                              sem.at[0]).start(priority=0)
        pltpu.make_async_copy(y_hbm.at[pl.ds(p1_ref[0, 0, r], 1), :], buf1.at[pl.ds(r, 1), :],
                              sem.at[1]).start(priority=1)
        return carry

    lax.fori_loop(0, tc, issue, 0, unroll=ISSUE_UNROLL)
    pltpu.make_async_copy(y_hbm.at[pl.ds(0, tc), :], buf0, sem.at[0]).wait()
    pltpu.make_async_copy(y_hbm.at[pl.ds(0, tc), :], buf1, sem.at[1]).wait()
    g = g_ref[...]
    o_ref[...] = h_ref[...] + g[:, 0:1] * buf0[...] + g[:, 1:2] * buf1[...]


def _combine(h, y, dest, gates):
    n = h.shape[0]
    tc = TC_COMBINE
    nblk = n // tc
    p0 = dest[:, 0].reshape(nblk, 1, tc)
    p1 = dest[:, 1].reshape(nblk, 1, tc)
    smem = pl.BlockSpec((1, 1, tc), lambda i: (i, 0, 0), memory_space=pltpu.SMEM)
    return pl.pallas_call(
        _combine_kernel,
        grid=(nblk,),
        in_specs=[smem, smem, pl.BlockSpec((tc, D_MODEL), lambda i: (i, 0)),
                  pl.BlockSpec((tc, TOP_K), lambda i: (i, 0)), pl.BlockSpec(memory_space=pl.ANY)],
        out_specs=pl.BlockSpec((tc, D_MODEL), lambda i: (i, 0)),
        out_shape=jax.ShapeDtypeStruct((n, D_MODEL), F32),
        scratch_shapes=[pltpu.VMEM((tc, D_MODEL), F32), pltpu.VMEM((tc, D_MODEL), F32),
                        pltpu.SemaphoreType.DMA((2,))],
        compiler_params=_cparams(("arbitrary",)),
        name="moe_combine",
    )(p0, p1, h, gates, y)


def _route(logits):
    n = logits.shape[0]
    top_logit, top_idx = lax.top_k(logits, TOP_K)
    gates = jax.nn.softmax(top_logit, axis=-1)
    e_flat = top_idx.reshape(n * TOP_K).astype(jnp.int32)
    onehot = (e_flat[:, None] == jnp.arange(N_EXPERTS, dtype=jnp.int32)[None, :]).astype(jnp.int32)
    csum = jnp.cumsum(onehot, axis=0)
    counts = csum[-1]
    padded = (counts + TM_MOE - 1) // TM_MOE * TM_MOE
    pend = jnp.cumsum(padded)
    pstart = pend - padded
    dest = jnp.sum(onehot * (csum - 1 + pstart[None, :]), axis=1).reshape(n, TOP_K)
    n_rows = n * TOP_K + N_EXPERTS * TM_MOE
    nblk = n_rows // TM_MOE
    n_used = pend[-1] // TM_MOE
    blk_start = jnp.minimum(jnp.arange(nblk, dtype=jnp.int32) * TM_MOE, (n_used - 1) * TM_MOE)
    blk_expert = jnp.minimum(
        jnp.sum((blk_start[:, None] >= pend[None, :]).astype(jnp.int32), axis=1), N_EXPERTS - 1)
    blk_valid = jnp.clip(counts[blk_expert] - (blk_start - pstart[blk_expert]), 0, TM_MOE)
    i32 = lambda a: a.astype(jnp.int32)
    return (i32(dest), gates, i32(pstart + counts), i32(pend), i32(blk_expert), i32(blk_valid),
            i32(n_used).reshape(1), n_rows)


def _rope_tables(seq_len):
    half = HEAD_DIM // 2
    inv_freq = jnp.power(ROPE_THETA, -(2.0 / HEAD_DIM) * jnp.arange(half, dtype=F32))
    ang = jnp.arange(seq_len, dtype=F32)[:, None] * inv_freq[None, :]
    cos, sin = jnp.cos(ang), jnp.sin(ang)
    cos_t = jnp.tile(cos, (1, LANES // half))
    sin_t = jnp.tile(jnp.concatenate([-sin, sin], axis=1), (1, LANES // HEAD_DIM))
    return cos_t, sin_t


def kernel(x, mem, g_attn, w_in, g_qk_na, rpb_na, g_qk_dil, g_mem, w_mem_kv, g_qk_mem, g_out, w_out,
           g_ffn, w_gate_dense, w_up_dense, w_down_dense, w_router, w_gate_moe, w_up_moe, w_down_moe):
    batch, seq_len, _ = x.shape
    mem_len = mem.shape[1]
    depth = g_attn.shape[0]
    n = batch * seq_len
    scale = HEAD_DIM ** -0.5

    cos_t, sin_t = _rope_tables(seq_len)
    seg = jnp.asarray(np.kron(np.eye(2 * LANES // HEAD_DIM), np.ones((HEAD_DIM, HEAD_DIM))), BF16)
    band_masks = _band_masks()
    na_bias = _na_bias_tables(rpb_na, seq_len // GRID_W)
    ones = lambda w: jnp.ones((w,), F32)

    h = x.reshape(n, D_MODEL)
    mem2 = mem.reshape(batch * mem_len, D_MODEL)
    for layer in range(depth):
        gq_row = jnp.concatenate([
            jnp.tile(g_qk_na[layer, 0] * scale, H_NA), jnp.tile(g_qk_na[layer, 1], H_NA), ones(W_NA),
            jnp.tile(g_qk_dil[layer, 0] * scale, H_DIL), jnp.tile(g_qk_dil[layer, 1], H_DIL), ones(W_DIL),
            jnp.tile(g_qk_mem[layer, 0] * scale, H_MEM)]).reshape(1, IN_WIDTH)
        proj = _inproj(h, g_attn[layer].reshape(1, D_MODEL), w_in[layer].astype(BF16), gq_row, cos_t, sin_t,
                       seg, batch, seq_len)
        qa, ka, va, qb, kb, vb, qm = proj[:_N_FLAT_OUT]

        dil = []
        for d in DILATIONS:
            if d == 1:
                qkv = [a.reshape(batch, seq_len, W_DIL) for a in (qb, kb, vb)]
            else:
                base = _N_FLAT_OUT + 3 * _RESIDUE_DILS.index(d)
                qkv = [a.reshape(batch * d, seq_len // d, W_DIL) for a in proj[base:base + 3]]
            dil += _band_attention(*qkv, band_masks)

        km, vm = _memkv(mem2, g_mem[layer].reshape(1, D_MODEL), w_mem_kv[layer].astype(BF16),
                        jnp.tile(g_qk_mem[layer, 1], LANES // HEAD_DIM).reshape(1, LANES), seg)
        o_na, o_mem = _na_mem_attention(qa, ka, va, na_bias, layer, qm, km, vm, batch, seq_len, mem_len)

        i = layer // 2
        moe = layer % 2 == 1
        w_router_pad = None
        if moe:
            wr_hi = w_router[i].astype(BF16)
            wr_lo = (w_router[i] - wr_hi.astype(F32)).astype(BF16)
            w_router_pad = jnp.pad(jnp.concatenate([wr_hi, wr_lo], axis=1),
                                   ((0, 0), (0, LANES - 2 * N_EXPERTS)))
        outs = _outproj(o_na, dil, o_mem, h, g_out[layer].reshape(1, D_MODEL), w_out[layer].astype(BF16),
                        g_ffn[layer].reshape(1, D_MODEL), w_router_pad, batch, seq_len)
        if not moe:
            h, u = outs
            h = _dense_ffn(u, h, w_gate_dense[i].astype(BF16), w_up_dense[i].astype(BF16),
                           w_down_dense[i].astype(BF16))
        else:
            h, u, logits = outs
            dest, gates, vend, pend, blk_expert, blk_valid, n_used, n_rows = _route(
                logits[:, :N_EXPERTS] + logits[:, N_EXPERTS:2 * N_EXPERTS])
            x_sorted = _dispatch(u, dest, vend, pend, n_rows)
            y = _moe_ffn(x_sorted, blk_expert, blk_valid, n_used, w_gate_moe[i], w_up_moe[i], w_down_moe[i])
            h = _combine(h, y, dest, gates)
    return h.reshape(batch, seq_len, D_MODEL)
```

```python
import functools

import numpy as np
import jax
import jax.numpy as jnp
from jax import lax
from jax.experimental import pallas as pl
from jax.experimental.pallas import tpu as pltpu

F32 = jnp.float32
BF16 = jnp.bfloat16

D_MODEL = 1024
HEAD_DIM = 64
H_NA, H_DIL, H_MEM = 6, 6, 4
W_NA, W_DIL, W_MEM = H_NA * HEAD_DIM, H_DIL * HEAD_DIM, H_MEM * HEAD_DIM
IN_WIDTH = 3 * W_NA + 3 * W_DIL + W_MEM
GRID_W = 64
NA_KH, NA_KW = 8, 16
DIL_CFG = ((128, 1), (512, 4), (2048, 16))
ROPE_THETA = 10000.0
N_EXPERTS = 8
TOP_K = 2
RMS_EPS = 1e-6
NEG_INF = -1e30

LANES = 128
TM_PROJ = 512
PROJ_CHAINS = 2
NA_QROWS = 4
NA_QB = NA_QROWS * GRID_W
NA_KROWS = 12
NA_KB = NA_KROWS * GRID_W
NA_STEP_BLOCKS = 2
DIL_QB = 128
DIL_HALF = 64
DIL_KB = DIL_QB + 2 * DIL_HALF
DIL_STEP = 1024
TM_FFN = 512
TF_DENSE = 256
TM_MOE = 2048
TF_MOE = 512
MOE_CHUNK = 256
TC_COMBINE = 512
ZERO_ROWS = 256
ISSUE_UNROLL = 8
VMEM_LIMIT = 56 * 1024 * 1024


def _cparams(sem):
    return pltpu.CompilerParams(dimension_semantics=sem, vmem_limit_bytes=VMEM_LIMIT)


def _rms(x, g):
    return x * lax.rsqrt(jnp.mean(x * x, axis=-1, keepdims=True) + RMS_EPS) * g


_CHUNKS = (
    [(0, c, True, False) for c in range(3)] + [(1, c, True, False) for c in range(3)]
    + [(2, c, False, False) for c in range(3)]
    + [(3, c, True, True) for c in range(3)] + [(4, c, True, True) for c in range(3)]
    + [(5, c, False, False) for c in range(3)]
    + [(6, c, True, False) for c in range(2)]
)


DILATIONS = tuple(d for _, d in DIL_CFG)
_RESIDUE_DILS = tuple(d for d in DILATIONS if d > 1)
assert len(_RESIDUE_DILS) == 2 and _RESIDUE_DILS[1] == _RESIDUE_DILS[0] ** 2
_N_FLAT_OUT = 7


def _inproj_kernel(h_ref, ga_ref, w_ref, gq_ref, cos_ref, sin_ref, seg_ref, *rest):
    out_refs, stage_ref, stage2_ref, pa_ref, pb_ref = rest[:-4], rest[-4], rest[-3], rest[-2], rest[-1]
    i = pl.program_id(0)
    tm = h_ref.shape[0]
    seg = seg_ref[...]
    lane = lax.broadcasted_iota(jnp.int32, (1, LANES), 1)
    first_half = (lane % HEAD_DIM) < (HEAD_DIM // 2)
    f = _RESIDUE_DILS[0]

    @pl.when(i == 0)
    def _():
        pb_ref[...] = jnp.zeros_like(pb_ref)

    def step(p_new, p):
        u = _rms(h_ref[...], ga_ref[...]).astype(BF16)

        def project(k):
            cols = slice(k * 2 * LANES, (k + 1) * 2 * LANES)
            p_new[:, cols] = jnp.dot(u, w_ref[:, cols], preferred_element_type=F32)

        def finish(c, x):
            oi, oc, _, rope = _CHUNKS[c]
            cols = slice(oc * LANES, (oc + 1) * LANES)
            if rope:
                swapped = jnp.where(first_half, pltpu.roll(x, LANES - HEAD_DIM // 2, 1),
                                    pltpu.roll(x, HEAD_DIM // 2, 1))
                x = x * cos_ref[...] + swapped * sin_ref[...]
            out_refs[oi][:, cols] = x.astype(BF16)
            if 3 <= oi <= 5:
                st = (oi - 3) * (W_DIL // LANES) + oc
                dst1 = out_refs[_N_FLAT_OUT + (oi - 3)]
                dst2 = out_refs[_N_FLAT_OUT + 3 + (oi - 3)]
                stage_ref[st] = x
                for r in range(f):
                    xr = stage_ref[st, pl.ds(r, tm // f, stride=f), :]
                    dst1[r, :, cols] = xr.astype(BF16)
                    stage2_ref[st * f + r] = xr
                    for r2 in range(f):
                        dst2[r + f * r2, :, cols] = stage2_ref[
                            st * f + r, pl.ds(r2, tm // (f * f), stride=f), :].astype(BF16)

        def normed_pair(c):
            cols = slice(c * LANES, (c + 2) * LANES)
            x2 = p[:, cols]
            ss = jnp.dot((x2 * x2).astype(BF16), seg, preferred_element_type=F32)
            x2 = x2 * lax.rsqrt(ss * (1.0 / HEAD_DIM) + RMS_EPS) * gq_ref[:, cols]
            finish(c, x2[:, :LANES])
            finish(c + 1, x2[:, LANES:])

        b_units = []
        c = 0
        while c < len(_CHUNKS):
            if _CHUNKS[c][2]:
                assert _CHUNKS[c + 1][2]
                b_units.append(functools.partial(normed_pair, c))
                c += 2
            else:
                b_units.append(functools.partial(lambda c: finish(c, p[:, c * LANES:(c + 1) * LANES]), c))
                c += 1
        a_units = [functools.partial(project, k) for k in range(IN_WIDTH // (2 * LANES))]
        for k in range(max(len(a_units), len(b_units))):
            if k < len(a_units):
                a_units[k]()
            if k < len(b_units):
                b_units[k]()

    @pl.when(i % 2 == 0)
    def _():
        step(pa_ref, pb_ref)

    @pl.when(i % 2 == 1)
    def _():
        step(pb_ref, pa_ref)


def _inproj(h, g_attn, w_in_bf16, gq_row, cos_t, sin_t, seg, batch, seq_len):
    n = h.shape[0]
    tm = TM_PROJ
    nsteps = n // tm
    tblocks = seq_len // tm
    widths = (W_NA, W_NA, W_NA, W_DIL, W_DIL, W_DIL, W_MEM)
    nst, f = 3 * W_DIL // LANES, _RESIDUE_DILS[0]
    prev = lambda i: jnp.maximum(i - 1, 0)
    out_specs = [pl.BlockSpec((tm, w), lambda i: (prev(i), 0)) for w in widths]
    out_shape = [jax.ShapeDtypeStruct((n, w), BF16) for w in widths]
    for d in _RESIDUE_DILS:
        out_specs += [pl.BlockSpec((None, d, tm // d, W_DIL),
                                   lambda i: (prev(i) // tblocks, 0, prev(i) % tblocks, 0))] * 3
        out_shape += [jax.ShapeDtypeStruct((batch, d, seq_len // d, W_DIL), BF16)] * 3
    return pl.pallas_call(
        _inproj_kernel,
        grid=(nsteps + 1,),
        in_specs=[
            pl.BlockSpec((tm, D_MODEL), lambda i: (jnp.minimum(i, nsteps - 1), 0)),
            pl.BlockSpec((1, D_MODEL), lambda i: (0, 0)),
            pl.BlockSpec((D_MODEL, IN_WIDTH), lambda i: (0, 0)),
            pl.BlockSpec((1, IN_WIDTH), lambda i: (0, 0)),
            pl.BlockSpec((tm, LANES), lambda i: (prev(i) % tblocks, 0)),
            pl.BlockSpec((tm, LANES), lambda i: (prev(i) % tblocks, 0)),
            pl.BlockSpec((2 * LANES, 2 * LANES), lambda i: (0, 0)),
        ],
        out_specs=out_specs,
        out_shape=out_shape,
        scratch_shapes=[pltpu.VMEM((nst, tm, LANES), F32), pltpu.VMEM((nst * f, tm // f, LANES), F32),
                        pltpu.VMEM((tm, IN_WIDTH), F32), pltpu.VMEM((tm, IN_WIDTH), F32)],
        compiler_params=_cparams(("arbitrary",)),
        name="inproj",
    )(h, g_attn, w_in_bf16, gq_row, cos_t, sin_t, seg)


def _stack_heads(q):
    first = lax.broadcasted_iota(jnp.int32, (1, LANES), 1) < HEAD_DIM
    zero = jnp.zeros_like(q)
    return jnp.concatenate([jnp.where(first, q, zero), jnp.where(first, zero, q)], axis=0)


def _with_ones(v):
    return jnp.concatenate([v, jnp.ones_like(v)], axis=1)


def _attend_all(problems, consume, want_lse, skew):
    def logits(n):
        q, k, bias, v_ones = problems[n]()
        s = lax.dot_general(q, k, (((1,), (1,)), ((), ())), preferred_element_type=F32)
        return (s if bias is None else s + bias), v_ones

    ahead = logits(0) if skew else None
    for n in range(len(problems)):
        s, v_ones = ahead if skew else logits(n)
        if skew and n + 1 < len(problems):
            ahead = logits(n + 1)
        m = jnp.max(s, axis=-1, keepdims=True)
        p = jnp.exp(s - m).astype(BF16)
        ol = jnp.dot(p, v_ones, preferred_element_type=F32)
        l = ol[:, LANES:]
        consume(n, ol[:, :LANES] / l, (m + jnp.log(l)) if want_lse else None)


def _na_mem_kernel(q_ref, k_ref, v_ref, *rest, rows):
    bias_refs = rest[:NA_STEP_BLOCKS]
    qm_ref, km_ref, vm_ref, o_ref, om_ref = rest[NA_STEP_BLOCKS:]
    i = pl.program_id(1)
    first = lax.broadcasted_iota(jnp.int32, (1, LANES), 1) < HEAD_DIM
    n_tiles = W_NA // LANES
    n_mem = W_MEM // LANES

    def na_problem(sb, p, hh):
        srow = jnp.clip((i * NA_STEP_BLOCKS + sb) * NA_QROWS - NA_KH // 2, 0, rows - NA_KROWS)
        start = pl.multiple_of(srow * GRID_W, GRID_W)
        sl = slice(p * LANES, (p + 1) * LANES)
        q = q_ref[sb * NA_QB:(sb + 1) * NA_QB, sl]
        zero = jnp.zeros_like(q)
        q = jnp.where(first, q, zero) if hh == 0 else jnp.where(first, zero, q)
        return (q, k_ref[pl.ds(start, NA_KB), sl], bias_refs[sb][0, 2 * p + hh],
                _with_ones(v_ref[pl.ds(start, NA_KB), sl]))

    def mem_problem(sb, c):
        sl = slice(c * LANES, (c + 1) * LANES)
        return (_stack_heads(qm_ref[sb * NA_QB:(sb + 1) * NA_QB, sl]), km_ref[:, sl], None,
                _with_ones(vm_ref[:, sl]))

    problems, targets = [], []
    for sb in range(NA_STEP_BLOCKS):
        for p in range(n_tiles):
            for hh in range(2):
                problems.append(functools.partial(na_problem, sb, p, hh))
                targets.append(("na", sb, p, hh))
        for c in range(n_mem):
            problems.append(functools.partial(mem_problem, sb, c))
            targets.append(("mem", sb, c, None))
    held = {}

    def consume(n, o, _):
        kind, sb, t, hh = targets[n]
        qrows = slice(sb * NA_QB, (sb + 1) * NA_QB)
        sl = slice(t * LANES, (t + 1) * LANES)
        if kind == "mem":
            om_ref[qrows, sl] = jnp.where(first, o[:NA_QB], o[NA_QB:]).astype(BF16)
        elif hh == 0:
            held[(sb, t)] = o
        else:
            o_ref[qrows, sl] = jnp.where(first, held.pop((sb, t)), o).astype(BF16)

    _attend_all(problems, consume, False, skew=True)


def _na_mem_attention(q, k, v, bias, layer, qm, km, vm, batch, seq_len, mem_len):
    rows = seq_len // GRID_W
    nblk = seq_len // NA_QB
    step = NA_STEP_BLOCKS * NA_QB
    q3, k3, v3 = (a.reshape(batch, seq_len, W_NA) for a in (q, k, v))

    def bias_spec(sb):
        def index(b, i):
            blk = i * NA_STEP_BLOCKS + sb
            return (layer, (blk > 0).astype(jnp.int32) + (blk == nblk - 1).astype(jnp.int32), 0, 0, 0)
        return pl.BlockSpec((None, 1, H_NA, NA_QB, NA_KB), index)

    o, om = pl.pallas_call(
        functools.partial(_na_mem_kernel, rows=rows),
        grid=(batch, nblk // NA_STEP_BLOCKS),
        in_specs=[
            pl.BlockSpec((None, step, W_NA), lambda b, i: (b, i, 0)),
            pl.BlockSpec((None, seq_len, W_NA), lambda b, i: (b, 0, 0)),
            pl.BlockSpec((None, seq_len, W_NA), lambda b, i: (b, 0, 0)),
            *[bias_spec(sb) for sb in range(NA_STEP_BLOCKS)],
            pl.BlockSpec((None, step, W_MEM), lambda b, i: (b, i, 0)),
            pl.BlockSpec((None, mem_len, W_MEM), lambda b, i: (b, 0, 0)),
            pl.BlockSpec((None, mem_len, W_MEM), lambda b, i: (b, 0, 0)),
        ],
        out_specs=[pl.BlockSpec((None, step, W_NA), lambda b, i: (b, i, 0)),
                   pl.BlockSpec((None, step, W_MEM), lambda b, i: (b, i, 0))],
        out_shape=[jax.ShapeDtypeStruct((batch, seq_len, W_NA), BF16),
                   jax.ShapeDtypeStruct((batch, seq_len, W_MEM), BF16)],
        compiler_params=_cparams(("parallel", "arbitrary")),
        name="na_mem_attention",
    )(q3, k3, v3, *([bias] * NA_STEP_BLOCKS), qm.reshape(batch, seq_len, W_MEM),
      km.reshape(batch, mem_len, W_MEM), vm.reshape(batch, mem_len, W_MEM))
    return o.reshape(batch * seq_len, W_NA), om.reshape(batch * seq_len, W_MEM)


def _na_bias_tables(rpb, rows):
    depth = rpb.shape[0]
    nblk = rows // NA_QROWS
    qc = np.arange(GRID_W)[:, None]
    kc = np.arange(GRID_W)[None, :]
    cs = np.clip(qc - NA_KW // 2, 0, GRID_W - NA_KW)
    col_valid = (kc >= cs) & (kc < cs + NA_KW)
    col_rel = kc - qc + (NA_KW - 1)
    col_sel = (col_rel[None] == np.arange(2 * NA_KW - 1)[:, None, None]).astype(np.float32)
    t1 = jnp.einsum('lhab,bqk->lhaqk', rpb, jnp.asarray(col_sel), precision=lax.Precision.HIGHEST)
    t1 = jnp.where(jnp.asarray(col_valid), t1, NEG_INF)
    masked = jnp.full((depth, H_NA, GRID_W, GRID_W), NEG_INF, F32)
    tables = []
    for blk in (0, 1, nblk - 1):
        r0 = blk * NA_QROWS
        srow = int(np.clip(r0 - NA_KH // 2, 0, rows - NA_KROWS))
        slabs = []
        for qr in range(r0, r0 + NA_QROWS):
            rs = int(np.clip(qr - NA_KH // 2, 0, rows - NA_KH))
            slabs.append(jnp.concatenate(
                [t1[:, :, kr - qr + NA_KH - 1] if rs <= kr < rs + NA_KH else masked
                 for kr in range(srow, srow + NA_KROWS)], axis=-1))
        tables.append(jnp.stack(slabs, axis=2))
    return jnp.stack(tables, axis=1).reshape(depth, 3, H_NA, NA_QB, NA_KB)


def _band_kernel(q_ref, k_ref, v_ref, mask_ref, o_ref, lse_ref, *, length):
    j = pl.program_id(1)
    step = q_ref.shape[0]
    n_tiles = W_DIL // LANES
    first = lax.broadcasted_iota(jnp.int32, (1, LANES), 1) < HEAD_DIM
    lane = lax.broadcasted_iota(jnp.int32, (1, LANES), 1)

    def problem(sb, p):
        q0 = j * step + sb * DIL_QB
        start = pl.multiple_of(jnp.clip(q0 - DIL_HALF, 0, length - DIL_KB), DIL_HALF)
        variant = (q0 > 0).astype(jnp.int32) + (q0 == length - DIL_QB).astype(jnp.int32)
        mask = mask_ref[variant]
        sl = slice(p * LANES, (p + 1) * LANES)
        return (_stack_heads(q_ref[sb * DIL_QB:(sb + 1) * DIL_QB, sl]), k_ref[pl.ds(start, DIL_KB), sl],
                jnp.concatenate([mask, mask], axis=0), _with_ones(v_ref[pl.ds(start, DIL_KB), sl]))

    problems = [functools.partial(problem, sb, p) for sb in range(step // DIL_QB) for p in range(n_tiles)]
    lse_heads = {}

    def consume(n, o, lse):
        sb, p = divmod(n, n_tiles)
        rows = slice(sb * DIL_QB, (sb + 1) * DIL_QB)
        o_ref[rows, p * LANES:(p + 1) * LANES] = jnp.where(first, o[:DIL_QB], o[DIL_QB:]).astype(BF16)
        acc = lse_heads.pop(sb, jnp.zeros((DIL_QB, LANES), F32))
        acc = jnp.where(lane == 2 * p, lse[:DIL_QB], jnp.where(lane == 2 * p + 1, lse[DIL_QB:], acc))
        if p == n_tiles - 1:
            lse_ref[rows, :] = acc
        else:
            lse_heads[sb] = acc

    _attend_all(problems, consume, True, skew=False)


def _band_attention(q, k, v, masks):
    nseq, length, _ = q.shape
    step = min(DIL_STEP, length)
    nblk = length // step
    blk = pl.BlockSpec((None, step, W_DIL), lambda s, j: (s, j, 0))
    seq = pl.BlockSpec((None, length, W_DIL), lambda s, j: (s, 0, 0))
    return pl.pallas_call(
        functools.partial(_band_kernel, length=length),
        grid=(nseq, nblk),
        in_specs=[blk, seq, seq, pl.BlockSpec((3, DIL_QB, DIL_KB), lambda s, j: (0, 0, 0))],
        out_specs=[blk, pl.BlockSpec((None, step, LANES), lambda s, j: (s, j, 0))],
        out_shape=[jax.ShapeDtypeStruct((nseq, length, W_DIL), BF16),
                   jax.ShapeDtypeStruct((nseq, length, LANES), F32)],
        compiler_params=_cparams(("parallel", "arbitrary")),
        name="band_attention",
    )(q, k, v, masks)


def _band_masks():
    qq = np.arange(DIL_QB)[:, None]
    kk = np.arange(DIL_KB)[None, :]
    out = []
    for off in (0, -DIL_HALF, DIL_QB - DIL_KB):
        rel = kk + off - qq
        out.append(np.where(np.abs(rel) <= DIL_HALF, 0.0, NEG_INF))
    return jnp.asarray(np.stack(out), F32)


def _memkv_kernel(mem_ref, g_ref, w_ref, gk_ref, seg_ref, km_ref, vm_ref):
    mn = _rms(mem_ref[...], g_ref[...]).astype(BF16)
    kv = jnp.dot(mn, w_ref[...], preferred_element_type=F32)
    for c in range(W_MEM // LANES):
        x = kv[:, c * LANES:(c + 1) * LANES]
        ss = jnp.dot((x * x).astype(BF16), seg_ref[:LANES, :LANES], preferred_element_type=F32)
        x = x * lax.rsqrt(ss * (1.0 / HEAD_DIM) + RMS_EPS) * gk_ref[...]
        km_ref[:, c * LANES:(c + 1) * LANES] = x.astype(BF16)
    vm_ref[...] = kv[:, W_MEM:].astype(BF16)


def _memkv(mem2, g_mem, w_kv_bf16, gk_row, seg):
    rows = mem2.shape[0]
    full = lambda shape: pl.BlockSpec(shape, lambda i: (0,) * len(shape))
    return pl.pallas_call(
        _memkv_kernel,
        grid=(1,),
        in_specs=[full((rows, D_MODEL)), full((1, D_MODEL)), full((D_MODEL, 2 * W_MEM)),
                  full((1, LANES)), full((2 * LANES, 2 * LANES))],
        out_specs=[full((rows, W_MEM)), full((rows, W_MEM))],
        out_shape=[jax.ShapeDtypeStruct((rows, W_MEM), BF16)] * 2,
        compiler_params=_cparams(("arbitrary",)),
        name="mem_kv",
    )(mem2, g_mem, w_kv_bf16, gk_row, seg)


def _outproj_kernel(ona_ref, *rest, with_router):
    nd = 2 * len(DILATIONS)
    dil_refs, rest = rest[:nd], rest[nd:]
    om_ref, h_ref, go_ref, w_ref, gf_ref = rest[:5]
    rest = rest[5:]
    if with_router:
        wr_ref, hn_ref, u_ref, lg_ref = rest[:4]
        stage_refs = rest[4:]
    else:
        hn_ref, u_ref = rest[:2]
        stage_refs = rest[2:]
    stage_refs, tmp_ref = stage_refs[:-1], stage_refs[-1]
    tm = h_ref.shape[0]
    f = _RESIDUE_DILS[0]
    nc = W_DIL // LANES
    go = go_ref[...]
    spread = (lax.broadcasted_iota(jnp.int32, (LANES, W_DIL), 0)
              == lax.broadcasted_iota(jnp.int32, (LANES, W_DIL), 1) // HEAD_DIM).astype(BF16)

    def chain(r0, th):
        rows = slice(r0, r0 + th)

        def token_order(ref, d, st, ncols):
            if d == 1:
                return ref[rows, :].astype(F32)
            for c in range(ncols):
                cols = slice(c * LANES, (c + 1) * LANES)
                for r in range(f):
                    if d == f:
                        st[c, pl.ds(r0 + r, th // f, stride=f), :] = ref[
                            r, r0 // f:(r0 + th) // f, cols].astype(F32)
                    else:
                        for r2 in range(f):
                            tmp_ref[c * f + r, pl.ds(r0 // f + r2, th // d, stride=f), :] = ref[
                                r + f * r2, r0 // d:(r0 + th) // d, cols].astype(F32)
                        st[c, pl.ds(r0 + r, th // f, stride=f), :] = tmp_ref[
                            c * f + r, r0 // f:(r0 + th) // f, :]
            return jnp.concatenate([st[c, rows, :] for c in range(ncols)], axis=-1)

        def per_head_to_lanes(w):
            w_hi = w.astype(BF16)
            w_lo = (w - w_hi.astype(F32)).astype(BF16)
            return (jnp.dot(w_hi, spread, preferred_element_type=F32)
                    + jnp.dot(w_lo, spread, preferred_element_type=F32))

        os_, ls_ = [], []
        for di, d in enumerate(DILATIONS):
            o_ref_d, l_ref_d = dil_refs[2 * di:2 * di + 2]
            o_st, l_st = (stage_refs[2 * (di - 1)], stage_refs[2 * (di - 1) + 1]) if d > 1 else (None, None)
            os_.append(token_order(o_ref_d, d, o_st, nc))
            ls_.append(token_order(l_ref_d, d, l_st, 1))
        mx = functools.reduce(jnp.maximum, ls_)
        es = [jnp.exp(l - mx) for l in ls_]
        inv = 1.0 / sum(es)
        odil = sum(per_head_to_lanes(e * inv) * o for e, o in zip(es, os_))
        mixed = jnp.concatenate([
            _rms(ona_ref[rows, :].astype(F32), go[:, :W_NA]),
            _rms(odil, go[:, W_NA:W_NA + W_DIL]),
            _rms(om_ref[rows, :].astype(F32), go[:, W_NA + W_DIL:]),
        ], axis=-1).astype(BF16)
        yield
        hn = h_ref[rows, :] + jnp.dot(mixed, w_ref[...], preferred_element_type=F32)
        hn_ref[rows, :] = hn
        yield
        u = _rms(hn, gf_ref[...])
        if with_router:
            u_ref[rows, :] = u
            u_hi = u.astype(BF16)
            u_lo = (u - u_hi.astype(F32)).astype(BF16)
            lg_ref[rows, :] = (jnp.dot(u_hi, wr_ref[...], preferred_element_type=F32)
                               + jnp.dot(u_lo, wr_ref[...], preferred_element_type=F32))
        else:
            u_ref[rows, :] = u.astype(BF16)

    th = tm // PROJ_CHAINS
    chains = [chain(ci * th, th) for ci in range(PROJ_CHAINS)]
    while chains:
        for ch in list(chains):
            if next(ch, "done") == "done":
                chains.remove(ch)


def _outproj(ona, dil, om, h, g_out, w_out_bf16, g_ffn, w_router_pad, batch, seq_len):
    n = h.shape[0]
    tm = TM_PROJ
    tblocks = seq_len // tm
    with_router = w_router_pad is not None
    row = lambda w: pl.BlockSpec((tm, w), lambda i: (i, 0))
    full = lambda shape: pl.BlockSpec(shape, lambda i: (0,) * len(shape))
    in_specs = [row(W_NA)]
    args = [ona]
    scratch = []
    for di, d in enumerate(DILATIONS):
        for a, w in zip(dil[2 * di:2 * di + 2], (W_DIL, LANES)):
            if d == 1:
                in_specs.append(row(w))
                args.append(a.reshape(n, w))
            else:
                in_specs.append(pl.BlockSpec((None, d, tm // d, w),
                                             lambda i: (i // tblocks, 0, i % tblocks, 0)))
                args.append(a.reshape(batch, d, seq_len // d, w))
                scratch.append(pltpu.VMEM((w // LANES, tm, LANES), F32))
    f = _RESIDUE_DILS[0]
    scratch.append(pltpu.VMEM((W_DIL // LANES * f, tm // f, LANES), F32))
    in_specs += [row(W_MEM), row(D_MODEL), full((1, D_MODEL)), full((D_MODEL, D_MODEL)),
                 full((1, D_MODEL))]
    args += [om, h, g_out, w_out_bf16, g_ffn]
    out_specs = [row(D_MODEL), row(D_MODEL)]
    out_shape = [jax.ShapeDtypeStruct((n, D_MODEL), F32),
                 jax.ShapeDtypeStruct((n, D_MODEL), F32 if with_router else BF16)]
    if with_router:
        in_specs.append(full((D_MODEL, LANES)))
        args.append(w_router_pad)
        out_specs.append(row(LANES))
        out_shape.append(jax.ShapeDtypeStruct((n, LANES), F32))
    return pl.pallas_call(
        functools.partial(_outproj_kernel, with_router=with_router),
        grid=(n // tm,),
        in_specs=in_specs,
        out_specs=out_specs,
        out_shape=out_shape,
        scratch_shapes=scratch,
        compiler_params=_cparams(("parallel",)),
        name="outproj",
    )(*args)


def _swiglu(x, wg, wu, wd):
    g = jnp.dot(x, wg, preferred_element_type=F32)
    u = jnp.dot(x, wu, preferred_element_type=F32)
    hm = (g * jax.nn.sigmoid(g) * u).astype(BF16)
    return jnp.dot(hm, wd, preferred_element_type=F32)


def _dense_ffn_kernel(u_ref, h_ref, wg_ref, wu_ref, wd_ref, o_ref, hm_ref):
    x = u_ref[...]
    for c in range(hm_ref.shape[1] // TF_DENSE):
        cols = slice(c * TF_DENSE, (c + 1) * TF_DENSE)
        g = jnp.dot(x, wg_ref[:, cols], preferred_element_type=F32)
        u = jnp.dot(x, wu_ref[:, cols], preferred_element_type=F32)
        hm_ref[:, cols] = (g * jax.nn.sigmoid(g) * u).astype(BF16)
    o_ref[...] = h_ref[...] + jnp.dot(hm_ref[...], wd_ref[...], preferred_element_type=F32)


def _dense_ffn(u, h, wg_bf16, wu_bf16, wd_bf16):
    n = h.shape[0]
    d_ff = wg_bf16.shape[1]
    tm = TM_FFN
    full = lambda shape: pl.BlockSpec(shape, lambda i: (0,) * len(shape))
    return pl.pallas_call(
        _dense_ffn_kernel,
        grid=(n // tm,),
        in_specs=[
            pl.BlockSpec((tm, D_MODEL), lambda i: (i, 0)),
            pl.BlockSpec((tm, D_MODEL), lambda i: (i, 0)),
            full((D_MODEL, d_ff)), full((D_MODEL, d_ff)), full((d_ff, D_MODEL)),
        ],
        out_specs=pl.BlockSpec((tm, D_MODEL), lambda i: (i, 0)),
        out_shape=jax.ShapeDtypeStruct((n, D_MODEL), F32),
        scratch_shapes=[pltpu.VMEM((tm, d_ff), BF16)],
        compiler_params=_cparams(("parallel",)),
        name="dense_ffn",
    )(u, h, wg_bf16, wu_bf16, wd_bf16)


def _dispatch_kernel(vend_ref, pend_ref, d0_ref, d1_ref, u_ref, x_hbm, zbuf, sem, zsem):
    i = pl.program_id(0)
    tc = u_ref.shape[0]
    zr = zbuf.shape[0]
    n_rows = x_hbm.shape[0]

    def zero_pieces(lo, hi):
        def start(p, carry):
            pltpu.make_async_copy(zbuf, x_hbm.at[pl.ds(pl.multiple_of(p * zr, zr), zr), :], zsem).start()
            return carry

        def wait(p, carry):
            pltpu.make_async_copy(zbuf, x_hbm.at[pl.ds(0, zr), :], zsem).wait()
            return carry

        lax.fori_loop(lo, hi, start, 0)
        lax.fori_loop(lo, hi, wait, 0)

    @pl.when(i == 0)
    def _():
        zbuf[...] = jnp.zeros_like(zbuf)
        for e in range(N_EXPERTS):
            zero_pieces(vend_ref[e] // zr, pend_ref[e] // zr)
        zero_pieces(pend_ref[N_EXPERTS - 1] // zr, n_rows // zr)

    def issue(r, carry):
        src = u_ref.at[pl.ds(r, 1), :]
        pltpu.make_async_copy(src, x_hbm.at[pl.ds(d0_ref[0, 0, r], 1), :], sem).start(priority=0)
        pltpu.make_async_copy(src, x_hbm.at[pl.ds(d1_ref[0, 0, r], 1), :], sem).start(priority=1)
        return carry

    lax.fori_loop(0, tc, issue, 0, unroll=ISSUE_UNROLL)
    for _ in range(TOP_K):
        pltpu.make_async_copy(u_ref, x_hbm.at[pl.ds(0, tc), :], sem).wait()


def _dispatch(u, dest, vend, pend, n_rows):
    n = u.shape[0]
    tc = TC_COMBINE
    nblk = n // tc
    d0 = dest[:, 0].reshape(nblk, 1, tc)
    d1 = dest[:, 1].reshape(nblk, 1, tc)
    smem = pl.BlockSpec((1, 1, tc), lambda i, ve, pe: (i, 0, 0), memory_space=pltpu.SMEM)
    grid_spec = pltpu.PrefetchScalarGridSpec(
        num_scalar_prefetch=2,
        grid=(nblk,),
        in_specs=[smem, smem, pl.BlockSpec((tc, D_MODEL), lambda i, ve, pe: (i, 0))],
        out_specs=pl.BlockSpec(memory_space=pl.ANY),
        scratch_shapes=[pltpu.VMEM((ZERO_ROWS, D_MODEL), F32), pltpu.SemaphoreType.DMA(()),
                        pltpu.SemaphoreType.DMA(())],
    )
    return pl.pallas_call(
        _dispatch_kernel,
        grid_spec=grid_spec,
        out_shape=jax.ShapeDtypeStruct((n_rows, D_MODEL), F32),
        compiler_params=_cparams(("arbitrary",)),
        name="moe_dispatch",
    )(vend, pend, d0, d1, u)


def _moe_ffn_kernel(be_ref, nv_ref, nu_ref, x_ref, wg_ref, wu_ref, wd_ref, y_ref, xb):
    i = pl.program_id(0)
    j = pl.program_id(1)
    active = i < nu_ref[0]
    tm = x_ref.shape[0]
    nchunk = tm // MOE_CHUNK

    def chunk(c, wg, wu, wd):
        rows = slice(c * MOE_CHUNK, (c + 1) * MOE_CHUNK)
        y_ref[rows, :] += _swiglu(xb[rows, :], wg, wu, wd)

    @pl.when(j == 0)
    def _():
        y_ref[...] = jnp.zeros_like(y_ref)

    @pl.when(active)
    def _():
        @pl.when(j == 0)
        def _():
            xb[...] = x_ref[...].astype(BF16)

        nvalid = nv_ref[i]

        @pl.when(nvalid > tm - MOE_CHUNK)
        def _():
            wg, wu, wd = (r[...].astype(BF16) for r in (wg_ref, wu_ref, wd_ref))

            def gate_up(c):
                x = xb[c * MOE_CHUNK:(c + 1) * MOE_CHUNK, :]
                return (jnp.dot(x, wg, preferred_element_type=F32),
                        jnp.dot(x, wu, preferred_element_type=F32))

            ahead = gate_up(0)
            for c in range(nchunk):
                g, u = ahead
                if c + 1 < nchunk:
                    ahead = gate_up(c + 1)
                hm = (g * jax.nn.sigmoid(g) * u).astype(BF16)
                y_ref[c * MOE_CHUNK:(c + 1) * MOE_CHUNK, :] += jnp.dot(hm, wd, preferred_element_type=F32)

        @pl.when(nvalid <= tm - MOE_CHUNK)
        def _():
            wg, wu, wd = (r[...].astype(BF16) for r in (wg_ref, wu_ref, wd_ref))
            for c in range(nchunk):
                @pl.when(c * MOE_CHUNK < nvalid)
                def _():
                    chunk(c, wg, wu, wd)


def _moe_ffn(x_sorted, blk_expert, blk_valid, n_used, wg, wu, wd):
    n_rows = x_sorted.shape[0]
    tm, tf = TM_MOE, TF_MOE
    nblk = n_rows // tm
    d_ff = wg.shape[2]
    nj = d_ff // tf

    def jeff(i, j, nu):
        return jnp.where(i < nu[0], j, nj - 1)

    grid_spec = pltpu.PrefetchScalarGridSpec(
        num_scalar_prefetch=3,
        grid=(nblk, nj),
        in_specs=[
            pl.BlockSpec((tm, D_MODEL), lambda i, j, be, nv, nu: (jnp.minimum(i, nu[0] - 1), 0)),
            pl.BlockSpec((None, D_MODEL, tf), lambda i, j, be, nv, nu: (be[i], 0, jeff(i, j, nu))),
            pl.BlockSpec((None, D_MODEL, tf), lambda i, j, be, nv, nu: (be[i], 0, jeff(i, j, nu))),
            pl.BlockSpec((None, tf, D_MODEL), lambda i, j, be, nv, nu: (be[i], jeff(i, j, nu), 0)),
        ],
        out_specs=pl.BlockSpec((tm, D_MODEL), lambda i, j, be, nv, nu: (i, 0)),
        scratch_shapes=[pltpu.VMEM((tm, D_MODEL), BF16)],
    )
    return pl.pallas_call(
        _moe_ffn_kernel,
        grid_spec=grid_spec,
        out_shape=jax.ShapeDtypeStruct((n_rows, D_MODEL), F32),
        compiler_params=_cparams(("arbitrary", "arbitrary")),
        name="moe_ffn",
    )(blk_expert, blk_valid, n_used, x_sorted, wg, wu, wd)


def _combine_kernel(p0_ref, p1_ref, h_ref, g_ref, y_hbm, o_ref, buf0, buf1, sem):
    tc = buf0.shape[0]

    def issue(r, carry):
        pltpu.make_async_copy(y_hbm.at[pl.ds(p0_ref[0, 0, r], 1), :], buf0.at[pl.ds(r, 1), :],
                              sem.at[0]).start(priority=0)
        pltpu.make_async_copy(y_hbm.at[pl.ds(p1_ref[0, 0, r], 1), :], buf1.at[pl.ds(r, 1), :],
                              sem.at[1]).start(priority=1)
        return carry

    lax.fori_loop(0, tc, issue, 0, unroll=ISSUE_UNROLL)
    pltpu.make_async_copy(y_hbm.at[pl.ds(0, tc), :], buf0, sem.at[0]).wait()
    pltpu.make_async_copy(y_hbm.at[pl.ds(0, tc), :], buf1, sem.at[1]).wait()
    g = g_ref[...]
    o_ref[...] = h_ref[...] + g[:, 0:1] * buf0[...] + g[:, 1:2] * buf1[...]


def _combine(h, y, dest, gates):
    n = h.shape[0]
    tc = TC_COMBINE
    nblk = n // tc
    p0 = dest[:, 0].reshape(nblk, 1, tc)
    p1 = dest[:, 1].reshape(nblk, 1, tc)
    smem = pl.BlockSpec((1, 1, tc), lambda i: (i, 0, 0), memory_space=pltpu.SMEM)
    return pl.pallas_call(
        _combine_kernel,
        grid=(nblk,),
        in_specs=[smem, smem, pl.BlockSpec((tc, D_MODEL), lambda i: (i, 0)),
                  pl.BlockSpec((tc, TOP_K), lambda i: (i, 0)), pl.BlockSpec(memory_space=pl.ANY)],
        out_specs=pl.BlockSpec((tc, D_MODEL), lambda i: (i, 0)),
        out_shape=jax.ShapeDtypeStruct((n, D_MODEL), F32),
        scratch_shapes=[pltpu.VMEM((tc, D_MODEL), F32), pltpu.VMEM((tc, D_MODEL), F32),
                        pltpu.SemaphoreType.DMA((2,))],
        compiler_params=_cparams(("arbitrary",)),
        name="moe_combine",
    )(p0, p1, h, gates, y)


def _route(logits):
    n = logits.shape[0]
    top_logit, top_idx = lax.top_k(logits, TOP_K)
    gates = jax.nn.softmax(top_logit, axis=-1)
    e_flat = top_idx.reshape(n * TOP_K).astype(jnp.int32)
    onehot = (e_flat[:, None] == jnp.arange(N_EXPERTS, dtype=jnp.int32)[None, :]).astype(jnp.int32)
    csum = jnp.cumsum(onehot, axis=0)
    counts = csum[-1]
    padded = (counts + TM_MOE - 1) // TM_MOE * TM_MOE
    pend = jnp.cumsum(padded)
    pstart = pend - padded
    dest = jnp.sum(onehot * (csum - 1 + pstart[None, :]), axis=1).reshape(n, TOP_K)
    n_rows = n * TOP_K + N_EXPERTS * TM_MOE
    nblk = n_rows // TM_MOE
    n_used = pend[-1] // TM_MOE
    blk_start = jnp.minimum(jnp.arange(nblk, dtype=jnp.int32) * TM_MOE, (n_used - 1) * TM_MOE)
    blk_expert = jnp.minimum(
        jnp.sum((blk_start[:, None] >= pend[None, :]).astype(jnp.int32), axis=1), N_EXPERTS - 1)
    blk_valid = jnp.clip(counts[blk_expert] - (blk_start - pstart[blk_expert]), 0, TM_MOE)
    i32 = lambda a: a.astype(jnp.int32)
    return (i32(dest), gates, i32(pstart + counts), i32(pend), i32(blk_expert), i32(blk_valid),
            i32(n_used).reshape(1), n_rows)


def _rope_tables(seq_len):
    half = HEAD_DIM // 2
    inv_freq = jnp.power(ROPE_THETA, -(2.0 / HEAD_DIM) * jnp.arange(half, dtype=F32))
    ang = jnp.arange(seq_len, dtype=F32)[:, None] * inv_freq[None, :]
    cos, sin = jnp.cos(ang), jnp.sin(ang)
    cos_t = jnp.tile(cos, (1, LANES // half))
    sin_t = jnp.tile(jnp.concatenate([-sin, sin], axis=1), (1, LANES // HEAD_DIM))
    return cos_t, sin_t


def kernel(x, mem, g_attn, w_in, g_qk_na, rpb_na, g_qk_dil, g_mem, w_mem_kv, g_qk_mem, g_out, w_out,
           g_ffn, w_gate_dense, w_up_dense, w_down_dense, w_router, w_gate_moe, w_up_moe, w_down_moe):
    batch, seq_len, _ = x.shape
    mem_len = mem.shape[1]
    depth = g_attn.shape[0]
    n = batch * seq_len
    scale = HEAD_DIM ** -0.5

    cos_t, sin_t = _rope_tables(seq_len)
    seg = jnp.asarray(np.kron(np.eye(2 * LANES // HEAD_DIM), np.ones((HEAD_DIM, HEAD_DIM))), BF16)
    band_masks = _band_masks()
    na_bias = _na_bias_tables(rpb_na, seq_len // GRID_W)
    ones = lambda w: jnp.ones((w,), F32)

    h = x.reshape(n, D_MODEL)
    mem2 = mem.reshape(batch * mem_len, D_MODEL)
    for layer in range(depth):
        gq_row = jnp.concatenate([
            jnp.tile(g_qk_na[layer, 0] * scale, H_NA), jnp.tile(g_qk_na[layer, 1], H_NA), ones(W_NA),
            jnp.tile(g_qk_dil[layer, 0] * scale, H_DIL), jnp.tile(g_qk_dil[layer, 1], H_DIL), ones(W_DIL),
            jnp.tile(g_qk_mem[layer, 0] * scale, H_MEM)]).reshape(1, IN_WIDTH)
        proj = _inproj(h, g_attn[layer].reshape(1, D_MODEL), w_in[layer].astype(BF16), gq_row, cos_t, sin_t,
                       seg, batch, seq_len)
        qa, ka, va, qb, kb, vb, qm = proj[:_N_FLAT_OUT]

        dil = []
        for d in DILATIONS:
            if d == 1:
                qkv = [a.reshape(batch, seq_len, W_DIL) for a in (qb, kb, vb)]
            else:
                base = _N_FLAT_OUT + 3 * _RESIDUE_DILS.index(d)
                qkv = [a.reshape(batch * d, seq_len // d, W_DIL) for a in proj[base:base + 3]]
            dil += _band_attention(*qkv, band_masks)

        km, vm = _memkv(mem2, g_mem[layer].reshape(1, D_MODEL), w_mem_kv[layer].astype(BF16),
                        jnp.tile(g_qk_mem[layer, 1], LANES // HEAD_DIM).reshape(1, LANES), seg)
        o_na, o_mem = _na_mem_attention(qa, ka, va, na_bias, layer, qm, km, vm, batch, seq_len, mem_len)

        i = layer // 2
        moe = layer % 2 == 1
        w_router_pad = None
        if moe:
            wr_hi = w_router[i].astype(BF16)
            wr_lo = (w_router[i] - wr_hi.astype(F32)).astype(BF16)
            w_router_pad = jnp.pad(jnp.concatenate([wr_hi, wr_lo], axis=1),
                                   ((0, 0), (0, LANES - 2 * N_EXPERTS)))
        outs = _outproj(o_na, dil, o_mem, h, g_out[layer].reshape(1, D_MODEL), w_out[layer].astype(BF16),
                        g_ffn[layer].reshape(1, D_MODEL), w_router_pad, batch, seq_len)
        if not moe:
            h, u = outs
            h = _dense_ffn(u, h, w_gate_dense[i].astype(BF16), w_up_dense[i].astype(BF16),
                           w_down_dense[i].astype(BF16))
        else:
            h, u, logits = outs
            dest, gates, vend, pend, blk_expert, blk_valid, n_used, n_rows = _route(
                logits[:, :N_EXPERTS] + logits[:, N_EXPERTS:2 * N_EXPERTS])
            x_sorted = _dispatch(u, dest, vend, pend, n_rows)
            y = _moe_ffn(x_sorted, blk_expert, blk_valid, n_used, w_gate_moe[i], w_up_moe[i], w_down_moe[i])
            h = _combine(h, y, dest, gates)
    return h.reshape(batch, seq_len, D_MODEL)
```

```python
import functools

import numpy as np
import jax
import jax.numpy as jnp
from jax import lax
from jax.experimental import pallas as pl
from jax.experimental.pallas import tpu as pltpu

F32 = jnp.float32
BF16 = jnp.bfloat16

D_MODEL = 1024
HEAD_DIM = 64
H_NA, H_DIL, H_MEM = 6, 6, 4
W_NA, W_DIL, W_MEM = H_NA * HEAD_DIM, H_DIL * HEAD_DIM, H_MEM * HEAD_DIM
IN_WIDTH = 3 * W_NA + 3 * W_DIL + W_MEM
GRID_W = 64
NA_KH, NA_KW = 8, 16
DIL_CFG = ((128, 1), (512, 4), (2048, 16))
ROPE_THETA = 10000.0
N_EXPERTS = 8
TOP_K = 2
RMS_EPS = 1e-6
NEG_INF = -1e30
LOG2E = 1.4426950408889634

LANES = 128
TM_PROJ = 512
PROJ_CHAINS = 2
NA_QROWS = 4
NA_QB = NA_QROWS * GRID_W
NA_KROWS = 12
NA_KB = NA_KROWS * GRID_W
NA_STEP_BLOCKS = 2
DIL_QB = 128
DIL_HALF = 64
DIL_KB = DIL_QB + 2 * DIL_HALF
DIL_STEP = 1024
TM_FFN = 512
TF_DENSE = 256
TM_MOE = 2048
TF_MOE = 512
MOE_CHUNK = 256
TC_COMBINE = 512
ZERO_ROWS = 256
ISSUE_UNROLL = 8
VMEM_LIMIT = 56 * 1024 * 1024


def _cparams(sem):
    return pltpu.CompilerParams(dimension_semantics=sem, vmem_limit_bytes=VMEM_LIMIT)


def _rms(x, g):
    return x * lax.rsqrt(jnp.mean(x * x, axis=-1, keepdims=True) + RMS_EPS) * g


_CHUNKS = (
    [(0, c, True, False) for c in range(3)] + [(1, c, True, False) for c in range(3)]
    + [(2, c, False, False) for c in range(3)]
    + [(3, c, True, True) for c in range(3)] + [(4, c, True, True) for c in range(3)]
    + [(5, c, False, False) for c in range(3)]
    + [(6, c, True, False) for c in range(2)]
)


DILATIONS = tuple(d for _, d in DIL_CFG)
_RESIDUE_DILS = tuple(d for d in DILATIONS if d > 1)
assert len(_RESIDUE_DILS) == 2 and _RESIDUE_DILS[1] == _RESIDUE_DILS[0] ** 2
_N_FLAT_OUT = 7


def _inproj_kernel(h_ref, ga_ref, w_ref, gq_ref, cos_ref, sin_ref, seg_ref, *rest):
    out_refs, stage_ref, stage2_ref, pa_ref, pb_ref = rest[:-4], rest[-4], rest[-3], rest[-2], rest[-1]
    i = pl.program_id(0)
    tm = h_ref.shape[0]
    seg = seg_ref[...]
    lane = lax.broadcasted_iota(jnp.int32, (1, LANES), 1)
    first_half = (lane % HEAD_DIM) < (HEAD_DIM // 2)
    f = _RESIDUE_DILS[0]

    @pl.when(i == 0)
    def _():
        pb_ref[...] = jnp.zeros_like(pb_ref)

    def step(p_new, p):
        u = _rms(h_ref[...], ga_ref[...]).astype(BF16)

        def project(k):
            cols = slice(k * 2 * LANES, (k + 1) * 2 * LANES)
            p_new[:, cols] = jnp.dot(u, w_ref[:, cols], preferred_element_type=F32)

        def finish(c, x):
            oi, oc, _, rope = _CHUNKS[c]
            cols = slice(oc * LANES, (oc + 1) * LANES)
            if rope:
                swapped = jnp.where(first_half, pltpu.roll(x, LANES - HEAD_DIM // 2, 1),
                                    pltpu.roll(x, HEAD_DIM // 2, 1))
                x = x * cos_ref[...] + swapped * sin_ref[...]
            out_refs[oi][:, cols] = x.astype(BF16)
            if 3 <= oi <= 5:
                st = (oi - 3) * (W_DIL // LANES) + oc
                dst1 = out_refs[_N_FLAT_OUT + (oi - 3)]
                dst2 = out_refs[_N_FLAT_OUT + 3 + (oi - 3)]
                stage_ref[st] = x
                for r in range(f):
                    xr = stage_ref[st, pl.ds(r, tm // f, stride=f), :]
                    dst1[r, :, cols] = xr.astype(BF16)
                    stage2_ref[st * f + r] = xr
                    for r2 in range(f):
                        dst2[r + f * r2, :, cols] = stage2_ref[
                            st * f + r, pl.ds(r2, tm // (f * f), stride=f), :].astype(BF16)

        def normed_pair(c):
            cols = slice(c * LANES, (c + 2) * LANES)
            x2 = p[:, cols]
            ss = jnp.dot((x2 * x2).astype(BF16), seg, preferred_element_type=F32)
            x2 = x2 * lax.rsqrt(ss * (1.0 / HEAD_DIM) + RMS_EPS) * gq_ref[:, cols]
            finish(c, x2[:, :LANES])
            finish(c + 1, x2[:, LANES:])

        b_units = []
        c = 0
        while c < len(_CHUNKS):
            if _CHUNKS[c][2]:
                assert _CHUNKS[c + 1][2]
                b_units.append(functools.partial(normed_pair, c))
                c += 2
            else:
                b_units.append(functools.partial(lambda c: finish(c, p[:, c * LANES:(c + 1) * LANES]), c))
                c += 1
        a_units = [functools.partial(project, k) for k in range(IN_WIDTH // (2 * LANES))]
        for k in range(max(len(a_units), len(b_units))):
            if k < len(a_units):
                a_units[k]()
            if k < len(b_units):
                b_units[k]()

    @pl.when(i % 2 == 0)
    def _():
        step(pa_ref, pb_ref)

    @pl.when(i % 2 == 1)
    def _():
        step(pb_ref, pa_ref)


def _inproj(h, g_attn, w_in_bf16, gq_row, cos_t, sin_t, seg, batch, seq_len):
    n = h.shape[0]
    tm = TM_PROJ
    nsteps = n // tm
    tblocks = seq_len // tm
    widths = (W_NA, W_NA, W_NA, W_DIL, W_DIL, W_DIL, W_MEM)
    nst, f = 3 * W_DIL // LANES, _RESIDUE_DILS[0]
    prev = lambda i: jnp.maximum(i - 1, 0)
    out_specs = [pl.BlockSpec((tm, w), lambda i: (prev(i), 0)) for w in widths]
    out_shape = [jax.ShapeDtypeStruct((n, w), BF16) for w in widths]
    for d in _RESIDUE_DILS:
        out_specs += [pl.BlockSpec((None, d, tm // d, W_DIL),
                                   lambda i: (prev(i) // tblocks, 0, prev(i) % tblocks, 0))] * 3
        out_shape += [jax.ShapeDtypeStruct((batch, d, seq_len // d, W_DIL), BF16)] * 3
    return pl.pallas_call(
        _inproj_kernel,
        grid=(nsteps + 1,),
        in_specs=[
            pl.BlockSpec((tm, D_MODEL), lambda i: (jnp.minimum(i, nsteps - 1), 0)),
            pl.BlockSpec((1, D_MODEL), lambda i: (0, 0)),
            pl.BlockSpec((D_MODEL, IN_WIDTH), lambda i: (0, 0)),
            pl.BlockSpec((1, IN_WIDTH), lambda i: (0, 0)),
            pl.BlockSpec((tm, LANES), lambda i: (prev(i) % tblocks, 0)),
            pl.BlockSpec((tm, LANES), lambda i: (prev(i) % tblocks, 0)),
            pl.BlockSpec((2 * LANES, 2 * LANES), lambda i: (0, 0)),
        ],
        out_specs=out_specs,
        out_shape=out_shape,
        scratch_shapes=[pltpu.VMEM((nst, tm, LANES), F32), pltpu.VMEM((nst * f, tm // f, LANES), F32),
                        pltpu.VMEM((tm, IN_WIDTH), F32), pltpu.VMEM((tm, IN_WIDTH), F32)],
        compiler_params=_cparams(("arbitrary",)),
        name="inproj",
    )(h, g_attn, w_in_bf16, gq_row, cos_t, sin_t, seg)


def _stack_heads(q):
    first = lax.broadcasted_iota(jnp.int32, (1, LANES), 1) < HEAD_DIM
    zero = jnp.zeros_like(q)
    return jnp.concatenate([jnp.where(first, q, zero), jnp.where(first, zero, q)], axis=0)


def _with_ones(v):
    return jnp.concatenate([v, jnp.ones_like(v)], axis=1)


def _attend_all(problems, consume, want_lse, skew):
    def logits(n):
        q, k, bias, v_ones = problems[n]()
        s = lax.dot_general(q, k, (((1,), (1,)), ((), ())), preferred_element_type=F32)
        return (s if bias is None else s + bias), v_ones

    ahead = logits(0) if skew else None
    for n in range(len(problems)):
        s, v_ones = ahead if skew else logits(n)
        if skew and n + 1 < len(problems):
            ahead = logits(n + 1)
        m = jnp.max(s, axis=-1, keepdims=True)
        p = jnp.exp2(s - m).astype(BF16)
        ol = jnp.dot(p, v_ones, preferred_element_type=F32)
        l = ol[:, LANES:]
        consume(n, ol[:, :LANES] / l, (m + jnp.log2(l)) if want_lse else None)


def _na_mem_kernel(q_ref, k_ref, v_ref, *rest, rows):
    bias_refs = rest[:NA_STEP_BLOCKS]
    qm_ref, km_ref, vm_ref, o_ref, om_ref = rest[NA_STEP_BLOCKS:]
    i = pl.program_id(1)
    first = lax.broadcasted_iota(jnp.int32, (1, LANES), 1) < HEAD_DIM
    n_tiles = W_NA // LANES
    n_mem = W_MEM // LANES

    def na_problem(sb, p, hh):
        srow = jnp.clip((i * NA_STEP_BLOCKS + sb) * NA_QROWS - NA_KH // 2, 0, rows - NA_KROWS)
        start = pl.multiple_of(srow * GRID_W, GRID_W)
        sl = slice(p * LANES, (p + 1) * LANES)
        q = q_ref[sb * NA_QB:(sb + 1) * NA_QB, sl]
        zero = jnp.zeros_like(q)
        q = jnp.where(first, q, zero) if hh == 0 else jnp.where(first, zero, q)
        return (q, k_ref[pl.ds(start, NA_KB), sl], bias_refs[sb][0, 2 * p + hh],
                _with_ones(v_ref[pl.ds(start, NA_KB), sl]))

    def mem_problem(sb, c):
        sl = slice(c * LANES, (c + 1) * LANES)
        return (_stack_heads(qm_ref[sb * NA_QB:(sb + 1) * NA_QB, sl]), km_ref[:, sl], None,
                _with_ones(vm_ref[:, sl]))

    problems, targets = [], []
    for sb in range(NA_STEP_BLOCKS):
        for p in range(n_tiles):
            for hh in range(2):
                problems.append(functools.partial(na_problem, sb, p, hh))
                targets.append(("na", sb, p, hh))
        for c in range(n_mem):
            problems.append(functools.partial(mem_problem, sb, c))
            targets.append(("mem", sb, c, None))
    held = {}

    def consume(n, o, _):
        kind, sb, t, hh = targets[n]
        qrows = slice(sb * NA_QB, (sb + 1) * NA_QB)
        sl = slice(t * LANES, (t + 1) * LANES)
        if kind == "mem":
            om_ref[qrows, sl] = jnp.where(first, o[:NA_QB], o[NA_QB:]).astype(BF16)
        elif hh == 0:
            held[(sb, t)] = o
        else:
            o_ref[qrows, sl] = jnp.where(first, held.pop((sb, t)), o).astype(BF16)

    _attend_all(problems, consume, False, skew=True)


def _na_mem_attention(q, k, v, bias, layer, qm, km, vm, batch, seq_len, mem_len):
    rows = seq_len // GRID_W
    nblk = seq_len // NA_QB
    step = NA_STEP_BLOCKS * NA_QB
    q3, k3, v3 = (a.reshape(batch, seq_len, W_NA) for a in (q, k, v))

    def bias_spec(sb):
        def index(b, i):
            blk = i * NA_STEP_BLOCKS + sb
            return (layer, (blk > 0).astype(jnp.int32) + (blk == nblk - 1).astype(jnp.int32), 0, 0, 0)
        return pl.BlockSpec((None, 1, H_NA, NA_QB, NA_KB), index)

    o, om = pl.pallas_call(
        functools.partial(_na_mem_kernel, rows=rows),
        grid=(batch, nblk // NA_STEP_BLOCKS),
        in_specs=[
            pl.BlockSpec((None, step, W_NA), lambda b, i: (b, i, 0)),
            pl.BlockSpec((None, seq_len, W_NA), lambda b, i: (b, 0, 0)),
            pl.BlockSpec((None, seq_len, W_NA), lambda b, i: (b, 0, 0)),
            *[bias_spec(sb) for sb in range(NA_STEP_BLOCKS)],
            pl.BlockSpec((None, step, W_MEM), lambda b, i: (b, i, 0)),
            pl.BlockSpec((None, mem_len, W_MEM), lambda b, i: (b, 0, 0)),
            pl.BlockSpec((None, mem_len, W_MEM), lambda b, i: (b, 0, 0)),
        ],
        out_specs=[pl.BlockSpec((None, step, W_NA), lambda b, i: (b, i, 0)),
                   pl.BlockSpec((None, step, W_MEM), lambda b, i: (b, i, 0))],
        out_shape=[jax.ShapeDtypeStruct((batch, seq_len, W_NA), BF16),
                   jax.ShapeDtypeStruct((batch, seq_len, W_MEM), BF16)],
        compiler_params=_cparams(("parallel", "arbitrary")),
        name="na_mem_attention",
    )(q3, k3, v3, *([bias] * NA_STEP_BLOCKS), qm.reshape(batch, seq_len, W_MEM),
      km.reshape(batch, mem_len, W_MEM), vm.reshape(batch, mem_len, W_MEM))
    return o.reshape(batch * seq_len, W_NA), om.reshape(batch * seq_len, W_MEM)


def _na_bias_tables(rpb, rows):
    depth = rpb.shape[0]
    nblk = rows // NA_QROWS
    qc = np.arange(GRID_W)[:, None]
    kc = np.arange(GRID_W)[None, :]
    cs = np.clip(qc - NA_KW // 2, 0, GRID_W - NA_KW)
    col_valid = (kc >= cs) & (kc < cs + NA_KW)
    col_rel = kc - qc + (NA_KW - 1)
    col_sel = (col_rel[None] == np.arange(2 * NA_KW - 1)[:, None, None]).astype(np.float32)
    t1 = jnp.einsum('lhab,bqk->lhaqk', rpb, jnp.asarray(col_sel), precision=lax.Precision.HIGHEST)
    t1 = jnp.where(jnp.asarray(col_valid), t1 * LOG2E, NEG_INF)
    masked = jnp.full((depth, H_NA, GRID_W, GRID_W), NEG_INF, F32)
    tables = []
    for blk in (0, 1, nblk - 1):
        r0 = blk * NA_QROWS
        srow = int(np.clip(r0 - NA_KH // 2, 0, rows - NA_KROWS))
        slabs = []
        for qr in range(r0, r0 + NA_QROWS):
            rs = int(np.clip(qr - NA_KH // 2, 0, rows - NA_KH))
            slabs.append(jnp.concatenate(
                [t1[:, :, kr - qr + NA_KH - 1] if rs <= kr < rs + NA_KH else masked
                 for kr in range(srow, srow + NA_KROWS)], axis=-1))
        tables.append(jnp.stack(slabs, axis=2))
    return jnp.stack(tables, axis=1).reshape(depth, 3, H_NA, NA_QB, NA_KB)


def _band_kernel(q_ref, k_ref, v_ref, mask_ref, o_ref, lse_ref, *, length):
    j = pl.program_id(1)
    step = q_ref.shape[0]
    n_tiles = W_DIL // LANES
    first = lax.broadcasted_iota(jnp.int32, (1, LANES), 1) < HEAD_DIM
    lane = lax.broadcasted_iota(jnp.int32, (1, LANES), 1)

    def problem(sb, p):
        q0 = j * step + sb * DIL_QB
        start = pl.multiple_of(jnp.clip(q0 - DIL_HALF, 0, length - DIL_KB), DIL_HALF)
        variant = (q0 > 0).astype(jnp.int32) + (q0 == length - DIL_QB).astype(jnp.int32)
        mask = mask_ref[variant]
        sl = slice(p * LANES, (p + 1) * LANES)
        return (_stack_heads(q_ref[sb * DIL_QB:(sb + 1) * DIL_QB, sl]), k_ref[pl.ds(start, DIL_KB), sl],
                jnp.concatenate([mask, mask], axis=0), _with_ones(v_ref[pl.ds(start, DIL_KB), sl]))

    problems = [functools.partial(problem, sb, p) for sb in range(step // DIL_QB) for p in range(n_tiles)]
    lse_heads = {}

    def consume(n, o, lse):
        sb, p = divmod(n, n_tiles)
        rows = slice(sb * DIL_QB, (sb + 1) * DIL_QB)
        o_ref[rows, p * LANES:(p + 1) * LANES] = jnp.where(first, o[:DIL_QB], o[DIL_QB:]).astype(BF16)
        acc = lse_heads.pop(sb, jnp.zeros((DIL_QB, LANES), F32))
        acc = jnp.where(lane == 2 * p, lse[:DIL_QB], jnp.where(lane == 2 * p + 1, lse[DIL_QB:], acc))
        if p == n_tiles - 1:
            lse_ref[rows, :] = acc
        else:
            lse_heads[sb] = acc

    _attend_all(problems, consume, True, skew=False)


def _band_attention(q, k, v, masks):
    nseq, length, _ = q.shape
    step = min(DIL_STEP, length)
    nblk = length // step
    blk = pl.BlockSpec((None, step, W_DIL), lambda s, j: (s, j, 0))
    seq = pl.BlockSpec((None, length, W_DIL), lambda s, j: (s, 0, 0))
    return pl.pallas_call(
        functools.partial(_band_kernel, length=length),
        grid=(nseq, nblk),
        in_specs=[blk, seq, seq, pl.BlockSpec((3, DIL_QB, DIL_KB), lambda s, j: (0, 0, 0))],
        out_specs=[blk, pl.BlockSpec((None, step, LANES), lambda s, j: (s, j, 0))],
        out_shape=[jax.ShapeDtypeStruct((nseq, length, W_DIL), BF16),
                   jax.ShapeDtypeStruct((nseq, length, LANES), F32)],
        compiler_params=_cparams(("parallel", "arbitrary")),
        name="band_attention",
    )(q, k, v, masks)


def _band_masks():
    qq = np.arange(DIL_QB)[:, None]
    kk = np.arange(DIL_KB)[None, :]
    out = []
    for off in (0, -DIL_HALF, DIL_QB - DIL_KB):
        rel = kk + off - qq
        out.append(np.where(np.abs(rel) <= DIL_HALF, 0.0, NEG_INF))
    return jnp.asarray(np.stack(out), F32)


def _memkv_kernel(mem_ref, g_ref, w_ref, gk_ref, seg_ref, km_ref, vm_ref):
    mn = _rms(mem_ref[...], g_ref[...]).astype(BF16)
    kv = jnp.dot(mn, w_ref[...], preferred_element_type=F32)
    for c in range(W_MEM // LANES):
        x = kv[:, c * LANES:(c + 1) * LANES]
        ss = jnp.dot((x * x).astype(BF16), seg_ref[:LANES, :LANES], preferred_element_type=F32)
        x = x * lax.rsqrt(ss * (1.0 / HEAD_DIM) + RMS_EPS) * gk_ref[...]
        km_ref[:, c * LANES:(c + 1) * LANES] = x.astype(BF16)
    vm_ref[...] = kv[:, W_MEM:].astype(BF16)


def _memkv(mem2, g_mem, w_kv_bf16, gk_row, seg):
    rows = mem2.shape[0]
    full = lambda shape: pl.BlockSpec(shape, lambda i: (0,) * len(shape))
    return pl.pallas_call(
        _memkv_kernel,
        grid=(1,),
        in_specs=[full((rows, D_MODEL)), full((1, D_MODEL)), full((D_MODEL, 2 * W_MEM)),
                  full((1, LANES)), full((2 * LANES, 2 * LANES))],
        out_specs=[full((rows, W_MEM)), full((rows, W_MEM))],
        out_shape=[jax.ShapeDtypeStruct((rows, W_MEM), BF16)] * 2,
        compiler_params=_cparams(("arbitrary",)),
        name="mem_kv",
    )(mem2, g_mem, w_kv_bf16, gk_row, seg)


def _outproj_kernel(ona_ref, *rest, with_router):
    nd = 2 * len(DILATIONS)
    dil_refs, rest = rest[:nd], rest[nd:]
    om_ref, h_ref, go_ref, w_ref, gf_ref = rest[:5]
    rest = rest[5:]
    if with_router:
        wr_ref, hn_ref, u_ref, lg_ref = rest[:4]
        stage_refs = rest[4:]
    else:
        hn_ref, u_ref = rest[:2]
        stage_refs = rest[2:]
    stage_refs, tmp_ref = stage_refs[:-1], stage_refs[-1]
    tm = h_ref.shape[0]
    f = _RESIDUE_DILS[0]
    nc = W_DIL // LANES
    go = go_ref[...]
    spread = (lax.broadcasted_iota(jnp.int32, (LANES, W_DIL), 0)
              == lax.broadcasted_iota(jnp.int32, (LANES, W_DIL), 1) // HEAD_DIM).astype(BF16)

    def chain(r0, th):
        rows = slice(r0, r0 + th)

        def token_order(ref, d, st, ncols):
            if d == 1:
                return ref[rows, :].astype(F32)
            for c in range(ncols):
                cols = slice(c * LANES, (c + 1) * LANES)
                for r in range(f):
                    if d == f:
                        st[c, pl.ds(r0 + r, th // f, stride=f), :] = ref[
                            r, r0 // f:(r0 + th) // f, cols].astype(F32)
                    else:
                        for r2 in range(f):
                            tmp_ref[c * f + r, pl.ds(r0 // f + r2, th // d, stride=f), :] = ref[
                                r + f * r2, r0 // d:(r0 + th) // d, cols].astype(F32)
                        st[c, pl.ds(r0 + r, th // f, stride=f), :] = tmp_ref[
                            c * f + r, r0 // f:(r0 + th) // f, :]
            return jnp.concatenate([st[c, rows, :] for c in range(ncols)], axis=-1)

        def per_head_to_lanes(w):
            w_hi = w.astype(BF16)
            w_lo = (w - w_hi.astype(F32)).astype(BF16)
            return (jnp.dot(w_hi, spread, preferred_element_type=F32)
                    + jnp.dot(w_lo, spread, preferred_element_type=F32))

        os_, ls_ = [], []
        for di, d in enumerate(DILATIONS):
            o_ref_d, l_ref_d = dil_refs[2 * di:2 * di + 2]
            o_st, l_st = (stage_refs[2 * (di - 1)], stage_refs[2 * (di - 1) + 1]) if d > 1 else (None, None)
            os_.append(token_order(o_ref_d, d, o_st, nc))
            ls_.append(token_order(l_ref_d, d, l_st, 1))
        mx = functools.reduce(jnp.maximum, ls_)
        es = [jnp.exp2(l - mx) for l in ls_]
        inv = 1.0 / sum(es)
        odil = sum(per_head_to_lanes(e * inv) * o for e, o in zip(es, os_))
        mixed = jnp.concatenate([
            _rms(ona_ref[rows, :].astype(F32), go[:, :W_NA]),
            _rms(odil, go[:, W_NA:W_NA + W_DIL]),
            _rms(om_ref[rows, :].astype(F32), go[:, W_NA + W_DIL:]),
        ], axis=-1).astype(BF16)
        yield
        hn = h_ref[rows, :] + jnp.dot(mixed, w_ref[...], preferred_element_type=F32)
        hn_ref[rows, :] = hn
        yield
        u = _rms(hn, gf_ref[...])
        if with_router:
            u_ref[rows, :] = u
            u_hi = u.astype(BF16)
            u_lo = (u - u_hi.astype(F32)).astype(BF16)
            lg_ref[rows, :] = (jnp.dot(u_hi, wr_ref[...], preferred_element_type=F32)
                               + jnp.dot(u_lo, wr_ref[...], preferred_element_type=F32))
        else:
            u_ref[rows, :] = u.astype(BF16)

    th = tm // PROJ_CHAINS
    chains = [chain(ci * th, th) for ci in range(PROJ_CHAINS)]
    while chains:
        for ch in list(chains):
            if next(ch, "done") == "done":
                chains.remove(ch)


def _outproj(ona, dil, om, h, g_out, w_out_bf16, g_ffn, w_router_pad, batch, seq_len):
    n = h.shape[0]
    tm = TM_PROJ
    tblocks = seq_len // tm
    with_router = w_router_pad is not None
    row = lambda w: pl.BlockSpec((tm, w), lambda i: (i, 0))
    full = lambda shape: pl.BlockSpec(shape, lambda i: (0,) * len(shape))
    in_specs = [row(W_NA)]
    args = [ona]
    scratch = []
    for di, d in enumerate(DILATIONS):
        for a, w in zip(dil[2 * di:2 * di + 2], (W_DIL, LANES)):
            if d == 1:
                in_specs.append(row(w))
                args.append(a.reshape(n, w))
            else:
                in_specs.append(pl.BlockSpec((None, d, tm // d, w),
                                             lambda i: (i // tblocks, 0, i % tblocks, 0)))
                args.append(a.reshape(batch, d, seq_len // d, w))
                scratch.append(pltpu.VMEM((w // LANES, tm, LANES), F32))
    f = _RESIDUE_DILS[0]
    scratch.append(pltpu.VMEM((W_DIL // LANES * f, tm // f, LANES), F32))
    in_specs += [row(W_MEM), row(D_MODEL), full((1, D_MODEL)), full((D_MODEL, D_MODEL)),
                 full((1, D_MODEL))]
    args += [om, h, g_out, w_out_bf16, g_ffn]
    out_specs = [row(D_MODEL), row(D_MODEL)]
    out_shape = [jax.ShapeDtypeStruct((n, D_MODEL), F32),
                 jax.ShapeDtypeStruct((n, D_MODEL), F32 if with_router else BF16)]
    if with_router:
        in_specs.append(full((D_MODEL, LANES)))
        args.append(w_router_pad)
        out_specs.append(row(LANES))
        out_shape.append(jax.ShapeDtypeStruct((n, LANES), F32))
    return pl.pallas_call(
        functools.partial(_outproj_kernel, with_router=with_router),
        grid=(n // tm,),
        in_specs=in_specs,
        out_specs=out_specs,
        out_shape=out_shape,
        scratch_shapes=scratch,
        compiler_params=_cparams(("parallel",)),
        name="outproj",
    )(*args)


def _swiglu(x, wg, wu, wd):
    g = jnp.dot(x, wg, preferred_element_type=F32)
    u = jnp.dot(x, wu, preferred_element_type=F32)
    hm = (g * jax.nn.sigmoid(g) * u).astype(BF16)
    return jnp.dot(hm, wd, preferred_element_type=F32)


def _dense_ffn_kernel(u_ref, h_ref, wg_ref, wu_ref, wd_ref, o_ref, hm_ref):
    x = u_ref[...]
    for c in range(hm_ref.shape[1] // TF_DENSE):
        cols = slice(c * TF_DENSE, (c + 1) * TF_DENSE)
        g = jnp.dot(x, wg_ref[:, cols], preferred_element_type=F32)
        u = jnp.dot(x, wu_ref[:, cols], preferred_element_type=F32)
        hm_ref[:, cols] = (g * jax.nn.sigmoid(g) * u).astype(BF16)
    o_ref[...] = h_ref[...] + jnp.dot(hm_ref[...], wd_ref[...], preferred_element_type=F32)


def _dense_ffn(u, h, wg_bf16, wu_bf16, wd_bf16):
    n = h.shape[0]
    d_ff = wg_bf16.shape[1]
    tm = TM_FFN
    full = lambda shape: pl.BlockSpec(shape, lambda i: (0,) * len(shape))
    return pl.pallas_call(
        _dense_ffn_kernel,
        grid=(n // tm,),
        in_specs=[
            pl.BlockSpec((tm, D_MODEL), lambda i: (i, 0)),
            pl.BlockSpec((tm, D_MODEL), lambda i: (i, 0)),
            full((D_MODEL, d_ff)), full((D_MODEL, d_ff)), full((d_ff, D_MODEL)),
        ],
        out_specs=pl.BlockSpec((tm, D_MODEL), lambda i: (i, 0)),
        out_shape=jax.ShapeDtypeStruct((n, D_MODEL), F32),
        scratch_shapes=[pltpu.VMEM((tm, d_ff), BF16)],
        compiler_params=_cparams(("parallel",)),
        name="dense_ffn",
    )(u, h, wg_bf16, wu_bf16, wd_bf16)


def _dispatch_kernel(vend_ref, pend_ref, d0_ref, d1_ref, u_ref, x_hbm, zbuf, sem, zsem):
    i = pl.program_id(0)
    tc = u_ref.shape[0]
    zr = zbuf.shape[0]
    n_rows = x_hbm.shape[0]

    def zero_pieces(lo, hi):
        def start(p, carry):
            pltpu.make_async_copy(zbuf, x_hbm.at[pl.ds(pl.multiple_of(p * zr, zr), zr), :], zsem).start()
            return carry

        def wait(p, carry):
            pltpu.make_async_copy(zbuf, x_hbm.at[pl.ds(0, zr), :], zsem).wait()
            return carry

        lax.fori_loop(lo, hi, start, 0)
        lax.fori_loop(lo, hi, wait, 0)

    @pl.when(i == 0)
    def _():
        zbuf[...] = jnp.zeros_like(zbuf)
        for e in range(N_EXPERTS):
            zero_pieces(vend_ref[e] // zr, pend_ref[e] // zr)
        zero_pieces(pend_ref[N_EXPERTS - 1] // zr, n_rows // zr)

    def issue(r, carry):
        src = u_ref.at[pl.ds(r, 1), :]
        pltpu.make_async_copy(src, x_hbm.at[pl.ds(d0_ref[0, 0, r], 1), :], sem).start(priority=0)
        pltpu.make_async_copy(src, x_hbm.at[pl.ds(d1_ref[0, 0, r], 1), :], sem).start(priority=1)
        return carry

    lax.fori_loop(0, tc, issue, 0, unroll=ISSUE_UNROLL)
    for _ in range(TOP_K):
        pltpu.make_async_copy(u_ref, x_hbm.at[pl.ds(0, tc), :], sem).wait()


def _dispatch(u, dest, vend, pend, n_rows):
    n = u.shape[0]
    tc = TC_COMBINE
    nblk = n // tc
    d0 = dest[:, 0].reshape(nblk, 1, tc)
    d1 = dest[:, 1].reshape(nblk, 1, tc)
    smem = pl.BlockSpec((1, 1, tc), lambda i, ve, pe: (i, 0, 0), memory_space=pltpu.SMEM)
    grid_spec = pltpu.PrefetchScalarGridSpec(
        num_scalar_prefetch=2,
        grid=(nblk,),
        in_specs=[smem, smem, pl.BlockSpec((tc, D_MODEL), lambda i, ve, pe: (i, 0))],
        out_specs=pl.BlockSpec(memory_space=pl.ANY),
        scratch_shapes=[pltpu.VMEM((ZERO_ROWS, D_MODEL), F32), pltpu.SemaphoreType.DMA(()),
                        pltpu.SemaphoreType.DMA(())],
    )
    return pl.pallas_call(
        _dispatch_kernel,
        grid_spec=grid_spec,
        out_shape=jax.ShapeDtypeStruct((n_rows, D_MODEL), F32),
        compiler_params=_cparams(("arbitrary",)),
        name="moe_dispatch",
    )(vend, pend, d0, d1, u)


def _moe_ffn_kernel(be_ref, nv_ref, nu_ref, x_ref, wg_ref, wu_ref, wd_ref, y_ref, xb):
    i = pl.program_id(0)
    j = pl.program_id(1)
    active = i < nu_ref[0]
    tm = x_ref.shape[0]
    nchunk = tm // MOE_CHUNK

    def chunk(c, wg, wu, wd):
        rows = slice(c * MOE_CHUNK, (c + 1) * MOE_CHUNK)
        y_ref[rows, :] += _swiglu(xb[rows, :], wg, wu, wd)

    @pl.when(j == 0)
    def _():
        y_ref[...] = jnp.zeros_like(y_ref)

    @pl.when(active)
    def _():
        @pl.when(j == 0)
        def _():
            xb[...] = x_ref[...].astype(BF16)

        nvalid = nv_ref[i]

        @pl.when(nvalid > tm - MOE_CHUNK)
        def _():
            wg, wu, wd = (r[...].astype(BF16) for r in (wg_ref, wu_ref, wd_ref))

            def gate_up(c):
                x = xb[c * MOE_CHUNK:(c + 1) * MOE_CHUNK, :]
                return (jnp.dot(x, wg, preferred_element_type=F32),
                        jnp.dot(x, wu, preferred_element_type=F32))

            ahead = gate_up(0)
            for c in range(nchunk):
                g, u = ahead
                if c + 1 < nchunk:
                    ahead = gate_up(c + 1)
                hm = (g * jax.nn.sigmoid(g) * u).astype(BF16)
                y_ref[c * MOE_CHUNK:(c + 1) * MOE_CHUNK, :] += jnp.dot(hm, wd, preferred_element_type=F32)

        @pl.when(nvalid <= tm - MOE_CHUNK)
        def _():
            wg, wu, wd = (r[...].astype(BF16) for r in (wg_ref, wu_ref, wd_ref))
            for c in range(nchunk):
                @pl.when(c * MOE_CHUNK < nvalid)
                def _():
                    chunk(c, wg, wu, wd)


def _moe_ffn(x_sorted, blk_expert, blk_valid, n_used, wg, wu, wd):
    n_rows = x_sorted.shape[0]
    tm, tf = TM_MOE, TF_MOE
    nblk = n_rows // tm
    d_ff = wg.shape[2]
    nj = d_ff // tf

    def jeff(i, j, nu):
        return jnp.where(i < nu[0], j, nj - 1)

    grid_spec = pltpu.PrefetchScalarGridSpec(
        num_scalar_prefetch=3,
        grid=(nblk, nj),
        in_specs=[
            pl.BlockSpec((tm, D_MODEL), lambda i, j, be, nv, nu: (jnp.minimum(i, nu[0] - 1), 0)),
            pl.BlockSpec((None, D_MODEL, tf), lambda i, j, be, nv, nu: (be[i], 0, jeff(i, j, nu))),
            pl.BlockSpec((None, D_MODEL, tf), lambda i, j, be, nv, nu: (be[i], 0, jeff(i, j, nu))),
            pl.BlockSpec((None, tf, D_MODEL), lambda i, j, be, nv, nu: (be[i], jeff(i, j, nu), 0)),
        ],
        out_specs=pl.BlockSpec((tm, D_MODEL), lambda i, j, be, nv, nu: (i, 0)),
        scratch_shapes=[pltpu.VMEM((tm, D_MODEL), BF16)],
    )
    return pl.pallas_call(
        _moe_ffn_kernel,
        grid_spec=grid_spec,
        out_shape=jax.ShapeDtypeStruct((n_rows, D_MODEL), F32),
        compiler_params=_cparams(("arbitrary", "arbitrary")),
        name="moe_ffn",
    )(blk_expert, blk_valid, n_used, x_sorted, wg, wu, wd)


def _combine_kernel(p0_ref, p1_ref, h_ref, g_ref, y_hbm, o_ref, buf0, buf1, sem):
    tc = buf0.shape[0]

    def issue(r, carry):
        pltpu.make_async_copy(y_hbm.at[pl.ds(p0_ref[0, 0, r], 1), :], buf0.at[pl.ds(r, 1), :],
                              sem.at[0]).start(priority=0)
        pltpu.make_async_copy(y_hbm.at[pl.ds(p1_ref[0, 0, r], 1), :], buf1.at[pl.ds(r, 1), :],
                              sem.at[1]).start(priority=1)
        return carry

    lax.fori_loop(0, tc, issue, 0, unroll=ISSUE_UNROLL)
    pltpu.make_async_copy(y_hbm.at[pl.ds(0, tc), :], buf0, sem.at[0]).wait()
    pltpu.make_async_copy(y_hbm.at[pl.ds(0, tc), :], buf1, sem.at[1]).wait()
    g = g_ref[...]
    o_ref[...] = h_ref[...] + g[:, 0:1] * buf0[...] + g[:, 1:2] * buf1[...]


def _combine(h, y, dest, gates):
    n = h.shape[0]
    tc = TC_COMBINE
    nblk = n // tc
    p0 = dest[:, 0].reshape(nblk, 1, tc)
    p1 = dest[:, 1].reshape(nblk, 1, tc)
    smem = pl.BlockSpec((1, 1, tc), lambda i: (i, 0, 0), memory_space=pltpu.SMEM)
    return pl.pallas_call(
        _combine_kernel,
        grid=(nblk,),
        in_specs=[smem, smem, pl.BlockSpec((tc, D_MODEL), lambda i: (i, 0)),
                  pl.BlockSpec((tc, TOP_K), lambda i: (i, 0)), pl.BlockSpec(memory_space=pl.ANY)],
        out_specs=pl.BlockSpec((tc, D_MODEL), lambda i: (i, 0)),
        out_shape=jax.ShapeDtypeStruct((n, D_MODEL), F32),
        scratch_shapes=[pltpu.VMEM((tc, D_MODEL), F32), pltpu.VMEM((tc, D_MODEL), F32),
                        pltpu.SemaphoreType.DMA((2,))],
        compiler_params=_cparams(("arbitrary",)),
        name="moe_combine",
    )(p0, p1, h, gates, y)


def _route(logits):
    n = logits.shape[0]
    top_logit, top_idx = lax.top_k(logits, TOP_K)
    gates = jax.nn.softmax(top_logit, axis=-1)
    e_flat = top_idx.reshape(n * TOP_K).astype(jnp.int32)
    onehot = (e_flat[:, None] == jnp.arange(N_EXPERTS, dtype=jnp.int32)[None, :]).astype(jnp.int32)
    csum = jnp.cumsum(onehot, axis=0)
    counts = csum[-1]
    padded = (counts + TM_MOE - 1) // TM_MOE * TM_MOE
    pend = jnp.cumsum(padded)
    pstart = pend - padded
    dest = jnp.sum(onehot * (csum - 1 + pstart[None, :]), axis=1).reshape(n, TOP_K)
    n_rows = n * TOP_K + N_EXPERTS * TM_MOE
    nblk = n_rows // TM_MOE
    n_used = pend[-1] // TM_MOE
    blk_start = jnp.minimum(jnp.arange(nblk, dtype=jnp.int32) * TM_MOE, (n_used - 1) * TM_MOE)
    blk_expert = jnp.minimum(
        jnp.sum((blk_start[:, None] >= pend[None, :]).astype(jnp.int32), axis=1), N_EXPERTS - 1)
    blk_valid = jnp.clip(counts[blk_expert] - (blk_start - pstart[blk_expert]), 0, TM_MOE)
    i32 = lambda a: a.astype(jnp.int32)
    return (i32(dest), gates, i32(pstart + counts), i32(pend), i32(blk_expert), i32(blk_valid),
            i32(n_used).reshape(1), n_rows)


def _rope_tables(seq_len):
    half = HEAD_DIM // 2
    inv_freq = jnp.power(ROPE_THETA, -(2.0 / HEAD_DIM) * jnp.arange(half, dtype=F32))
    ang = jnp.arange(seq_len, dtype=F32)[:, None] * inv_freq[None, :]
    cos, sin = jnp.cos(ang), jnp.sin(ang)
    cos_t = jnp.tile(cos, (1, LANES // half))
    sin_t = jnp.tile(jnp.concatenate([-sin, sin], axis=1), (1, LANES // HEAD_DIM))
    return cos_t, sin_t


def kernel(x, mem, g_attn, w_in, g_qk_na, rpb_na, g_qk_dil, g_mem, w_mem_kv, g_qk_mem, g_out, w_out,
           g_ffn, w_gate_dense, w_up_dense, w_down_dense, w_router, w_gate_moe, w_up_moe, w_down_moe):
    batch, seq_len, _ = x.shape
    mem_len = mem.shape[1]
    depth = g_attn.shape[0]
    n = batch * seq_len
    scale = LOG2E * HEAD_DIM ** -0.5

    cos_t, sin_t = _rope_tables(seq_len)
    seg = jnp.asarray(np.kron(np.eye(2 * LANES // HEAD_DIM), np.ones((HEAD_DIM, HEAD_DIM))), BF16)
    band_masks = _band_masks()
    na_bias = _na_bias_tables(rpb_na, seq_len // GRID_W)
    ones = lambda w: jnp.ones((w,), F32)

    h = x.reshape(n, D_MODEL)
    mem2 = mem.reshape(batch * mem_len, D_MODEL)
    for layer in range(depth):
        gq_row = jnp.concatenate([
            jnp.tile(g_qk_na[layer, 0] * scale, H_NA), jnp.tile(g_qk_na[layer, 1], H_NA), ones(W_NA),
            jnp.tile(g_qk_dil[layer, 0] * scale, H_DIL), jnp.tile(g_qk_dil[layer, 1], H_DIL), ones(W_DIL),
            jnp.tile(g_qk_mem[layer, 0] * scale, H_MEM)]).reshape(1, IN_WIDTH)
        proj = _inproj(h, g_attn[layer].reshape(1, D_MODEL), w_in[layer].astype(BF16), gq_row, cos_t, sin_t,
                       seg, batch, seq_len)
        qa, ka, va, qb, kb, vb, qm = proj[:_N_FLAT_OUT]

        dil = []
        for d in DILATIONS:
            if d == 1:
                qkv = [a.reshape(batch, seq_len, W_DIL) for a in (qb, kb, vb)]
            else:
                base = _N_FLAT_OUT + 3 * _RESIDUE_DILS.index(d)
                qkv = [a.reshape(batch * d, seq_len // d, W_DIL) for a in proj[base:base + 3]]
            dil += _band_attention(*qkv, band_masks)

        km, vm = _memkv(mem2, g_mem[layer].reshape(1, D_MODEL), w_mem_kv[layer].astype(BF16),
                        jnp.tile(g_qk_mem[layer, 1], LANES // HEAD_DIM).reshape(1, LANES), seg)
        o_na, o_mem = _na_mem_attention(qa, ka, va, na_bias, layer, qm, km, vm, batch, seq_len, mem_len)

        i = layer // 2
        moe = layer % 2 == 1
        w_router_pad = None
        if moe:
            wr_hi = w_router[i].astype(BF16)
            wr_lo = (w_router[i] - wr_hi.astype(F32)).astype(BF16)
            w_router_pad = jnp.pad(jnp.concatenate([wr_hi, wr_lo], axis=1),
                                   ((0, 0), (0, LANES - 2 * N_EXPERTS)))
        outs = _outproj(o_na, dil, o_mem, h, g_out[layer].reshape(1, D_MODEL), w_out[layer].astype(BF16),
                        g_ffn[layer].reshape(1, D_MODEL), w_router_pad, batch, seq_len)
        if not moe:
            h, u = outs
            h = _dense_ffn(u, h, w_gate_dense[i].astype(BF16), w_up_dense[i].astype(BF16),
                           w_down_dense[i].astype(BF16))
        else:
            h, u, logits = outs
            dest, gates, vend, pend, blk_expert, blk_valid, n_used, n_rows = _route(
                logits[:, :N_EXPERTS] + logits[:, N_EXPERTS:2 * N_EXPERTS])
            x_sorted = _dispatch(u, dest, vend, pend, n_rows)
            y = _moe_ffn(x_sorted, blk_expert, blk_valid, n_used, w_gate_moe[i], w_up_moe[i], w_down_moe[i])
            h = _combine(h, y, dest, gates)
    return h.reshape(batch, seq_len, D_MODEL)
```

```python
import functools

import numpy as np
import jax
import jax.numpy as jnp
from jax import lax
from jax.experimental import pallas as pl
from jax.experimental.pallas import tpu as pltpu

F32 = jnp.float32
BF16 = jnp.bfloat16

D_MODEL = 1024
HEAD_DIM = 64
H_NA, H_DIL, H_MEM = 6, 6, 4
W_NA, W_DIL, W_MEM = H_NA * HEAD_DIM, H_DIL * HEAD_DIM, H_MEM * HEAD_DIM
IN_WIDTH = 3 * W_NA + 3 * W_DIL + W_MEM
GRID_W = 64
NA_KH, NA_KW = 8, 16
DIL_CFG = ((128, 1), (512, 4), (2048, 16))
ROPE_THETA = 10000.0
N_EXPERTS = 8
TOP_K = 2
RMS_EPS = 1e-6
NEG_INF = -1e30
LOG2E = 1.4426950408889634

LANES = 128
TM_PROJ = 512
PROJ_CHAINS = 2
NA_QROWS = 4
NA_QB = NA_QROWS * GRID_W
NA_KROWS = 12
NA_KB = NA_KROWS * GRID_W
NA_STEP_BLOCKS = 2
DIL_QB = 128
DIL_HALF = 64
DIL_KB = DIL_QB + 2 * DIL_HALF
DIL_STEP = 2048
TM_FFN = 512
TF_DENSE = 256
TM_MOE = 2048
TF_MOE = 512
MOE_CHUNK = 256
TC_COMBINE = 1024
ZERO_ROWS = 256
ISSUE_UNROLL = 8
VMEM_LIMIT = 56 * 1024 * 1024


def _cparams(sem):
    return pltpu.CompilerParams(dimension_semantics=sem, vmem_limit_bytes=VMEM_LIMIT)


def _rms(x, g):
    return x * lax.rsqrt(jnp.mean(x * x, axis=-1, keepdims=True) + RMS_EPS) * g


_CHUNKS = (
    [(0, c, True, False) for c in range(3)] + [(1, c, True, False) for c in range(3)]
    + [(2, c, False, False) for c in range(3)]
    + [(3, c, True, True) for c in range(3)] + [(4, c, True, True) for c in range(3)]
    + [(5, c, False, False) for c in range(3)]
    + [(6, c, True, False) for c in range(2)]
)


DILATIONS = tuple(d for _, d in DIL_CFG)
_RESIDUE_DILS = tuple(d for d in DILATIONS if d > 1)
assert len(_RESIDUE_DILS) == 2 and _RESIDUE_DILS[1] == _RESIDUE_DILS[0] ** 2
_N_FLAT_OUT = 7


def _inproj_kernel(h_ref, ga_ref, w_ref, gq_ref, cos_ref, sin_ref, seg_ref, *rest):
    out_refs, stage_ref, stage2_ref, pa_ref, pb_ref = rest[:-4], rest[-4], rest[-3], rest[-2], rest[-1]
    i = pl.program_id(0)
    tm = h_ref.shape[0]
    seg = seg_ref[...]
    lane = lax.broadcasted_iota(jnp.int32, (1, LANES), 1)
    first_half = (lane % HEAD_DIM) < (HEAD_DIM // 2)
    f = _RESIDUE_DILS[0]

    @pl.when(i == 0)
    def _():
        pb_ref[...] = jnp.zeros_like(pb_ref)

    def step(p_new, p):
        u = _rms(h_ref[...], ga_ref[...]).astype(BF16)

        def project(k):
            cols = slice(k * 2 * LANES, (k + 1) * 2 * LANES)
            p_new[:, cols] = jnp.dot(u, w_ref[:, cols], preferred_element_type=F32)

        def finish(c, x):
            oi, oc, _, rope = _CHUNKS[c]
            cols = slice(oc * LANES, (oc + 1) * LANES)
            if rope:
                swapped = jnp.where(first_half, pltpu.roll(x, LANES - HEAD_DIM // 2, 1),
                                    pltpu.roll(x, HEAD_DIM // 2, 1))
                x = x * cos_ref[...] + swapped * sin_ref[...]
            out_refs[oi][:, cols] = x.astype(BF16)
            if 3 <= oi <= 5:
                st = (oi - 3) * (W_DIL // LANES) + oc
                dst1 = out_refs[_N_FLAT_OUT + (oi - 3)]
                dst2 = out_refs[_N_FLAT_OUT + 3 + (oi - 3)]
                stage_ref[st] = x
                for r in range(f):
                    xr = stage_ref[st, pl.ds(r, tm // f, stride=f), :]
                    dst1[r, :, cols] = xr.astype(BF16)
                    stage2_ref[st * f + r] = xr
                    for r2 in range(f):
                        dst2[r + f * r2, :, cols] = stage2_ref[
                            st * f + r, pl.ds(r2, tm // (f * f), stride=f), :].astype(BF16)

        def normed_pair(c):
            cols = slice(c * LANES, (c + 2) * LANES)
            x2 = p[:, cols]
            ss = jnp.dot((x2 * x2).astype(BF16), seg, preferred_element_type=F32)
            x2 = x2 * lax.rsqrt(ss * (1.0 / HEAD_DIM) + RMS_EPS) * gq_ref[:, cols]
            finish(c, x2[:, :LANES])
            finish(c + 1, x2[:, LANES:])

        b_units = []
        c = 0
        while c < len(_CHUNKS):
            if _CHUNKS[c][2]:
                assert _CHUNKS[c + 1][2]
                b_units.append(functools.partial(normed_pair, c))
                c += 2
            else:
                b_units.append(functools.partial(lambda c: finish(c, p[:, c * LANES:(c + 1) * LANES]), c))
                c += 1
        a_units = [functools.partial(project, k) for k in range(IN_WIDTH // (2 * LANES))]
        for k in range(max(len(a_units), len(b_units))):
            if k < len(a_units):
                a_units[k]()
            if k < len(b_units):
                b_units[k]()

    @pl.when(i % 2 == 0)
    def _():
        step(pa_ref, pb_ref)

    @pl.when(i % 2 == 1)
    def _():
        step(pb_ref, pa_ref)


def _inproj(h, g_attn, w_in_bf16, gq_row, cos_t, sin_t, seg, batch, seq_len):
    n = h.shape[0]
    tm = TM_PROJ
    nsteps = n // tm
    tblocks = seq_len // tm
    widths = (W_NA, W_NA, W_NA, W_DIL, W_DIL, W_DIL, W_MEM)
    nst, f = 3 * W_DIL // LANES, _RESIDUE_DILS[0]
    prev = lambda i: jnp.maximum(i - 1, 0)
    out_specs = [pl.BlockSpec((tm, w), lambda i: (prev(i), 0)) for w in widths]
    out_shape = [jax.ShapeDtypeStruct((n, w), BF16) for w in widths]
    for d in _RESIDUE_DILS:
        out_specs += [pl.BlockSpec((None, d, tm // d, W_DIL),
                                   lambda i: (prev(i) // tblocks, 0, prev(i) % tblocks, 0))] * 3
        out_shape += [jax.ShapeDtypeStruct((batch, d, seq_len // d, W_DIL), BF16)] * 3
    return pl.pallas_call(
        _inproj_kernel,
        grid=(nsteps + 1,),
        in_specs=[
            pl.BlockSpec((tm, D_MODEL), lambda i: (jnp.minimum(i, nsteps - 1), 0)),
            pl.BlockSpec((1, D_MODEL), lambda i: (0, 0)),
            pl.BlockSpec((D_MODEL, IN_WIDTH), lambda i: (0, 0)),
            pl.BlockSpec((1, IN_WIDTH), lambda i: (0, 0)),
            pl.BlockSpec((tm, LANES), lambda i: (prev(i) % tblocks, 0)),
            pl.BlockSpec((tm, LANES), lambda i: (prev(i) % tblocks, 0)),
            pl.BlockSpec((2 * LANES, 2 * LANES), lambda i: (0, 0)),
        ],
        out_specs=out_specs,
        out_shape=out_shape,
        scratch_shapes=[pltpu.VMEM((nst, tm, LANES), F32), pltpu.VMEM((nst * f, tm // f, LANES), F32),
                        pltpu.VMEM((tm, IN_WIDTH), F32), pltpu.VMEM((tm, IN_WIDTH), F32)],
        compiler_params=_cparams(("arbitrary",)),
        name="inproj",
    )(h, g_attn, w_in_bf16, gq_row, cos_t, sin_t, seg)


def _stack_heads(q):
    first = lax.broadcasted_iota(jnp.int32, (1, LANES), 1) < HEAD_DIM
    zero = jnp.zeros_like(q)
    return jnp.concatenate([jnp.where(first, q, zero), jnp.where(first, zero, q)], axis=0)


def _with_ones(v):
    return jnp.concatenate([v, jnp.ones_like(v)], axis=1)


def _attend_all(problems, consume, want_lse, skew):
    def logits(n):
        q, k, bias, v_ones = problems[n]()
        s = lax.dot_general(q, k, (((1,), (1,)), ((), ())), preferred_element_type=F32)
        return (s if bias is None else s + bias), v_ones

    ahead = logits(0) if skew else None
    for n in range(len(problems)):
        s, v_ones = ahead if skew else logits(n)
        if skew and n + 1 < len(problems):
            ahead = logits(n + 1)
        m = jnp.max(s, axis=-1, keepdims=True)
        p = jnp.exp2(s - m).astype(BF16)
        ol = jnp.dot(p, v_ones, preferred_element_type=F32)
        l = ol[:, LANES:]
        consume(n, ol[:, :LANES] / l, (m + jnp.log2(l)) if want_lse else None)


def _na_mem_kernel(q_ref, k_ref, v_ref, *rest, rows):
    bias_refs = rest[:NA_STEP_BLOCKS]
    qm_ref, km_ref, vm_ref, o_ref, om_ref = rest[NA_STEP_BLOCKS:]
    i = pl.program_id(1)
    first = lax.broadcasted_iota(jnp.int32, (1, LANES), 1) < HEAD_DIM
    n_tiles = W_NA // LANES
    n_mem = W_MEM // LANES

    def na_problem(sb, p, hh):
        srow = jnp.clip((i * NA_STEP_BLOCKS + sb) * NA_QROWS - NA_KH // 2, 0, rows - NA_KROWS)
        start = pl.multiple_of(srow * GRID_W, GRID_W)
        sl = slice(p * LANES, (p + 1) * LANES)
        q = q_ref[sb * NA_QB:(sb + 1) * NA_QB, sl]
        zero = jnp.zeros_like(q)
        q = jnp.where(first, q, zero) if hh == 0 else jnp.where(first, zero, q)
        return (q, k_ref[pl.ds(start, NA_KB), sl], bias_refs[sb][0, 2 * p + hh],
                _with_ones(v_ref[pl.ds(start, NA_KB), sl]))

    def mem_problem(sb, c):
        sl = slice(c * LANES, (c + 1) * LANES)
        return (_stack_heads(qm_ref[sb * NA_QB:(sb + 1) * NA_QB, sl]), km_ref[:, sl], None,
                _with_ones(vm_ref[:, sl]))

    problems, targets = [], []
    for sb in range(NA_STEP_BLOCKS):
        for p in range(n_tiles):
            for hh in range(2):
                problems.append(functools.partial(na_problem, sb, p, hh))
                targets.append(("na", sb, p, hh))
        for c in range(n_mem):
            problems.append(functools.partial(mem_problem, sb, c))
            targets.append(("mem", sb, c, None))
    held = {}

    def consume(n, o, _):
        kind, sb, t, hh = targets[n]
        qrows = slice(sb * NA_QB, (sb + 1) * NA_QB)
        sl = slice(t * LANES, (t + 1) * LANES)
        if kind == "mem":
            om_ref[qrows, sl] = jnp.where(first, o[:NA_QB], o[NA_QB:]).astype(BF16)
        elif hh == 0:
            held[(sb, t)] = o
        else:
            o_ref[qrows, sl] = jnp.where(first, held.pop((sb, t)), o).astype(BF16)

    _attend_all(problems, consume, False, skew=True)


def _na_mem_attention(q, k, v, bias, layer, qm, km, vm, batch, seq_len, mem_len):
    rows = seq_len // GRID_W
    nblk = seq_len // NA_QB
    step = NA_STEP_BLOCKS * NA_QB
    q3, k3, v3 = (a.reshape(batch, seq_len, W_NA) for a in (q, k, v))

    def bias_spec(sb):
        def index(b, i):
            blk = i * NA_STEP_BLOCKS + sb
            return (layer, (blk > 0).astype(jnp.int32) + (blk == nblk - 1).astype(jnp.int32), 0, 0, 0)
        return pl.BlockSpec((None, 1, H_NA, NA_QB, NA_KB), index)

    o, om = pl.pallas_call(
        functools.partial(_na_mem_kernel, rows=rows),
        grid=(batch, nblk // NA_STEP_BLOCKS),
        in_specs=[
            pl.BlockSpec((None, step, W_NA), lambda b, i: (b, i, 0)),
            pl.BlockSpec((None, seq_len, W_NA), lambda b, i: (b, 0, 0)),
            pl.BlockSpec((None, seq_len, W_NA), lambda b, i: (b, 0, 0)),
            *[bias_spec(sb) for sb in range(NA_STEP_BLOCKS)],
            pl.BlockSpec((None, step, W_MEM), lambda b, i: (b, i, 0)),
            pl.BlockSpec((None, mem_len, W_MEM), lambda b, i: (b, 0, 0)),
            pl.BlockSpec((None, mem_len, W_MEM), lambda b, i: (b, 0, 0)),
        ],
        out_specs=[pl.BlockSpec((None, step, W_NA), lambda b, i: (b, i, 0)),
                   pl.BlockSpec((None, step, W_MEM), lambda b, i: (b, i, 0))],
        out_shape=[jax.ShapeDtypeStruct((batch, seq_len, W_NA), BF16),
                   jax.ShapeDtypeStruct((batch, seq_len, W_MEM), BF16)],
        compiler_params=_cparams(("parallel", "arbitrary")),
        name="na_mem_attention",
    )(q3, k3, v3, *([bias] * NA_STEP_BLOCKS), qm.reshape(batch, seq_len, W_MEM),
      km.reshape(batch, mem_len, W_MEM), vm.reshape(batch, mem_len, W_MEM))
    return o.reshape(batch * seq_len, W_NA), om.reshape(batch * seq_len, W_MEM)


def _na_bias_tables(rpb, rows):
    depth = rpb.shape[0]
    nblk = rows // NA_QROWS
    qc = np.arange(GRID_W)[:, None]
    kc = np.arange(GRID_W)[None, :]
    cs = np.clip(qc - NA_KW // 2, 0, GRID_W - NA_KW)
    col_valid = (kc >= cs) & (kc < cs + NA_KW)
    col_rel = kc - qc + (NA_KW - 1)
    col_sel = (col_rel[None] == np.arange(2 * NA_KW - 1)[:, None, None]).astype(np.float32)
    t1 = jnp.einsum('lhab,bqk->lhaqk', rpb, jnp.asarray(col_sel), precision=lax.Precision.HIGHEST)
    t1 = jnp.where(jnp.asarray(col_valid), t1 * LOG2E, NEG_INF)
    masked = jnp.full((depth, H_NA, GRID_W, GRID_W), NEG_INF, F32)
    tables = []
    for blk in (0, 1, nblk - 1):
        r0 = blk * NA_QROWS
        srow = int(np.clip(r0 - NA_KH // 2, 0, rows - NA_KROWS))
        slabs = []
        for qr in range(r0, r0 + NA_QROWS):
            rs = int(np.clip(qr - NA_KH // 2, 0, rows - NA_KH))
            slabs.append(jnp.concatenate(
                [t1[:, :, kr - qr + NA_KH - 1] if rs <= kr < rs + NA_KH else masked
                 for kr in range(srow, srow + NA_KROWS)], axis=-1))
        tables.append(jnp.stack(slabs, axis=2))
    return jnp.stack(tables, axis=1).reshape(depth, 3, H_NA, NA_QB, NA_KB)


def _band_kernel(q_ref, k_ref, v_ref, mask_ref, o_ref, lse_ref, *, length):
    j = pl.program_id(1)
    nseq, step = q_ref.shape[0], q_ref.shape[1]
    n_tiles = W_DIL // LANES
    nsub = step // DIL_QB
    first = lax.broadcasted_iota(jnp.int32, (1, LANES), 1) < HEAD_DIM
    lane = lax.broadcasted_iota(jnp.int32, (1, LANES), 1)

    def problem(g, sb, p):
        q0 = j * step + sb * DIL_QB
        start = pl.multiple_of(jnp.clip(q0 - DIL_HALF, 0, length - DIL_KB), DIL_HALF)
        variant = (q0 > 0).astype(jnp.int32) + (q0 == length - DIL_QB).astype(jnp.int32)
        mask = mask_ref[variant]
        sl = slice(p * LANES, (p + 1) * LANES)
        return (_stack_heads(q_ref[g, sb * DIL_QB:(sb + 1) * DIL_QB, sl]), k_ref[g, pl.ds(start, DIL_KB), sl],
                jnp.concatenate([mask, mask], axis=0), _with_ones(v_ref[g, pl.ds(start, DIL_KB), sl]))

    keys = [(g, sb, p) for g in range(nseq) for sb in range(nsub) for p in range(n_tiles)]
    problems = [functools.partial(problem, *key) for key in keys]
    lse_heads = {}

    def consume(n, o, lse):
        g, sb, p = keys[n]
        rows = slice(sb * DIL_QB, (sb + 1) * DIL_QB)
        o_ref[g, rows, p * LANES:(p + 1) * LANES] = jnp.where(first, o[:DIL_QB], o[DIL_QB:]).astype(BF16)
        acc = lse_heads.pop((g, sb), jnp.zeros((DIL_QB, LANES), F32))
        acc = jnp.where(lane == 2 * p, lse[:DIL_QB], jnp.where(lane == 2 * p + 1, lse[DIL_QB:], acc))
        if p == n_tiles - 1:
            lse_ref[g, rows, :] = acc
        else:
            lse_heads[(g, sb)] = acc

    _attend_all(problems, consume, True, skew=False)


def _band_attention(q, k, v, masks):
    nseq, length, _ = q.shape
    step = min(DIL_STEP, length)
    group = min(DIL_STEP // step, nseq)
    nblk = length // step
    blk = lambda w: pl.BlockSpec((group, step, w), lambda s, j: (s, j, 0))
    seq = pl.BlockSpec((group, length, W_DIL), lambda s, j: (s, 0, 0))
    return pl.pallas_call(
        functools.partial(_band_kernel, length=length),
        grid=(nseq // group, nblk),
        in_specs=[blk(W_DIL), seq, seq, pl.BlockSpec((3, DIL_QB, DIL_KB), lambda s, j: (0, 0, 0))],
        out_specs=[blk(W_DIL), blk(LANES)],
        out_shape=[jax.ShapeDtypeStruct((nseq, length, W_DIL), BF16),
                   jax.ShapeDtypeStruct((nseq, length, LANES), F32)],
        compiler_params=_cparams(("parallel", "arbitrary")),
        name="band_attention",
    )(q, k, v, masks)


def _band_masks():
    qq = np.arange(DIL_QB)[:, None]
    kk = np.arange(DIL_KB)[None, :]
    out = []
    for off in (0, -DIL_HALF, DIL_QB - DIL_KB):
        rel = kk + off - qq
        out.append(np.where(np.abs(rel) <= DIL_HALF, 0.0, NEG_INF))
    return jnp.asarray(np.stack(out), F32)


def _memkv_kernel(mem_ref, g_ref, w_ref, gk_ref, seg_ref, km_ref, vm_ref):
    mn = _rms(mem_ref[...], g_ref[...]).astype(BF16)
    kv = jnp.dot(mn, w_ref[...], preferred_element_type=F32)
    for c in range(W_MEM // LANES):
        x = kv[:, c * LANES:(c + 1) * LANES]
        ss = jnp.dot((x * x).astype(BF16), seg_ref[:LANES, :LANES], preferred_element_type=F32)
        x = x * lax.rsqrt(ss * (1.0 / HEAD_DIM) + RMS_EPS) * gk_ref[...]
        km_ref[:, c * LANES:(c + 1) * LANES] = x.astype(BF16)
    vm_ref[...] = kv[:, W_MEM:].astype(BF16)


def _memkv(mem2, g_mem, w_kv_bf16, gk_row, seg):
    rows = mem2.shape[0]
    full = lambda shape: pl.BlockSpec(shape, lambda i: (0,) * len(shape))
    return pl.pallas_call(
        _memkv_kernel,
        grid=(1,),
        in_specs=[full((rows, D_MODEL)), full((1, D_MODEL)), full((D_MODEL, 2 * W_MEM)),
                  full((1, LANES)), full((2 * LANES, 2 * LANES))],
        out_specs=[full((rows, W_MEM)), full((rows, W_MEM))],
        out_shape=[jax.ShapeDtypeStruct((rows, W_MEM), BF16)] * 2,
        compiler_params=_cparams(("arbitrary",)),
        name="mem_kv",
    )(mem2, g_mem, w_kv_bf16, gk_row, seg)


def _outproj_kernel(ona_ref, *rest, with_router):
    nd = 2 * len(DILATIONS)
    dil_refs, rest = rest[:nd], rest[nd:]
    om_ref, h_ref, go_ref, w_ref, gf_ref = rest[:5]
    rest = rest[5:]
    if with_router:
        wr_ref, hn_ref, u_ref, lg_ref = rest[:4]
        stage_refs = rest[4:]
    else:
        hn_ref, u_ref = rest[:2]
        stage_refs = rest[2:]
    stage_refs, tmp_ref = stage_refs[:-1], stage_refs[-1]
    tm = h_ref.shape[0]
    f = _RESIDUE_DILS[0]
    nc = W_DIL // LANES
    go = go_ref[...]
    spread = (lax.broadcasted_iota(jnp.int32, (LANES, W_DIL), 0)
              == lax.broadcasted_iota(jnp.int32, (LANES, W_DIL), 1) // HEAD_DIM).astype(BF16)

    def chain(r0, th):
        rows = slice(r0, r0 + th)

        def token_order(ref, d, st, ncols):
            if d == 1:
                return ref[rows, :].astype(F32)
            for c in range(ncols):
                cols = slice(c * LANES, (c + 1) * LANES)
                for r in range(f):
                    if d == f:
                        st[c, pl.ds(r0 + r, th // f, stride=f), :] = ref[
                            r, r0 // f:(r0 + th) // f, cols].astype(F32)
                    else:
                        for r2 in range(f):
                            tmp_ref[c * f + r, pl.ds(r0 // f + r2, th // d, stride=f), :] = ref[
                                r + f * r2, r0 // d:(r0 + th) // d, cols].astype(F32)
                        st[c, pl.ds(r0 + r, th // f, stride=f), :] = tmp_ref[
                            c * f + r, r0 // f:(r0 + th) // f, :]
            return jnp.concatenate([st[c, rows, :] for c in range(ncols)], axis=-1)

        def per_head_to_lanes(w):
            w_hi = w.astype(BF16)
            w_lo = (w - w_hi.astype(F32)).astype(BF16)
            return (jnp.dot(w_hi, spread, preferred_element_type=F32)
                    + jnp.dot(w_lo, spread, preferred_element_type=F32))

        os_, ls_ = [], []
        for di, d in enumerate(DILATIONS):
            o_ref_d, l_ref_d = dil_refs[2 * di:2 * di + 2]
            o_st, l_st = (stage_refs[2 * (di - 1)], stage_refs[2 * (di - 1) + 1]) if d > 1 else (None, None)
            os_.append(token_order(o_ref_d, d, o_st, nc))
            ls_.append(token_order(l_ref_d, d, l_st, 1))
        mx = functools.reduce(jnp.maximum, ls_)
        es = [jnp.exp2(l - mx) for l in ls_]
        inv = 1.0 / sum(es)
        odil = sum(per_head_to_lanes(e * inv) * o for e, o in zip(es, os_))
        mixed = jnp.concatenate([
            _rms(ona_ref[rows, :].astype(F32), go[:, :W_NA]),
            _rms(odil, go[:, W_NA:W_NA + W_DIL]),
            _rms(om_ref[rows, :].astype(F32), go[:, W_NA + W_DIL:]),
        ], axis=-1).astype(BF16)
        yield
        hn = h_ref[rows, :] + jnp.dot(mixed, w_ref[...], preferred_element_type=F32)
        hn_ref[rows, :] = hn
        yield
        u = _rms(hn, gf_ref[...])
        if with_router:
            u_ref[rows, :] = u
            u_hi = u.astype(BF16)
            u_lo = (u - u_hi.astype(F32)).astype(BF16)
            lg_ref[rows, :] = (jnp.dot(u_hi, wr_ref[...], preferred_element_type=F32)
                               + jnp.dot(u_lo, wr_ref[...], preferred_element_type=F32))
        else:
            u_ref[rows, :] = u.astype(BF16)

    th = tm // PROJ_CHAINS
    chains = [chain(ci * th, th) for ci in range(PROJ_CHAINS)]
    while chains:
        for ch in list(chains):
            if next(ch, "done") == "done":
                chains.remove(ch)


def _outproj(ona, dil, om, h, g_out, w_out_bf16, g_ffn, w_router_pad, batch, seq_len):
    n = h.shape[0]
    tm = TM_PROJ
    tblocks = seq_len // tm
    with_router = w_router_pad is not None
    row = lambda w: pl.BlockSpec((tm, w), lambda i: (i, 0))
    full = lambda shape: pl.BlockSpec(shape, lambda i: (0,) * len(shape))
    in_specs = [row(W_NA)]
    args = [ona]
    scratch = []
    for di, d in enumerate(DILATIONS):
        for a, w in zip(dil[2 * di:2 * di + 2], (W_DIL, LANES)):
            if d == 1:
                in_specs.append(row(w))
                args.append(a.reshape(n, w))
            else:
                in_specs.append(pl.BlockSpec((None, d, tm // d, w),
                                             lambda i: (i // tblocks, 0, i % tblocks, 0)))
                args.append(a.reshape(batch, d, seq_len // d, w))
                scratch.append(pltpu.VMEM((w // LANES, tm, LANES), F32))
    f = _RESIDUE_DILS[0]
    scratch.append(pltpu.VMEM((W_DIL // LANES * f, tm // f, LANES), F32))
    in_specs += [row(W_MEM), row(D_MODEL), full((1, D_MODEL)), full((D_MODEL, D_MODEL)),
                 full((1, D_MODEL))]
    args += [om, h, g_out, w_out_bf16, g_ffn]
    out_specs = [row(D_MODEL), row(D_MODEL)]
    out_shape = [jax.ShapeDtypeStruct((n, D_MODEL), F32),
                 jax.ShapeDtypeStruct((n, D_MODEL), F32 if with_router else BF16)]
    if with_router:
        in_specs.append(full((D_MODEL, LANES)))
        args.append(w_router_pad)
        out_specs.append(row(LANES))
        out_shape.append(jax.ShapeDtypeStruct((n, LANES), F32))
    return pl.pallas_call(
        functools.partial(_outproj_kernel, with_router=with_router),
        grid=(n // tm,),
        in_specs=in_specs,
        out_specs=out_specs,
        out_shape=out_shape,
        scratch_shapes=scratch,
        compiler_params=_cparams(("parallel",)),
        name="outproj",
    )(*args)


def _swiglu(x, wg, wu, wd):
    g = jnp.dot(x, wg, preferred_element_type=F32)
    u = jnp.dot(x, wu, preferred_element_type=F32)
    hm = (g * jax.nn.sigmoid(g) * u).astype(BF16)
    return jnp.dot(hm, wd, preferred_element_type=F32)


def _dense_ffn_kernel(u_ref, h_ref, wg_ref, wu_ref, wd_ref, o_ref, hm_ref):
    x = u_ref[...]
    for c in range(hm_ref.shape[1] // TF_DENSE):
        cols = slice(c * TF_DENSE, (c + 1) * TF_DENSE)
        g = jnp.dot(x, wg_ref[:, cols], preferred_element_type=F32)
        u = jnp.dot(x, wu_ref[:, cols], preferred_element_type=F32)
        hm_ref[:, cols] = (g * jax.nn.sigmoid(g) * u).astype(BF16)
    o_ref[...] = h_ref[...] + jnp.dot(hm_ref[...], wd_ref[...], preferred_element_type=F32)


def _dense_ffn(u, h, wg_bf16, wu_bf16, wd_bf16):
    n = h.shape[0]
    d_ff = wg_bf16.shape[1]
    tm = TM_FFN
    full = lambda shape: pl.BlockSpec(shape, lambda i: (0,) * len(shape))
    return pl.pallas_call(
        _dense_ffn_kernel,
        grid=(n // tm,),
        in_specs=[
            pl.BlockSpec((tm, D_MODEL), lambda i: (i, 0)),
            pl.BlockSpec((tm, D_MODEL), lambda i: (i, 0)),
            full((D_MODEL, d_ff)), full((D_MODEL, d_ff)), full((d_ff, D_MODEL)),
        ],
        out_specs=pl.BlockSpec((tm, D_MODEL), lambda i: (i, 0)),
        out_shape=jax.ShapeDtypeStruct((n, D_MODEL), F32),
        scratch_shapes=[pltpu.VMEM((tm, d_ff), BF16)],
        compiler_params=_cparams(("parallel",)),
        name="dense_ffn",
    )(u, h, wg_bf16, wu_bf16, wd_bf16)


def _dispatch_kernel(vend_ref, pend_ref, d0_ref, d1_ref, u_ref, x_hbm, zbuf, sem, zsem):
    i = pl.program_id(0)
    tc = u_ref.shape[0]
    zr = zbuf.shape[0]
    n_rows = x_hbm.shape[0]

    def zero_pieces(lo, hi):
        def start(p, carry):
            pltpu.make_async_copy(zbuf, x_hbm.at[pl.ds(pl.multiple_of(p * zr, zr), zr), :], zsem).start()
            return carry

        def wait(p, carry):
            pltpu.make_async_copy(zbuf, x_hbm.at[pl.ds(0, zr), :], zsem).wait()
            return carry

        lax.fori_loop(lo, hi, start, 0)
        lax.fori_loop(lo, hi, wait, 0)

    @pl.when(i == 0)
    def _():
        zbuf[...] = jnp.zeros_like(zbuf)
        for e in range(N_EXPERTS):
            zero_pieces(vend_ref[e] // zr, pend_ref[e] // zr)
        zero_pieces(pend_ref[N_EXPERTS - 1] // zr, n_rows // zr)

    def issue(r, carry):
        src = u_ref.at[pl.ds(r, 1), :]
        pltpu.make_async_copy(src, x_hbm.at[pl.ds(d0_ref[0, 0, r], 1), :], sem).start(priority=0)
        pltpu.make_async_copy(src, x_hbm.at[pl.ds(d1_ref[0, 0, r], 1), :], sem).start(priority=1)
        return carry

    lax.fori_loop(0, tc, issue, 0, unroll=ISSUE_UNROLL)
    for _ in range(TOP_K):
        pltpu.make_async_copy(u_ref, x_hbm.at[pl.ds(0, tc), :], sem).wait()


def _dispatch(u, dest, vend, pend, n_rows):
    n = u.shape[0]
    tc = TC_COMBINE
    nblk = n // tc
    d0 = dest[:, 0].reshape(nblk, 1, tc)
    d1 = dest[:, 1].reshape(nblk, 1, tc)
    smem = pl.BlockSpec((1, 1, tc), lambda i, ve, pe: (i, 0, 0), memory_space=pltpu.SMEM)
    grid_spec = pltpu.PrefetchScalarGridSpec(
        num_scalar_prefetch=2,
        grid=(nblk,),
        in_specs=[smem, smem, pl.BlockSpec((tc, D_MODEL), lambda i, ve, pe: (i, 0))],
        out_specs=pl.BlockSpec(memory_space=pl.ANY),
        scratch_shapes=[pltpu.VMEM((ZERO_ROWS, D_MODEL), F32), pltpu.SemaphoreType.DMA(()),
                        pltpu.SemaphoreType.DMA(())],
    )
    return pl.pallas_call(
        _dispatch_kernel,
        grid_spec=grid_spec,
        out_shape=jax.ShapeDtypeStruct((n_rows, D_MODEL), F32),
        compiler_params=_cparams(("arbitrary",)),
        name="moe_dispatch",
    )(vend, pend, d0, d1, u)


def _moe_ffn_kernel(be_ref, nv_ref, nu_ref, x_ref, wg_ref, wu_ref, wd_ref, y_ref, xb):
    i = pl.program_id(0)
    j = pl.program_id(1)
    active = i < nu_ref[0]
    tm = x_ref.shape[0]
    nchunk = tm // MOE_CHUNK

    def chunk(c, wg, wu, wd):
        rows = slice(c * MOE_CHUNK, (c + 1) * MOE_CHUNK)
        y_ref[rows, :] += _swiglu(xb[rows, :], wg, wu, wd)

    @pl.when(j == 0)
    def _():
        y_ref[...] = jnp.zeros_like(y_ref)

    @pl.when(active)
    def _():
        @pl.when(j == 0)
        def _():
            xb[...] = x_ref[...].astype(BF16)

        nvalid = nv_ref[i]

        @pl.when(nvalid > tm - MOE_CHUNK)
        def _():
            wg, wu, wd = (r[...].astype(BF16) for r in (wg_ref, wu_ref, wd_ref))

            def gate_up(c):
                x = xb[c * MOE_CHUNK:(c + 1) * MOE_CHUNK, :]
                return (jnp.dot(x, wg, preferred_element_type=F32),
                        jnp.dot(x, wu, preferred_element_type=F32))

            ahead = gate_up(0)
            for c in range(nchunk):
                g, u = ahead
                if c + 1 < nchunk:
                    ahead = gate_up(c + 1)
                hm = (g * jax.nn.sigmoid(g) * u).astype(BF16)
                y_ref[c * MOE_CHUNK:(c + 1) * MOE_CHUNK, :] += jnp.dot(hm, wd, preferred_element_type=F32)

        @pl.when(nvalid <= tm - MOE_CHUNK)
        def _():
            wg, wu, wd = (r[...].astype(BF16) for r in (wg_ref, wu_ref, wd_ref))
            for c in range(nchunk):
                @pl.when(c * MOE_CHUNK < nvalid)
                def _():
                    chunk(c, wg, wu, wd)


def _moe_ffn(x_sorted, blk_expert, blk_valid, n_used, wg, wu, wd):
    n_rows = x_sorted.shape[0]
    tm, tf = TM_MOE, TF_MOE
    nblk = n_rows // tm
    d_ff = wg.shape[2]
    nj = d_ff // tf

    def jeff(i, j, nu):
        return jnp.where(i < nu[0], j, nj - 1)

    grid_spec = pltpu.PrefetchScalarGridSpec(
        num_scalar_prefetch=3,
        grid=(nblk, nj),
        in_specs=[
            pl.BlockSpec((tm, D_MODEL), lambda i, j, be, nv, nu: (jnp.minimum(i, nu[0] - 1), 0)),
            pl.BlockSpec((None, D_MODEL, tf), lambda i, j, be, nv, nu: (be[i], 0, jeff(i, j, nu))),
            pl.BlockSpec((None, D_MODEL, tf), lambda i, j, be, nv, nu: (be[i], 0, jeff(i, j, nu))),
            pl.BlockSpec((None, tf, D_MODEL), lambda i, j, be, nv, nu: (be[i], jeff(i, j, nu), 0)),
        ],
        out_specs=pl.BlockSpec((tm, D_MODEL), lambda i, j, be, nv, nu: (i, 0)),
        scratch_shapes=[pltpu.VMEM((tm, D_MODEL), BF16)],
    )
    return pl.pallas_call(
        _moe_ffn_kernel,
        grid_spec=grid_spec,
        out_shape=jax.ShapeDtypeStruct((n_rows, D_MODEL), F32),
        compiler_params=_cparams(("arbitrary", "arbitrary")),
        name="moe_ffn",
    )(blk_expert, blk_valid, n_used, x_sorted, wg, wu, wd)


def _combine_kernel(p0_ref, p1_ref, h_ref, g_ref, y_hbm, o_ref, buf0, buf1, sem):
    tc = buf0.shape[0]

    def issue(r, carry):
        pltpu.make_async_copy(y_hbm.at[pl.ds(p0_ref[0, 0, r], 1), :], buf0.at[pl.ds(r, 1), :],
                              sem.at[0]).start(priority=0)
        pltpu.make_async_copy(y_hbm.at[pl.ds(p1_ref[0, 0, r], 1), :], buf1.at[pl.ds(r, 1), :],
                              sem.at[1]).start(priority=1)
        return carry

    lax.fori_loop(0, tc, issue, 0, unroll=ISSUE_UNROLL)
    pltpu.make_async_copy(y_hbm.at[pl.ds(0, tc), :], buf0, sem.at[0]).wait()
    pltpu.make_async_copy(y_hbm.at[pl.ds(0, tc), :], buf1, sem.at[1]).wait()
    g = g_ref[...]
    o_ref[...] = h_ref[...] + g[:, 0:1] * buf0[...] + g[:, 1:2] * buf1[...]


def _combine(h, y, dest, gates):
    n = h.shape[0]
    tc = TC_COMBINE
    nblk = n // tc
    p0 = dest[:, 0].reshape(nblk, 1, tc)
    p1 = dest[:, 1].reshape(nblk, 1, tc)
    smem = pl.BlockSpec((1, 1, tc), lambda i: (i, 0, 0), memory_space=pltpu.SMEM)
    return pl.pallas_call(
        _combine_kernel,
        grid=(nblk,),
        in_specs=[smem, smem, pl.BlockSpec((tc, D_MODEL), lambda i: (i, 0)),
                  pl.BlockSpec((tc, TOP_K), lambda i: (i, 0)), pl.BlockSpec(memory_space=pl.ANY)],
        out_specs=pl.BlockSpec((tc, D_MODEL), lambda i: (i, 0)),
        out_shape=jax.ShapeDtypeStruct((n, D_MODEL), F32),
        scratch_shapes=[pltpu.VMEM((tc, D_MODEL), F32), pltpu.VMEM((tc, D_MODEL), F32),
                        pltpu.SemaphoreType.DMA((2,))],
        compiler_params=_cparams(("arbitrary",)),
        name="moe_combine",
    )(p0, p1, h, gates, y)


def _route(logits):
    n = logits.shape[0]
    top_logit, top_idx = lax.top_k(logits, TOP_K)
    gates = jax.nn.softmax(top_logit, axis=-1)
    e_flat = top_idx.reshape(n * TOP_K).astype(jnp.int32)
    onehot = (e_flat[:, None] == jnp.arange(N_EXPERTS, dtype=jnp.int32)[None, :]).astype(jnp.int32)
    csum = jnp.cumsum(onehot, axis=0)
    counts = csum[-1]
    padded = (counts + TM_MOE - 1) // TM_MOE * TM_MOE
    pend = jnp.cumsum(padded)
    pstart = pend - padded
    dest = jnp.sum(onehot * (csum - 1 + pstart[None, :]), axis=1).reshape(n, TOP_K)
    n_rows = n * TOP_K + N_EXPERTS * TM_MOE
    nblk = n_rows // TM_MOE
    n_used = pend[-1] // TM_MOE
    blk_start = jnp.minimum(jnp.arange(nblk, dtype=jnp.int32) * TM_MOE, (n_used - 1) * TM_MOE)
    blk_expert = jnp.minimum(
        jnp.sum((blk_start[:, None] >= pend[None, :]).astype(jnp.int32), axis=1), N_EXPERTS - 1)
    blk_valid = jnp.clip(counts[blk_expert] - (blk_start - pstart[blk_expert]), 0, TM_MOE)
    i32 = lambda a: a.astype(jnp.int32)
    return (i32(dest), gates, i32(pstart + counts), i32(pend), i32(blk_expert), i32(blk_valid),
            i32(n_used).reshape(1), n_rows)


def _rope_tables(seq_len):
    half = HEAD_DIM // 2
    inv_freq = jnp.power(ROPE_THETA, -(2.0 / HEAD_DIM) * jnp.arange(half, dtype=F32))
    ang = jnp.arange(seq_len, dtype=F32)[:, None] * inv_freq[None, :]
    cos, sin = jnp.cos(ang), jnp.sin(ang)
    cos_t = jnp.tile(cos, (1, LANES // half))
    sin_t = jnp.tile(jnp.concatenate([-sin, sin], axis=1), (1, LANES // HEAD_DIM))
    return cos_t, sin_t


def kernel(x, mem, g_attn, w_in, g_qk_na, rpb_na, g_qk_dil, g_mem, w_mem_kv, g_qk_mem, g_out, w_out,
           g_ffn, w_gate_dense, w_up_dense, w_down_dense, w_router, w_gate_moe, w_up_moe, w_down_moe):
    batch, seq_len, _ = x.shape
    mem_len = mem.shape[1]
    depth = g_attn.shape[0]
    n = batch * seq_len
    scale = LOG2E * HEAD_DIM ** -0.5

    cos_t, sin_t = _rope_tables(seq_len)
    seg = jnp.asarray(np.kron(np.eye(2 * LANES // HEAD_DIM), np.ones((HEAD_DIM, HEAD_DIM))), BF16)
    band_masks = _band_masks()
    na_bias = _na_bias_tables(rpb_na, seq_len // GRID_W)
    ones = lambda w: jnp.ones((w,), F32)

    h = x.reshape(n, D_MODEL)
    mem2 = mem.reshape(batch * mem_len, D_MODEL)
    for layer in range(depth):
        gq_row = jnp.concatenate([
            jnp.tile(g_qk_na[layer, 0] * scale, H_NA), jnp.tile(g_qk_na[layer, 1], H_NA), ones(W_NA),
            jnp.tile(g_qk_dil[layer, 0] * scale, H_DIL), jnp.tile(g_qk_dil[layer, 1], H_DIL), ones(W_DIL),
            jnp.tile(g_qk_mem[layer, 0] * scale, H_MEM)]).reshape(1, IN_WIDTH)
        proj = _inproj(h, g_attn[layer].reshape(1, D_MODEL), w_in[layer].astype(BF16), gq_row, cos_t, sin_t,
                       seg, batch, seq_len)
        qa, ka, va, qb, kb, vb, qm = proj[:_N_FLAT_OUT]

        dil = []
        for d in DILATIONS:
            if d == 1:
                qkv = [a.reshape(batch, seq_len, W_DIL) for a in (qb, kb, vb)]
            else:
                base = _N_FLAT_OUT + 3 * _RESIDUE_DILS.index(d)
                qkv = [a.reshape(batch * d, seq_len // d, W_DIL) for a in proj[base:base + 3]]
            dil += _band_attention(*qkv, band_masks)

        km, vm = _memkv(mem2, g_mem[layer].reshape(1, D_MODEL), w_mem_kv[layer].astype(BF16),
                        jnp.tile(g_qk_mem[layer, 1], LANES // HEAD_DIM).reshape(1, LANES), seg)
        o_na, o_mem = _na_mem_attention(qa, ka, va, na_bias, layer, qm, km, vm, batch, seq_len, mem_len)

        i = layer // 2
        moe = layer % 2 == 1
        w_router_pad = None
        if moe:
            wr_hi = w_router[i].astype(BF16)
            wr_lo = (w_router[i] - wr_hi.astype(F32)).astype(BF16)
            w_router_pad = jnp.pad(jnp.concatenate([wr_hi, wr_lo], axis=1),
                                   ((0, 0), (0, LANES - 2 * N_EXPERTS)))
        outs = _outproj(o_na, dil, o_mem, h, g_out[layer].reshape(1, D_MODEL), w_out[layer].astype(BF16),
                        g_ffn[layer].reshape(1, D_MODEL), w_router_pad, batch, seq_len)
        if not moe:
            h, u = outs
            h = _dense_ffn(u, h, w_gate_dense[i].astype(BF16), w_up_dense[i].astype(BF16),
                           w_down_dense[i].astype(BF16))
        else:
            h, u, logits = outs
            dest, gates, vend, pend, blk_expert, blk_valid, n_used, n_rows = _route(
                logits[:, :N_EXPERTS] + logits[:, N_EXPERTS:2 * N_EXPERTS])
            x_sorted = _dispatch(u, dest, vend, pend, n_rows)
            y = _moe_ffn(x_sorted, blk_expert, blk_valid, n_used, w_gate_moe[i], w_up_moe[i], w_down_moe[i])
            h = _combine(h, y, dest, gates)
    return h.reshape(batch, seq_len, D_MODEL)
```
